```python
import jax
import jax.numpy as jnp
from jax import lax
import numpy as np

D_MODEL = 4096
BATCH = 2
SEQ = 4096
DEPTH = 2

CHUNK = 64
N_META = 16
Q_BLOCK = 128
LN_EPS = 1e-5
RMS_EPS = 1e-6

FOX_HEADS = 8
FOX_HEAD_DIM = 128
FOX_W = FOX_HEADS * FOX_HEAD_DIM
FORGET_BIAS_INIT = 4.0

LRU_W = 2048
LRU_BLOCKS = 16
LRU_BLOCK_DIM = LRU_W // LRU_BLOCKS
CONV_W = 4
LRU_C = 8.0

GLA_HEADS = 4
GLA_DK = 128
GLA_DV = 256
GLA_KW = GLA_HEADS * GLA_DK
GLA_VW = GLA_HEADS * GLA_DV
GLA_RANK = 16
GLA_TAU = 16.0

N_BRANCH = 3
D_MIX = FOX_W + LRU_W + GLA_VW

N_EXPERTS = 32
TOP_K = 4
D_EXPERT = 512
SWIGLU_LIMIT = 7.0
SWIGLU_ALPHA = 1.702

DEEPNORM_ALPHA = (2.0 * DEPTH) ** 0.25
DEEPNORM_BETA = (8.0 * DEPTH) ** -0.25

IN_SPLITS = (FOX_W, FOX_W, FOX_W, FOX_HEADS, LRU_W, LRU_W, GLA_KW, GLA_KW, GLA_VW, GLA_RANK, GLA_VW, N_BRANCH * D_MODEL)
IN_COLS = sum(IN_SPLITS)
FOX_F_OFFSET = 3 * FOX_W

kernel_name = 'hybrid_fox_rglru_gla_moe_deepnorm'


def layer_norm(x, g, b):
    xf = x.astype(jnp.float32)
    mu = jnp.mean(xf, axis=-1, keepdims=True)
    var = jnp.mean(jnp.square(xf - mu), axis=-1, keepdims=True)
    return ((xf - mu) * lax.rsqrt(var + LN_EPS) * g + b).astype(x.dtype)


def forgetting_attention(q, k, v, f_logit):
    L = q.shape[1]
    c = jnp.cumsum(jax.nn.log_sigmoid(f_logit.astype(jnp.float32)), axis=1).transpose(0, 2, 1)
    scale = FOX_HEAD_DIM ** -0.5
    outs = []
    for start in range(0, L, Q_BLOCK):
        end = min(start + Q_BLOCK, L)
        s = jnp.einsum('bqhd,bkhd->bhqk', q[:, start:end], k[:, :end]).astype(jnp.float32) * scale
        s = s + c[:, :, start:end, None] - c[:, :, None, :end]
        mask = np.arange(end)[None, :] <= np.arange(start, end)[:, None]
        p = jax.nn.softmax(jnp.where(mask, s, -jnp.inf), axis=-1).astype(v.dtype)
        outs.append(jnp.einsum('bhqk,bkhd->bqhd', p, v[:, :end]))
    return jnp.concatenate(outs, axis=1)


def causal_depthwise_conv(x, w, b):
    L = x.shape[1]
    xp = jnp.pad(x, ((0, 0), (CONV_W - 1, 0), (0, 0)))
    y = b + xp[:, 0:L] * w[0]
    for j in range(1, CONV_W):
        y = y + xp[:, j:j + L] * w[j]
    return y


def rg_lru(x, w_r, b_r, w_i, b_i, lam):
    B, L, W = x.shape
    xb = x.reshape(B, L, LRU_BLOCKS, LRU_BLOCK_DIM)
    r = jax.nn.sigmoid(jnp.einsum('blnd,nde->blne', xb, w_r).reshape(B, L, W) + b_r)
    i = jax.nn.sigmoid(jnp.einsum('blnd,nde->blne', xb, w_i).reshape(B, L, W) + b_i)
    log_a = -LRU_C * r.astype(jnp.float32) * jax.nn.softplus(-lam.astype(jnp.float32))
    a = jnp.exp(log_a)
    u = jnp.sqrt(-jnp.expm1(2.0 * log_a)) * (i * x).astype(jnp.float32)

    def combine(left, right):
        a_l, h_l = left
        a_r, h_r = right
        return a_l * a_r, a_r * h_l + h_r

    _, h = lax.associative_scan(combine, (a, u), axis=1)
    return h.astype(x.dtype)


def gla_chunked(q, k, v, log_alpha):
    B, L = q.shape[:2]
    pad = (-L) % CHUNK
    pw = ((0, 0), (pad, 0), (0, 0), (0, 0))
    q, k, v, log_alpha = [jnp.pad(t, pw) for t in (q, k, v, log_alpha)]
    nc = (L + pad) // CHUNK
    q = q.reshape(B, nc, CHUNK, GLA_HEADS, GLA_DK)
    k = k.reshape(B, nc, CHUNK, GLA_HEADS, GLA_DK)
    v = v.reshape(B, nc, CHUNK, GLA_HEADS, GLA_DV)
    c = jnp.cumsum(log_alpha.reshape(B, nc, CHUNK, GLA_HEADS, GLA_DK), axis=2)
    c_last = c[:, :, -1]
    k_dec = (k.astype(jnp.float32) * jnp.exp(c_last[:, :, None] - c)).astype(v.dtype)
    u = jnp.einsum('bnchk,bnchv->nbhkv', k_dec, v).astype(jnp.float32)
    decay = jnp.exp(c_last).transpose(1, 0, 2, 3)

    def step(S, inp):
        d, u_n = inp
        S = d[..., None] * S + u_n
        return S, S

    S0 = jnp.zeros((B, GLA_HEADS, GLA_DK, GLA_DV), jnp.float32)
    _, S_all = lax.scan(step, S0, (decay, u))
    o = jnp.einsum('bnchk,nbhkv->bnchv', q.astype(jnp.float32), S_all)
    return o.reshape(B, nc * CHUNK, GLA_HEADS, GLA_DV)[:, pad:]


def hybrid_mixer(x, w_in, b_in, conv_w, conv_b, lru_w_r, lru_b_r, lru_w_i, lru_b_i, lru_lambda,
                 gla_w_alpha, gla_b_alpha, gla_norm_g, w_branch, w_out, b_out):
    B, L, _ = x.shape
    proj = x @ w_in + b_in
    (fq, fk, fv, ff, lx, lg, gq, gk, gv, ga, gg, gates) = jnp.split(
        proj, np.cumsum(IN_SPLITS)[:-1].tolist(), axis=-1)

    o_fox = forgetting_attention(fq.reshape(B, L, FOX_HEADS, FOX_HEAD_DIM),
                                 fk.reshape(B, L, FOX_HEADS, FOX_HEAD_DIM),
                                 fv.reshape(B, L, FOX_HEADS, FOX_HEAD_DIM), ff).reshape(B, L, FOX_W)

    xc = causal_depthwise_conv(lx, conv_w, conv_b)
    o_lru = rg_lru(xc, lru_w_r, lru_b_r, lru_w_i, lru_b_i, lru_lambda) * jax.nn.gelu(lg)

    log_alpha = jax.nn.log_sigmoid((ga @ gla_w_alpha + gla_b_alpha).astype(jnp.float32)) / GLA_TAU
    o = gla_chunked(gq.reshape(B, L, GLA_HEADS, GLA_DK) * (GLA_DK ** -0.5),
                    gk.reshape(B, L, GLA_HEADS, GLA_DK),
                    gv.reshape(B, L, GLA_HEADS, GLA_DV),
                    log_alpha.reshape(B, L, GLA_HEADS, GLA_DK))
    o = o * lax.rsqrt(jnp.mean(jnp.square(o), axis=-1, keepdims=True) + RMS_EPS)
    o = o * gla_norm_g.reshape(GLA_HEADS, GLA_DV)
    o_gla = o.reshape(B, L, GLA_VW).astype(x.dtype) * jax.nn.silu(gg)

    y_fox = o_fox @ w_branch[:FOX_W]
    y_lru = o_lru @ w_branch[FOX_W:FOX_W + LRU_W]
    y_gla = o_gla @ w_branch[FOX_W + LRU_W:]
    g = jax.nn.sigmoid(gates).reshape(B, L, N_BRANCH, D_MODEL)
    merged = g[:, :, 0] * y_fox + g[:, :, 1] * y_lru + g[:, :, 2] * y_gla
    return merged @ w_out + b_out


def moe_ffn(x, w_router, b_router, w_up, b_up, w_down, b_down):
    B, L, D = x.shape
    xf = x.reshape(B * L, D)
    logits = (xf @ w_router).astype(jnp.float32) + b_router.astype(jnp.float32)
    top_logit, top_idx = lax.top_k(logits, TOP_K)
    top_w = jax.nn.softmax(top_logit, axis=-1)
    combine = jnp.einsum('nk,nke->en', top_w,
                         jax.nn.one_hot(top_idx, N_EXPERTS, dtype=jnp.float32)).astype(x.dtype)

    def expert(acc, p):
        w_u, b_u, w_d, b_d, c = p
        h = xf @ w_u + b_u
        gate = jnp.minimum(h[:, 0::2], SWIGLU_LIMIT)
        up = jnp.clip(h[:, 1::2], -SWIGLU_LIMIT, SWIGLU_LIMIT)
        act = (up + 1.0) * gate * jax.nn.sigmoid(SWIGLU_ALPHA * gate)
        return acc + c[:, None] * (act @ w_d + b_d), None

    out, _ = lax.scan(expert, jnp.zeros_like(xf), (w_up, b_up, w_down, b_down, combine))
    return out.reshape(B, L, D)


def setup_inputs(seed: int = 0) -> dict:
    key = jax.random.key(seed)
    ks = jax.random.split(key, 32)
    f32 = jnp.float32

    def nrm(i, shape, scale):
        return jax.random.normal(ks[i], shape, f32) * scale

    def gain(i, shape):
        return 1.0 + nrm(i, shape, 0.02)

    a0 = jax.random.uniform(ks[13], (DEPTH, LRU_W), f32, 0.9, 0.999)
    w_branch = jnp.concatenate([nrm(17, (DEPTH, FOX_W, D_MODEL), FOX_W ** -0.5),
                                nrm(18, (DEPTH, LRU_W, D_MODEL), LRU_W ** -0.5),
                                nrm(19, (DEPTH, GLA_VW, D_MODEL), GLA_VW ** -0.5)], axis=1)
    b_in = nrm(6, (DEPTH, IN_COLS), 0.02).at[:, FOX_F_OFFSET:FOX_F_OFFSET + FOX_HEADS].add(FORGET_BIAS_INIT)
    return {
        'x': nrm(0, (BATCH, SEQ, D_MODEL), 1.0),
        'meta_tokens': nrm(1, (N_META, D_MODEL), 1.0),
        'emb_ln_g': gain(2, (D_MODEL,)),
        'emb_ln_b': nrm(3, (D_MODEL,), 0.02),
        'w_in': nrm(5, (DEPTH, D_MODEL, IN_COLS), D_MODEL ** -0.5),
        'b_in': b_in,
        'conv_w': nrm(7, (DEPTH, CONV_W, LRU_W), CONV_W ** -0.5),
        'conv_b': nrm(8, (DEPTH, LRU_W), 0.02),
        'lru_w_r': nrm(9, (DEPTH, LRU_BLOCKS, LRU_BLOCK_DIM, LRU_BLOCK_DIM), LRU_BLOCK_DIM ** -0.5),
        'lru_b_r': nrm(10, (DEPTH, LRU_W), 0.02),
        'lru_w_i': nrm(11, (DEPTH, LRU_BLOCKS, LRU_BLOCK_DIM, LRU_BLOCK_DIM), LRU_BLOCK_DIM ** -0.5),
        'lru_b_i': nrm(12, (DEPTH, LRU_W), 0.02),
        'lru_lambda': jnp.log(a0) - jnp.log1p(-a0),
        'gla_w_alpha': nrm(14, (DEPTH, GLA_RANK, GLA_KW), GLA_RANK ** -0.5),
        'gla_b_alpha': nrm(15, (DEPTH, GLA_KW), 0.02),
        'gla_norm_g': gain(16, (DEPTH, GLA_VW)),
        'w_branch': w_branch,
        'w_out': nrm(20, (DEPTH, D_MODEL, D_MODEL), DEEPNORM_BETA * D_MODEL ** -0.5),
        'b_out': nrm(21, (DEPTH, D_MODEL), 0.02),
        'ln1_g': gain(22, (DEPTH, D_MODEL)),
        'ln1_b': nrm(23, (DEPTH, D_MODEL), 0.02),
        'w_router': nrm(24, (DEPTH, D_MODEL, N_EXPERTS), D_MODEL ** -0.5),
        'b_router': nrm(25, (DEPTH, N_EXPERTS), 0.01),
        'w_up': nrm(26, (DEPTH, N_EXPERTS, D_MODEL, 2 * D_EXPERT), D_MODEL ** -0.5),
        'b_up': nrm(27, (DEPTH, N_EXPERTS, 2 * D_EXPERT), 0.02),
        'w_down': nrm(28, (DEPTH, N_EXPERTS, D_EXPERT, D_MODEL), DEEPNORM_BETA * D_EXPERT ** -0.5),
        'b_down': nrm(29, (DEPTH, N_EXPERTS, D_MODEL), 0.02),
        'ln2_g': gain(30, (DEPTH, D_MODEL)),
        'ln2_b': nrm(31, (DEPTH, D_MODEL), 0.02),
    }


def reference(x, meta_tokens, emb_ln_g, emb_ln_b, w_in, b_in, conv_w, conv_b, lru_w_r, lru_b_r,
              lru_w_i, lru_b_i, lru_lambda, gla_w_alpha, gla_b_alpha, gla_norm_g, w_branch, w_out, b_out,
              ln1_g, ln1_b, w_router, b_router, w_up, b_up, w_down, b_down, ln2_g, ln2_b):
    B = x.shape[0]
    meta = jnp.broadcast_to(meta_tokens[None].astype(x.dtype), (B, N_META, D_MODEL))
    h = layer_norm(jnp.concatenate([meta, x], axis=1), emb_ln_g, emb_ln_b)
    for l in range(DEPTH):
        mix = hybrid_mixer(h, w_in[l], b_in[l], conv_w[l], conv_b[l], lru_w_r[l], lru_b_r[l],
                           lru_w_i[l], lru_b_i[l], lru_lambda[l], gla_w_alpha[l], gla_b_alpha[l],
                           gla_norm_g[l], w_branch[l], w_out[l], b_out[l])
        h = layer_norm(DEEPNORM_ALPHA * h + mix, ln1_g[l], ln1_b[l])
        ffn = moe_ffn(h, w_router[l], b_router[l], w_up[l], b_up[l], w_down[l], b_down[l])
        h = layer_norm(DEEPNORM_ALPHA * h + ffn, ln2_g[l], ln2_b[l])
    return h[:, N_META:]
```

```python
import functools

import jax
import jax.numpy as jnp
from jax import lax
from jax.experimental import pallas as pl
from jax.experimental.pallas import tpu as pltpu

F32 = jnp.float32
BF16 = jnp.bfloat16
HIGHEST = lax.Precision.HIGHEST

ROW_TILE = 128
HEAD_DIM = 128
GLA_DV = 256
GLA_CHUNK = 64
CONV_W = 4
CONV_HALO = 8
LN_EPS = 1e-5
RMS_EPS = 1e-6
LRU_C = 8.0
GLA_TAU = 16.0
SWIGLU_LIMIT = 7.0
SWIGLU_ALPHA = 1.702
TOP_K = 4
MASKED_KEY_BIAS = 1e30
MOE_ROW_TILE = 256
VMEM_LIMIT = 56 * 1024 * 1024


def _pick(n, candidates):
    for c in candidates:
        if n % c == 0:
            return c
    raise ValueError(f"no tile in {candidates} divides {n}")


def _params(n_axes, vmem=VMEM_LIMIT):
    return pltpu.CompilerParams(dimension_semantics=("arbitrary",) * n_axes, vmem_limit_bytes=vmem)


def _layer_norm(v, g, b):
    mu = jnp.mean(v, axis=-1, keepdims=True)
    d = v - mu
    var = jnp.mean(d * d, axis=-1, keepdims=True)
    return d * lax.rsqrt(var + LN_EPS) * g + b


def _log_sigmoid(x):
    return jnp.minimum(x, 0.0) - jnp.log1p(jnp.exp(-jnp.abs(x)))


def _sigmoid(x):
    return 1.0 / (1.0 + jnp.exp(-x))


def _gelu_tanh(x):
    return 0.5 * x * (1.0 + jnp.tanh(0.7978845608028654 * (x + 0.044715 * (x * x * x))))


def _embed_ln_kernel(x_ref, head_ref, g_ref, b_ref, hf_ref, hb_ref, *, pad):
    i = pl.program_id(1)

    @pl.when(i == 0)
    def _():
        y = _layer_norm(head_ref[...], g_ref[...], b_ref[...])
        rows = lax.broadcasted_iota(jnp.int32, (ROW_TILE, 1), 0)
        y = jnp.where(rows >= pad, y, 0.0)
        hf_ref[...] = y
        hb_ref[...] = y.astype(BF16)

    @pl.when(i > 0)
    def _():
        y = _layer_norm(x_ref[0], g_ref[...], b_ref[...])
        hf_ref[...] = y
        hb_ref[...] = y.astype(BF16)


def _embed_ln(x, head, g, b, *, pad, lp):
    B, S, D = x.shape
    nt = lp // ROW_TILE
    T = B * lp
    return pl.pallas_call(
        functools.partial(_embed_ln_kernel, pad=pad),
        grid=(B, nt),
        in_specs=[
            pl.BlockSpec((1, ROW_TILE, D), lambda b, i: (b, jnp.maximum(i - 1, 0), 0)),
            pl.BlockSpec((ROW_TILE, D), lambda b, i: (0, 0)),
            pl.BlockSpec((1, D), lambda b, i: (0, 0)),
            pl.BlockSpec((1, D), lambda b, i: (0, 0)),
        ],
        out_specs=[
            pl.BlockSpec((ROW_TILE, D), lambda b, i: (b * nt + i, 0)),
            pl.BlockSpec((ROW_TILE, D), lambda b, i: (b * nt + i, 0)),
        ],
        out_shape=[jax.ShapeDtypeStruct((T, D), F32), jax.ShapeDtypeStruct((T, D), BF16)],
        compiler_params=_params(2),
        name="embed_ln",
    )(x, head, g, b)


def _mm_kernel(a_ref, w_ref, b_ref, s_ref, o_ref, *scratch):
    if scratch:
        (wbf_ref,) = scratch

        @pl.when(pl.program_id(1) == 0)
        def _():
            wbf_ref[...] = w_ref[...].astype(BF16)

        w = wbf_ref[...]
    else:
        w = w_ref[...]
    acc = jnp.dot(a_ref[...], w, preferred_element_type=F32)
    o_ref[...] = ((acc + b_ref[...]) * s_ref[...]).astype(o_ref.dtype)


def _matmul(a, w, bias, scale, out_dtype, name):
    M, K = a.shape
    N = w.shape[1]
    tm = _pick(M, (1056, 1024, 768, 512, 384, 256, 128))
    tn = _pick(N, (512, 256, 128))
    scratch = [pltpu.VMEM((K, tn), BF16)] if w.dtype != BF16 else []
    return pl.pallas_call(
        _mm_kernel,
        grid=(N // tn, M // tm),
        in_specs=[
            pl.BlockSpec((tm, K), lambda j, i: (i, 0)),
            pl.BlockSpec((K, tn), lambda j, i: (0, j)),
            pl.BlockSpec((1, tn), lambda j, i: (0, j)),
            pl.BlockSpec((1, tn), lambda j, i: (0, j)),
        ],
        out_specs=pl.BlockSpec((tm, tn), lambda j, i: (i, j)),
        out_shape=jax.ShapeDtypeStruct((M, N), out_dtype),
        scratch_shapes=scratch,
        compiler_params=_params(2),
        name=name,
    )(a, w, bias, scale)


def _prep_kernel(hb_ref, ws_ref, bs_ref, wa_ref, ba_ref, ccol_ref, crow_ref, la_ref, carry_ref, *, pad, tt):
    i = pl.program_id(1)

    @pl.when(i == 0)
    def _():
        carry_ref[...] = jnp.zeros_like(carry_ref)

    z = jnp.dot(hb_ref[...], ws_ref[...], preferred_element_type=F32) + bs_ref[...]
    pos = i * tt + lax.broadcasted_iota(jnp.int32, (tt, 1), 0)
    valid = pos >= pad
    la = _log_sigmoid(jnp.dot(z, wa_ref[...], preferred_element_type=F32, precision=HIGHEST) + ba_ref[...])
    la_ref[...] = jnp.where(valid, la * (1.0 / GLA_TAU), 0.0)
    lf = jnp.where(valid, _log_sigmoid(z), 0.0)

    r = lax.broadcasted_iota(jnp.int32, (ROW_TILE, ROW_TILE), 0)
    c = lax.broadcasted_iota(jnp.int32, (ROW_TILE, ROW_TILE), 1)
    tri = (r >= c).astype(F32)
    carry = carry_ref[...]
    for sb in range(tt // ROW_TILE):
        rows = slice(sb * ROW_TILE, (sb + 1) * ROW_TILE)
        cs = jnp.dot(tri, lf[rows], preferred_element_type=F32, precision=HIGHEST) + carry
        carry = cs[ROW_TILE - 1:ROW_TILE]
        ccol_ref[rows, :] = cs
        posr = i * tt + sb * ROW_TILE + lax.broadcasted_iota(jnp.int32, (1, ROW_TILE), 1)
        crow_ref[0, :, rows] = jnp.where(posr >= pad, cs.T[0:8], MASKED_KEY_BIAS)
    carry_ref[...] = carry


def _prep(hb, w_small, b_small, wa_ext, ba, *, B, lp, pad, tt):
    T, D = hb.shape
    KW = wa_ext.shape[1]
    nt = lp // tt
    return pl.pallas_call(
        functools.partial(_prep_kernel, pad=pad, tt=tt),
        grid=(B, nt),
        in_specs=[
            pl.BlockSpec((tt, D), lambda b, i: (b * nt + i, 0)),
            pl.BlockSpec((D, 128), lambda b, i: (0, 0)),
            pl.BlockSpec((1, 128), lambda b, i: (0, 0)),
            pl.BlockSpec((128, KW), lambda b, i: (0, 0)),
            pl.BlockSpec((1, KW), lambda b, i: (0, 0)),
        ],
        out_specs=[
            pl.BlockSpec((tt, 128), lambda b, i: (b * nt + i, 0)),
            pl.BlockSpec((1, 8, tt), lambda b, i: (b, 0, i)),
            pl.BlockSpec((tt, KW), lambda b, i: (b * nt + i, 0)),
        ],
        out_shape=[
            jax.ShapeDtypeStruct((T, 128), F32),
            jax.ShapeDtypeStruct((B, 8, lp), F32),
            jax.ShapeDtypeStruct((T, KW), F32),
        ],
        scratch_shapes=[pltpu.VMEM((1, 128), F32)],
        compiler_params=_params(2),
        name="gate_prep",
    )(hb, w_small, b_small, wa_ext, ba)


def _dot_nt(a, b):
    return lax.dot_general(a, b, (((1,), (1,)), ((), ())), preferred_element_type=F32)


def _fox_kernel(q_ref, k_ref, v_ref, ccol_ref, crow_ref, o_ref, *, heads, tq):
    qi = pl.program_id(1)
    q0 = pl.multiple_of(qi * tq, tq)
    rows = lax.broadcasted_iota(jnp.int32, (tq, tq), 0)
    cols = lax.broadcasted_iota(jnp.int32, (tq, tq), 1)
    causal = cols <= rows
    for h in range(heads):
        sl = slice(h * HEAD_DIM, (h + 1) * HEAD_DIM)
        q = q_ref[:, sl]
        cc = ccol_ref[:, h:h + 1]

        def scores(k0):
            kk = k_ref[pl.ds(k0, tq), sl]
            cr = crow_ref[0, h:h + 1, pl.ds(k0, tq)]
            return _dot_nt(q, kk) + (cc - cr)

        s = jnp.where(causal, scores(q0), -jnp.inf)
        m = jnp.max(s, axis=-1, keepdims=True)
        p = jnp.exp(s - m)
        l = jnp.sum(p, axis=-1, keepdims=True)
        acc = jnp.dot(p.astype(BF16), v_ref[pl.ds(q0, tq), sl], preferred_element_type=F32)

        def body(ki, carry):
            m, l, acc = carry
            k0 = pl.multiple_of(ki * tq, tq)
            s = scores(k0)
            m_new = jnp.maximum(m, jnp.max(s, axis=-1, keepdims=True))
            a = jnp.exp(m - m_new)
            p = jnp.exp(s - m_new)
            l = a * l + jnp.sum(p, axis=-1, keepdims=True)
            acc = a * acc + jnp.dot(p.astype(BF16), v_ref[pl.ds(k0, tq), sl], preferred_element_type=F32)
            return m_new, l, acc

        m, l, acc = lax.fori_loop(0, qi, body, (m, l, acc))
        o_ref[:, sl] = (acc / l).astype(o_ref.dtype)


def _fox(proj, ccol, crow, *, B, lp, fw, off_q, off_k, off_v, tq):
    T = proj.shape[0]
    nq = lp // tq
    heads = fw // HEAD_DIM
    return pl.pallas_call(
        functools.partial(_fox_kernel, heads=heads, tq=tq),
        grid=(B, nq),
        in_specs=[
            pl.BlockSpec((tq, fw), lambda b, i: (b * nq + i, off_q // fw)),
            pl.BlockSpec((lp, fw), lambda b, i: (b, off_k // fw)),
            pl.BlockSpec((lp, fw), lambda b, i: (b, off_v // fw)),
            pl.BlockSpec((tq, 128), lambda b, i: (b * nq + i, 0)),
            pl.BlockSpec((1, 8, lp), lambda b, i: (b, 0, 0)),
        ],
        out_specs=pl.BlockSpec((tq, fw), lambda b, i: (b * nq + i, 0)),
        out_shape=jax.ShapeDtypeStruct((T, fw), BF16),
        compiler_params=_params(2),
        name="fox_attention",
    )(proj, proj, proj, ccol, crow)


def _lru_kernel(lx_ref, lg_ref, cw_ref, cb_ref, wr_ref, br_ref, wi_ref, bi_ref, lam_ref, o_ref,
                ext_ref, a_ref, u_ref, hc_ref, *, pad, tt, nblk):
    i = pl.program_id(1)

    @pl.when(i == 0)
    def _():
        ext_ref[0:CONV_HALO, :] = jnp.zeros((CONV_HALO, ext_ref.shape[1]), F32)
        hc_ref[...] = jnp.zeros_like(hc_ref)

    @pl.when(i > 0)
    def _():
        ext_ref[0:CONV_HALO, :] = ext_ref[tt:tt + CONV_HALO, :]

    pos = i * tt + lax.broadcasted_iota(jnp.int32, (tt, 1), 0)
    valid = pos >= pad
    ext_ref[CONV_HALO:CONV_HALO + tt, :] = jnp.where(valid, lx_ref[...].astype(F32), 0.0)

    sp = jnp.maximum(-lam_ref[...], 0.0) + jnp.log1p(jnp.exp(-jnp.abs(lam_ref[...])))
    for n in range(nblk):
        sl = slice(n * HEAD_DIM, (n + 1) * HEAD_DIM)
        xc = cb_ref[:, sl]
        for j in range(CONV_W):
            start = CONV_HALO - (CONV_W - 1) + j
            xc = xc + cw_ref[j:j + 1, sl] * ext_ref[start:start + tt, sl]
        xb = xc.astype(BF16)
        r = _sigmoid(jnp.dot(xb, wr_ref[n].astype(BF16), preferred_element_type=F32) + br_ref[:, sl])
        g = _sigmoid(jnp.dot(xb, wi_ref[n].astype(BF16), preferred_element_type=F32) + bi_ref[:, sl])
        log_a = (-LRU_C) * r * sp[:, sl]
        a = jnp.exp(log_a)
        a_ref[:, sl] = a
        u = jnp.sqrt(-jnp.tanh(log_a) * (1.0 + a * a)) * (g * xc)
        u_ref[:, sl] = jnp.where(valid, u, 0.0)

    def body(gidx, h):
        r0 = pl.multiple_of(gidx * 8, 8)
        a8 = a_ref[pl.ds(r0, 8), :]
        u8 = u_ref[pl.ds(r0, 8), :]
        outs = []
        for r in range(8):
            h = a8[r:r + 1] * h + u8[r:r + 1]
            outs.append(h)
        u_ref[pl.ds(r0, 8), :] = jnp.concatenate(outs, axis=0)
        return h

    hc_ref[...] = lax.fori_loop(0, tt // 8, body, hc_ref[...])
    o_ref[...] = (u_ref[...] * _gelu_tanh(lg_ref[...].astype(F32))).astype(o_ref.dtype)


def _lru(proj, conv_w, conv_b, w_r, b_r, w_i, b_i, lam, *, B, lp, lw, off_x, off_g, pad, tt):
    T = proj.shape[0]
    nt = lp // tt
    nblk = lw // HEAD_DIM
    row = lambda b, i: (0, 0)
    return pl.pallas_call(
        functools.partial(_lru_kernel, pad=pad, tt=tt, nblk=nblk),
        grid=(B, nt),
        in_specs=[
            pl.BlockSpec((tt, lw), lambda b, i: (b * nt + i, off_x // lw)),
            pl.BlockSpec((tt, lw), lambda b, i: (b * nt + i, off_g // lw)),
            pl.BlockSpec((CONV_W, lw), row),
            pl.BlockSpec((1, lw), row),
            pl.BlockSpec((nblk, HEAD_DIM, HEAD_DIM), lambda b, i: (0, 0, 0)),
            pl.BlockSpec((1, lw), row),
            pl.BlockSpec((nblk, HEAD_DIM, HEAD_DIM), lambda b, i: (0, 0, 0)),
            pl.BlockSpec((1, lw), row),
            pl.BlockSpec((1, lw), row),
        ],
        out_specs=pl.BlockSpec((tt, lw), lambda b, i: (b * nt + i, 0)),
        out_shape=jax.ShapeDtypeStruct((T, lw), BF16),
        scratch_shapes=[
            pltpu.VMEM((tt + CONV_HALO, lw), F32),
            pltpu.VMEM((tt, lw), F32),
            pltpu.VMEM((tt, lw), F32),
            pltpu.VMEM((1, lw), F32),
        ],
        compiler_params=_params(2),
        name="conv_rglru",
    )(proj, proj, conv_w, conv_b, w_r, b_r, w_i, b_i, lam)


def _gla_kernel(q_ref, k_ref, v_ref, gg_ref, la_ref, ng_ref, o_ref, st_ref, *, pad, tt, heads):
    i = pl.program_id(1)

    @pl.when(i == 0)
    def _():
        st_ref[...] = jnp.zeros_like(st_ref)

    r = lax.broadcasted_iota(jnp.int32, (GLA_CHUNK, GLA_CHUNK), 0)
    c = lax.broadcasted_iota(jnp.int32, (GLA_CHUNK, GLA_CHUNK), 1)
    tri = (r >= c).astype(F32)
    for ci in range(tt // GLA_CHUNK):
        rows = slice(ci * GLA_CHUNK, (ci + 1) * GLA_CHUNK)
        pos = i * tt + ci * GLA_CHUNK + lax.broadcasted_iota(jnp.int32, (GLA_CHUNK, 1), 0)
        valid = pos >= pad
        for hd in range(heads):
            ks = slice(hd * HEAD_DIM, (hd + 1) * HEAD_DIM)
            vs = slice(hd * GLA_DV, (hd + 1) * GLA_DV)
            cs = jnp.dot(tri, la_ref[rows, ks], preferred_element_type=F32, precision=HIGHEST)
            cl = cs[GLA_CHUNK - 1:GLA_CHUNK]
            kdec = jnp.where(valid, k_ref[rows, ks].astype(F32) * jnp.exp(cl - cs), 0.0).astype(BF16)
            ut = lax.dot_general(v_ref[rows, vs], kdec, (((0,), (0,)), ((), ())), preferred_element_type=F32)
            st = st_ref[hd] * jnp.exp(cl) + ut
            st_ref[hd] = st
            o = _dot_nt(q_ref[rows, ks], st.astype(BF16))
            o = o * lax.rsqrt(jnp.mean(o * o, axis=-1, keepdims=True) + RMS_EPS) * ng_ref[:, vs]
            gg = gg_ref[rows, vs].astype(F32)
            o_ref[rows, vs] = (o * (gg * _sigmoid(gg))).astype(o_ref.dtype)


def _gla(proj, la, norm_g, *, B, lp, kw, vw, off_q, off_k, off_v, off_g, pad, tt):
    T = proj.shape[0]
    nt = lp // tt
    heads = kw // HEAD_DIM
    return pl.pallas_call(
        functools.partial(_gla_kernel, pad=pad, tt=tt, heads=heads),
        grid=(B, nt),
        in_specs=[
            pl.BlockSpec((tt, kw), lambda b, i: (b * nt + i, off_q // kw)),
            pl.BlockSpec((tt, kw), lambda b, i: (b * nt + i, off_k // kw)),
            pl.BlockSpec((tt, vw), lambda b, i: (b * nt + i, off_v // vw)),
            pl.BlockSpec((tt, vw), lambda b, i: (b * nt + i, off_g // vw)),
            pl.BlockSpec((tt, kw), lambda b, i: (b * nt + i, 0)),
            pl.BlockSpec((1, vw), lambda b, i: (0, 0)),
        ],
        out_specs=pl.BlockSpec((tt, vw), lambda b, i: (b * nt + i, 0)),
        out_shape=jax.ShapeDtypeStruct((T, vw), BF16),
        scratch_shapes=[pltpu.VMEM((heads, GLA_DV, HEAD_DIM), F32)],
        compiler_params=_params(2),
        name="gla_chunked",
    )(proj, proj, proj, proj, la, norm_g)


def _merge_kernel(of_ref, ol_ref, og_ref, w_ref, g0_ref, g1_ref, g2_ref, o_ref, wbf_ref, *, fw, lw):
    @pl.when(pl.program_id(1) == 0)
    def _():
        wbf_ref[...] = w_ref[...].astype(BF16)

    y0 = jnp.dot(of_ref[...], wbf_ref[0:fw, :], preferred_element_type=F32)
    y1 = jnp.dot(ol_ref[...], wbf_ref[fw:fw + lw, :], preferred_element_type=F32)
    y2 = jnp.dot(og_ref[...], wbf_ref[fw + lw:, :], preferred_element_type=F32)
    out = (_sigmoid(g0_ref[...].astype(F32)) * y0 + _sigmoid(g1_ref[...].astype(F32)) * y1
           + _sigmoid(g2_ref[...].astype(F32)) * y2)
    o_ref[...] = out.astype(o_ref.dtype)


def _merge(o_fox, o_lru, o_gla, w_branch, proj, *, off_gates, d):
    T, fw = o_fox.shape
    lw = o_lru.shape[1]
    vw = o_gla.shape[1]
    tm = _pick(T, (1056, 1024, 768, 512, 384, 256, 128))
    tn = _pick(d, (512, 256, 128))
    gate_spec = lambda b: pl.BlockSpec((tm, tn), lambda j, i: (i, (off_gates + b * d) // tn + j))
    return pl.pallas_call(
        functools.partial(_merge_kernel, fw=fw, lw=lw),
        grid=(d // tn, T // tm),
        in_specs=[
            pl.BlockSpec((tm, fw), lambda j, i: (i, 0)),
            pl.BlockSpec((tm, lw), lambda j, i: (i, 0)),
            pl.BlockSpec((tm, vw), lambda j, i: (i, 0)),
            pl.BlockSpec((fw + lw + vw, tn), lambda j, i: (0, j)),
            gate_spec(0), gate_spec(1), gate_spec(2),
        ],
        out_specs=pl.BlockSpec((tm, tn), lambda j, i: (i, j)),
        out_shape=jax.ShapeDtypeStruct((T, d), BF16),
        scratch_shapes=[pltpu.VMEM((fw + lw + vw, tn), BF16)],
        compiler_params=_params(2),
        name="branch_merge",
    )(o_fox, o_lru, o_gla, w_branch, proj, proj, proj)


def _ln_router_kernel(h_ref, mix_ref, g_ref, b_ref, wr_ref, br_ref, hf_ref, hb_ref, idx_ref, wt_ref, *, alpha):
    y = _layer_norm(alpha * h_ref[...] + mix_ref[...], g_ref[...], b_ref[...])
    hf_ref[...] = y
    hb_ref[...] = y.astype(BF16)
    logits = jnp.dot(y, wr_ref[...], preferred_element_type=F32, precision=HIGHEST) + br_ref[...]
    lane = lax.broadcasted_iota(jnp.int32, logits.shape, 1).astype(F32)
    idx = jnp.zeros_like(logits)
    vals = []
    for k in range(TOP_K):
        mx = jnp.max(logits, axis=-1, keepdims=True)
        sel = jnp.min(jnp.where(logits == mx, lane, float(logits.shape[1])), axis=-1, keepdims=True)
        vals.append(mx)
        idx = jnp.where(lane == float(k), sel, idx)
        logits = jnp.where(lane == sel, -jnp.inf, logits)
    es = [jnp.exp(v - vals[0]) for v in vals]
    tot = es[0]
    for e in es[1:]:
        tot = tot + e
    wt = jnp.zeros_like(logits)
    for k in range(TOP_K):
        wt = jnp.where(lane == float(k), es[k] / tot, wt)
    idx_ref[...] = idx.astype(jnp.int32)
    wt_ref[...] = wt


def _ln_router(h, mix, g, b, wr_pad, br_pad, *, alpha):
    T, D = h.shape
    tm = _pick(T, (192, 128))
    blk = pl.BlockSpec((tm, D), lambda i: (i, 0))
    row = pl.BlockSpec((1, D), lambda i: (0, 0))
    small = pl.BlockSpec((tm, 128), lambda i: (i, 0))
    return pl.pallas_call(
        functools.partial(_ln_router_kernel, alpha=alpha),
        grid=(T // tm,),
        in_specs=[blk, blk, row, row, pl.BlockSpec((D, 128), lambda i: (0, 0)), pl.BlockSpec((1, 128), lambda i: (0, 0))],
        out_specs=[blk, blk, small, small],
        out_shape=[
            jax.ShapeDtypeStruct((T, D), F32),
            jax.ShapeDtypeStruct((T, D), BF16),
            jax.ShapeDtypeStruct((T, 128), jnp.int32),
            jax.ShapeDtypeStruct((T, 128), F32),
        ],
        compiler_params=_params(1),
        name="ln_router",
    )(h, mix, g, b, wr_pad, br_pad)


def _row_gather(src_hbm, idx_ref, base, buf_ref, sem, n):
    def body(r, carry):
        tok = idx_ref[base + r]
        pltpu.make_async_copy(src_hbm.at[pl.ds(tok, 1)], buf_ref.at[pl.ds(r, 1)], sem).start()
        return carry

    lax.fori_loop(0, n, body, 0)


def _moe_kernel(te_ref, na_ref, tok_ref, h_hbm, wg_ref, bg_ref, wu_ref, bu_ref, wd_ref, bd_ref, rw_ref,
                y_ref, xbuf_ref, sem_ref, *, tmx):
    i = pl.program_id(0)
    na = na_ref[0]
    slot = lax.rem(i, 2)

    @pl.when(i == 0)
    def _():
        _row_gather(h_hbm, tok_ref, 0, xbuf_ref.at[0], sem_ref.at[0], tmx)

    @pl.when(i + 1 < na)
    def _():
        _row_gather(h_hbm, tok_ref, (i + 1) * tmx, xbuf_ref.at[1 - slot], sem_ref.at[1 - slot], tmx)

    @pl.when(i < na)
    def _():
        pltpu.make_async_copy(h_hbm.at[pl.ds(0, tmx)], xbuf_ref.at[slot], sem_ref.at[slot]).wait()
        x = xbuf_ref[slot].astype(BF16)
        g = jnp.dot(x, wg_ref[0], preferred_element_type=F32) + bg_ref[0]
        u = jnp.dot(x, wu_ref[0], preferred_element_type=F32) + bu_ref[0]
        g = jnp.minimum(g, SWIGLU_LIMIT)
        u = jnp.clip(u, -SWIGLU_LIMIT, SWIGLU_LIMIT)
        act = (u + 1.0) * g * _sigmoid(SWIGLU_ALPHA * g)
        y = jnp.dot(act.astype(BF16), wd_ref[0], preferred_element_type=F32) + bd_ref[0]
        y_ref[...] = y * rw_ref[...]

    @pl.when(i >= na)
    def _():
        y_ref[...] = jnp.zeros_like(y_ref)


def _moe(tile_expert, n_active, row_token, hf, wg, bg, wu, bu, wd, bd, row_w, *, tmx):
    T, D = hf.shape
    E, _, F = wg.shape
    P = row_token.shape[0]
    ntiles = P // tmx
    grid_spec = pltpu.PrefetchScalarGridSpec(
        num_scalar_prefetch=3,
        grid=(ntiles,),
        in_specs=[
            pl.BlockSpec(memory_space=pl.ANY),
            pl.BlockSpec((1, D, F), lambda i, te, na, tok: (te[i], 0, 0)),
            pl.BlockSpec((1, 1, F), lambda i, te, na, tok: (te[i], 0, 0)),
            pl.BlockSpec((1, D, F), lambda i, te, na, tok: (te[i], 0, 0)),
            pl.BlockSpec((1, 1, F), lambda i, te, na, tok: (te[i], 0, 0)),
            pl.BlockSpec((1, F, D), lambda i, te, na, tok: (te[i], 0, 0)),
            pl.BlockSpec((1, 1, D), lambda i, te, na, tok: (te[i], 0, 0)),
            pl.BlockSpec((tmx, 1), lambda i, te, na, tok: (i, 0)),
        ],
        out_specs=pl.BlockSpec((tmx, D), lambda i, te, na, tok: (i, 0)),
        scratch_shapes=[pltpu.VMEM((2, tmx, D), F32), pltpu.SemaphoreType.DMA((2,))],
    )
    return pl.pallas_call(
        functools.partial(_moe_kernel, tmx=tmx),
        grid_spec=grid_spec,
        out_shape=jax.ShapeDtypeStruct((P, D), F32),
        compiler_params=_params(1),
        name="moe_experts",
    )(tile_expert, n_active, row_token, hf, wg, bg, wu, bu, wd, bd, row_w)


def _combine_kernel(pos_ref, y_hbm, h_ref, g_ref, b_ref, hf_ref, hb_ref, buf_ref, sem_ref, *, alpha, tc, nt):
    i = pl.program_id(0)
    slot = lax.rem(i, 2)

    def start(tile, s):
        for k in range(TOP_K):
            _row_gather(y_hbm, pos_ref, (k * nt + tile) * tc, buf_ref.at[s, k], sem_ref.at[s], tc)

    @pl.when(i == 0)
    def _():
        start(0, 0)

    @pl.when(i + 1 < nt)
    def _():
        start(i + 1, 1 - slot)

    for k in range(TOP_K):
        pltpu.make_async_copy(y_hbm.at[pl.ds(0, tc)], buf_ref.at[slot, k], sem_ref.at[slot]).wait()
    ffn = buf_ref[slot, 0]
    for k in range(1, TOP_K):
        ffn = ffn + buf_ref[slot, k]
    y = _layer_norm(alpha * h_ref[...] + ffn, g_ref[...], b_ref[...])
    hf_ref[...] = y
    hb_ref[...] = y.astype(BF16)


def _combine(pos, ys, hf, g, b, *, alpha):
    T, D = hf.shape
    tc = ROW_TILE
    nt = T // tc
    grid_spec = pltpu.PrefetchScalarGridSpec(
        num_scalar_prefetch=1,
        grid=(nt,),
        in_specs=[
            pl.BlockSpec(memory_space=pl.ANY),
            pl.BlockSpec((tc, D), lambda i, pos: (i, 0)),
            pl.BlockSpec((1, D), lambda i, pos: (0, 0)),
            pl.BlockSpec((1, D), lambda i, pos: (0, 0)),
        ],
        out_specs=[pl.BlockSpec((tc, D), lambda i, pos: (i, 0)), pl.BlockSpec((tc, D), lambda i, pos: (i, 0))],
        scratch_shapes=[pltpu.VMEM((2, TOP_K, tc, D), F32), pltpu.SemaphoreType.DMA((2,))],
    )
    return pl.pallas_call(
        functools.partial(_combine_kernel, alpha=alpha, tc=tc, nt=nt),
        grid_spec=grid_spec,
        out_shape=[jax.ShapeDtypeStruct((T, D), F32), jax.ShapeDtypeStruct((T, D), BF16)],
        compiler_params=_params(1),
        name="moe_combine_ln",
    )(pos, ys, hf, g, b)


def _route(top_idx, top_w, valid_tok, n_experts, tmx, n_rows):
    T, K = top_idx.shape
    e = jnp.where(valid_tok[:, None], top_idx, n_experts).reshape(-1)
    order = jnp.argsort(e, stable=True).astype(jnp.int32)
    sizes = jnp.zeros((n_experts + 1,), jnp.int32).at[e].add(1)
    padded = ((sizes + tmx - 1) // tmx) * tmx
    padded = padded.at[n_experts].set(0)
    start = jnp.cumsum(sizes) - sizes
    pstart = jnp.cumsum(padded) - padded
    e_sorted = e[order]
    rank = jnp.arange(T * K, dtype=jnp.int32) - start[e_sorted]
    dest = jnp.where(e_sorted < n_experts, pstart[e_sorted] + rank, n_rows)
    row_token = jnp.zeros((n_rows,), jnp.int32).at[dest].set(order // K, mode="drop")
    row_w = jnp.zeros((n_rows,), F32).at[dest].set(top_w.reshape(-1)[order], mode="drop")
    pos = jnp.zeros((T * K,), jnp.int32).at[order].set(jnp.where(dest < n_rows, dest, 0))
    pos = pos.reshape(T, K).T.reshape(-1)
    tile_end = jnp.cumsum(padded[:n_experts]) // tmx
    n_active = tile_end[-1]
    tiles = jnp.arange(n_rows // tmx, dtype=jnp.int32)
    tile_expert = jnp.searchsorted(tile_end, jnp.minimum(tiles, n_active - 1), side="right").astype(jnp.int32)
    tile_expert = jnp.minimum(tile_expert, n_experts - 1)
    return tile_expert, n_active.reshape(1).astype(jnp.int32), row_token, row_w.reshape(n_rows, 1), pos


def _segments(fw, fh, lw, kw, vw, rank, d):
    names = ("fq", "fk", "fv", "ff", "lx", "lg", "gq", "gk", "gv", "ga", "gg", "gate0", "gate1", "gate2")
    widths = (fw, fw, fw, fh, lw, lw, kw, kw, vw, rank, vw, d, d, d)
    segs, off = {}, 0
    for n, w in zip(names, widths):
        segs[n] = (off, w)
        off += w
    return segs, off


def kernel(x, meta_tokens, emb_ln_g, emb_ln_b, w_in, b_in, conv_w, conv_b, lru_w_r, lru_b_r, lru_w_i, lru_b_i,
           lru_lambda, gla_w_alpha, gla_b_alpha, gla_norm_g, w_branch, w_out, b_out, ln1_g, ln1_b, w_router,
           b_router, w_up, b_up, w_down, b_down, ln2_g, ln2_b):
    B, S, D = x.shape
    n_meta = meta_tokens.shape[0]
    depth = w_in.shape[0]
    L = S + n_meta
    pad = (-L) % ROW_TILE
    lp = L + pad
    assert pad + n_meta == ROW_TILE and S % ROW_TILE == 0
    T = B * lp

    lw = conv_w.shape[2]
    rank, kw = gla_w_alpha.shape[1:]
    vw = gla_norm_g.shape[1]
    n_experts = w_router.shape[2]
    fexp = w_up.shape[3] // 2
    in_cols = w_in.shape[2]
    fh = (in_cols - 2 * lw - 2 * kw - 2 * vw - rank - 3 * D) // (3 * HEAD_DIM + 1)
    fw = fh * HEAD_DIM
    segs, total = _segments(fw, fh, lw, kw, vw, rank, D)
    assert total == in_cols and fh <= 8 and fh + rank <= 128
    alpha = (2.0 * depth) ** 0.25

    big = sorted((n for n in segs if n not in ("ff", "ga")), key=lambda n: -segs[n][1])
    off, noff = {}, 0
    for n in big:
        assert noff % segs[n][1] == 0
        off[n] = noff
        noff += segs[n][1]
    qscale = HEAD_DIM ** -0.5
    col_scale = jnp.concatenate(
        [jnp.full((segs[n][1],), qscale if n in ("fq", "gq") else 1.0, F32) for n in big]).reshape(1, noff)

    def cols(a, n):
        o, w = segs[n]
        return a[..., o:o + w]

    tt = _pick(lp, (384, 256, 128))
    tmx = MOE_ROW_TILE if (T * TOP_K) % MOE_ROW_TILE == 0 and T >= 4096 else 128
    n_valid = B * L
    n_rows = -(-(n_valid * TOP_K + n_experts * (tmx - 1)) // tmx) * tmx

    head = jnp.concatenate([jnp.zeros((pad, D), F32), meta_tokens.astype(F32)], axis=0)
    hf, hb = _embed_ln(x, head, emb_ln_g.reshape(1, D), emb_ln_b.reshape(1, D), pad=pad, lp=lp)
    valid_tok = (jnp.arange(T, dtype=jnp.int32) % lp) >= pad

    for l in range(depth):
        w_big = jnp.concatenate([cols(w_in[l], n) for n in big], axis=1).astype(BF16)
        b_big = jnp.concatenate([cols(b_in[l], n) for n in big]).reshape(1, noff)
        w_small = jnp.concatenate(
            [cols(w_in[l], "ff"), jnp.zeros((D, 8 - fh), F32), cols(w_in[l], "ga"),
             jnp.zeros((D, 128 - 8 - rank), F32)], axis=1).astype(BF16)
        b_small = jnp.concatenate(
            [cols(b_in[l], "ff"), jnp.zeros((8 - fh,), F32), cols(b_in[l], "ga"),
             jnp.zeros((128 - 8 - rank,), F32)]).reshape(1, 128)
        wa_ext = jnp.zeros((128, kw), F32).at[8:8 + rank].set(gla_w_alpha[l])

        proj = _matmul(hb, w_big, b_big, col_scale, BF16, "in_proj")
        ccol, crow, la = _prep(hb, w_small, b_small, wa_ext, gla_b_alpha[l].reshape(1, kw), B=B, lp=lp, pad=pad, tt=tt)
        o_fox = _fox(proj, ccol, crow, B=B, lp=lp, fw=fw, off_q=off["fq"], off_k=off["fk"], off_v=off["fv"], tq=tt)
        o_lru = _lru(proj, conv_w[l], conv_b[l].reshape(1, lw), lru_w_r[l], lru_b_r[l].reshape(1, lw), lru_w_i[l],
                     lru_b_i[l].reshape(1, lw), lru_lambda[l].reshape(1, lw), B=B, lp=lp, lw=lw, off_x=off["lx"],
                     off_g=off["lg"], pad=pad, tt=tt)
        o_gla = _gla(proj, la, gla_norm_g[l].reshape(1, vw), B=B, lp=lp, kw=kw, vw=vw, off_q=off["gq"],
                     off_k=off["gk"], off_v=off["gv"], off_g=off["gg"], pad=pad, tt=tt)
        merged = _merge(o_fox, o_lru, o_gla, w_branch[l], proj, off_gates=off["gate0"], d=D)
        mix = _matmul(merged, w_out[l], b_out[l].reshape(1, D), jnp.ones((1, D), F32), F32, "out_proj")

        wr_pad = jnp.zeros((D, 128), F32).at[:, :n_experts].set(w_router[l])
        br_pad = jnp.full((1, 128), -MASKED_KEY_BIAS, F32).at[0, :n_experts].set(b_router[l])
        hf, hb, top_idx, top_w = _ln_router(hf, mix, ln1_g[l].reshape(1, D), ln1_b[l].reshape(1, D), wr_pad, br_pad,
                                            alpha=alpha)
        tile_expert, n_active, row_token, row_w, pos = _route(
            top_idx[:, :TOP_K], top_w[:, :TOP_K], valid_tok, n_experts, tmx, n_rows)
        wg = w_up[l][:, :, 0::2].astype(BF16)
        wu = w_up[l][:, :, 1::2].astype(BF16)
        bg = b_up[l][:, 0::2].reshape(n_experts, 1, fexp)
        bu = b_up[l][:, 1::2].reshape(n_experts, 1, fexp)
        ys = _moe(tile_expert, n_active, row_token, hf, wg, bg, wu, bu, w_down[l].astype(BF16),
                  b_down[l].reshape(n_experts, 1, D), row_w, tmx=tmx)
        hf, hb = _combine(pos, ys, hf, ln2_g[l].reshape(1, D), ln2_b[l].reshape(1, D), alpha=alpha)

    return hf.reshape(B, lp, D)[:, pad + n_meta:]
```

```python
import functools
import math

import jax
import jax.numpy as jnp
from jax import lax
from jax.experimental import pallas as pl
from jax.experimental.pallas import tpu as pltpu

F32 = jnp.float32
BF16 = jnp.bfloat16
HIGHEST = lax.Precision.HIGHEST

ROW_TILE = 128
HEAD_DIM = 128
GLA_DV = 256
GLA_CHUNK = 64
CONV_W = 4
CONV_HALO = 8
LN_EPS = 1e-5
RMS_EPS = 1e-6
LRU_C = 8.0
GLA_TAU = 16.0
SWIGLU_LIMIT = 7.0
SWIGLU_ALPHA = 1.702
TOP_K = 4
MASKED_KEY_BIAS = 1e30
MOE_ROW_TILE = 256
VMEM_LIMIT = 56 * 1024 * 1024


def _pick(n, candidates):
    for c in candidates:
        if n % c == 0:
            return c
    raise ValueError(f"no tile in {candidates} divides {n}")


def _params(n_axes, vmem=VMEM_LIMIT):
    return pltpu.CompilerParams(dimension_semantics=("arbitrary",) * n_axes, vmem_limit_bytes=vmem)


def _layer_norm(v, g, b):
    mu = jnp.mean(v, axis=-1, keepdims=True)
    d = v - mu
    var = jnp.mean(d * d, axis=-1, keepdims=True)
    return d * lax.rsqrt(var + LN_EPS) * g + b


def _log_sigmoid(x):
    return jnp.minimum(x, 0.0) - jnp.log1p(jnp.exp(-jnp.abs(x)))


def _sigmoid(x):
    return 1.0 / (1.0 + jnp.exp(-x))


def _gelu_tanh(x):
    return 0.5 * x * (1.0 + jnp.tanh(0.7978845608028654 * (x + 0.044715 * (x * x * x))))


def _embed_ln_kernel(x_ref, head_ref, g_ref, b_ref, hf_ref, hb_ref, *, pad):
    i = pl.program_id(1)

    @pl.when(i == 0)
    def _():
        y = _layer_norm(head_ref[...], g_ref[...], b_ref[...])
        rows = lax.broadcasted_iota(jnp.int32, (ROW_TILE, 1), 0)
        y = jnp.where(rows >= pad, y, 0.0)
        hf_ref[...] = y
        hb_ref[...] = y.astype(BF16)

    @pl.when(i > 0)
    def _():
        y = _layer_norm(x_ref[0], g_ref[...], b_ref[...])
        hf_ref[...] = y
        hb_ref[...] = y.astype(BF16)


def _embed_ln(x, head, g, b, *, pad, lp):
    B, S, D = x.shape
    nt = lp // ROW_TILE
    T = B * lp
    return pl.pallas_call(
        functools.partial(_embed_ln_kernel, pad=pad),
        grid=(B, nt),
        in_specs=[
            pl.BlockSpec((1, ROW_TILE, D), lambda b, i: (b, jnp.maximum(i - 1, 0), 0)),
            pl.BlockSpec((ROW_TILE, D), lambda b, i: (0, 0)),
            pl.BlockSpec((1, D), lambda b, i: (0, 0)),
            pl.BlockSpec((1, D), lambda b, i: (0, 0)),
        ],
        out_specs=[
            pl.BlockSpec((ROW_TILE, D), lambda b, i: (b * nt + i, 0)),
            pl.BlockSpec((ROW_TILE, D), lambda b, i: (b * nt + i, 0)),
        ],
        out_shape=[jax.ShapeDtypeStruct((T, D), F32), jax.ShapeDtypeStruct((T, D), BF16)],
        compiler_params=_params(2),
        name="embed_ln",
    )(x, head, g, b)


def _mm_kernel(a_ref, w_ref, b_ref, s_ref, o_ref, *scratch):
    if scratch:
        (wbf_ref,) = scratch

        @pl.when(pl.program_id(1) == 0)
        def _():
            wbf_ref[...] = w_ref[...].astype(BF16)

        w = wbf_ref[...]
    else:
        w = w_ref[...]
    acc = jnp.dot(a_ref[...], w, preferred_element_type=F32)
    o_ref[...] = ((acc + b_ref[...]) * s_ref[...]).astype(o_ref.dtype)


def _matmul(a, w, bias, scale, out_dtype, name):
    M, K = a.shape
    N = w.shape[1]
    tm = _pick(M, (1056, 1024, 768, 512, 384, 256, 128))
    tn = _pick(N, (512, 256, 128))
    scratch = [pltpu.VMEM((K, tn), BF16)] if w.dtype != BF16 else []
    return pl.pallas_call(
        _mm_kernel,
        grid=(N // tn, M // tm),
        in_specs=[
            pl.BlockSpec((tm, K), lambda j, i: (i, 0)),
            pl.BlockSpec((K, tn), lambda j, i: (0, j)),
            pl.BlockSpec((1, tn), lambda j, i: (0, j)),
            pl.BlockSpec((1, tn), lambda j, i: (0, j)),
        ],
        out_specs=pl.BlockSpec((tm, tn), lambda j, i: (i, j)),
        out_shape=jax.ShapeDtypeStruct((M, N), out_dtype),
        scratch_shapes=scratch,
        compiler_params=_params(2),
        name=name,
    )(a, w, bias, scale)


def _mm_shift_kernel(a_ref, wa_ref, wb_ref, b_ref, s_ref, o_ref, wbf_ref, *, shift, chunk):
    tn = o_ref.shape[1]

    @pl.when(pl.program_id(1) == 0)
    def _():
        def body(c, carry):
            r0 = pl.multiple_of(c * chunk, chunk)
            if shift:
                w = jnp.concatenate([wa_ref[pl.ds(r0, chunk), :], wb_ref[pl.ds(r0, chunk), :]], axis=1)
                w = w[:, shift:shift + tn]
            else:
                w = wa_ref[pl.ds(r0, chunk), :]
            wbf_ref[pl.ds(r0, chunk), :] = w.astype(BF16)
            return carry

        lax.fori_loop(0, wa_ref.shape[0] // chunk, body, 0)

    acc = jnp.dot(a_ref[...], wbf_ref[...], preferred_element_type=F32)
    o_ref[...] = ((acc + b_ref[...]) * s_ref[...]).astype(o_ref.dtype)


def _in_proj(a, w, bias, scale, *, start, name):
    M, K = a.shape
    n = bias.shape[1]
    base = (start // 128) * 128
    shift = start - base
    tm = _pick(M, (1056, 1024, 768, 512, 384, 256, 128))
    tn = _pick(math.gcd(n, base), (512, 256, 128))
    return pl.pallas_call(
        functools.partial(_mm_shift_kernel, shift=shift, chunk=256),
        grid=(n // tn, M // tm),
        in_specs=[
            pl.BlockSpec((tm, K), lambda j, i: (i, 0)),
            pl.BlockSpec((K, tn), lambda j, i: (0, base // tn + j)),
            pl.BlockSpec((K, 128), lambda j, i: (0, (base + (j + 1) * tn) // 128)),
            pl.BlockSpec((1, tn), lambda j, i: (0, j)),
            pl.BlockSpec((1, tn), lambda j, i: (0, j)),
        ],
        out_specs=pl.BlockSpec((tm, tn), lambda j, i: (i, j)),
        out_shape=jax.ShapeDtypeStruct((M, n), BF16),
        scratch_shapes=[pltpu.VMEM((K, tn), BF16)],
        compiler_params=_params(2),
        name=name,
    )(a, w, w, bias, scale)


def _prep_kernel(hb_ref, ws_ref, bs_ref, wa_ref, ba_ref, ccol_ref, crow_ref, la_ref, carry_ref, *, pad, tt):
    i = pl.program_id(1)

    @pl.when(i == 0)
    def _():
        carry_ref[...] = jnp.zeros_like(carry_ref)

    z = jnp.dot(hb_ref[...], ws_ref[...], preferred_element_type=F32) + bs_ref[...]
    pos = i * tt + lax.broadcasted_iota(jnp.int32, (tt, 1), 0)
    valid = pos >= pad
    la = _log_sigmoid(jnp.dot(z, wa_ref[...], preferred_element_type=F32, precision=HIGHEST) + ba_ref[...])
    la_ref[...] = jnp.where(valid, la * (1.0 / GLA_TAU), 0.0)
    lf = jnp.where(valid, _log_sigmoid(z), 0.0)

    r = lax.broadcasted_iota(jnp.int32, (ROW_TILE, ROW_TILE), 0)
    c = lax.broadcasted_iota(jnp.int32, (ROW_TILE, ROW_TILE), 1)
    tri = (r >= c).astype(F32)
    carry = carry_ref[...]
    for sb in range(tt // ROW_TILE):
        rows = slice(sb * ROW_TILE, (sb + 1) * ROW_TILE)
        cs = jnp.dot(tri, lf[rows], preferred_element_type=F32, precision=HIGHEST) + carry
        carry = cs[ROW_TILE - 1:ROW_TILE]
        ccol_ref[rows, :] = cs
        posr = i * tt + sb * ROW_TILE + lax.broadcasted_iota(jnp.int32, (1, ROW_TILE), 1)
        crow_ref[0, :, rows] = jnp.where(posr >= pad, cs.T[0:8], MASKED_KEY_BIAS)
    carry_ref[...] = carry


def _prep(hb, w_small, b_small, wa_ext, ba, *, B, lp, pad, tt):
    T, D = hb.shape
    KW = wa_ext.shape[1]
    nt = lp // tt
    return pl.pallas_call(
        functools.partial(_prep_kernel, pad=pad, tt=tt),
        grid=(B, nt),
        in_specs=[
            pl.BlockSpec((tt, D), lambda b, i: (b * nt + i, 0)),
            pl.BlockSpec((D, 128), lambda b, i: (0, 0)),
            pl.BlockSpec((1, 128), lambda b, i: (0, 0)),
            pl.BlockSpec((128, KW), lambda b, i: (0, 0)),
            pl.BlockSpec((1, KW), lambda b, i: (0, 0)),
        ],
        out_specs=[
            pl.BlockSpec((tt, 128), lambda b, i: (b * nt + i, 0)),
            pl.BlockSpec((1, 8, tt), lambda b, i: (b, 0, i)),
            pl.BlockSpec((tt, KW), lambda b, i: (b * nt + i, 0)),
        ],
        out_shape=[
            jax.ShapeDtypeStruct((T, 128), F32),
            jax.ShapeDtypeStruct((B, 8, lp), F32),
            jax.ShapeDtypeStruct((T, KW), F32),
        ],
        scratch_shapes=[pltpu.VMEM((1, 128), F32)],
        compiler_params=_params(2),
        name="gate_prep",
    )(hb, w_small, b_small, wa_ext, ba)


def _dot_nt(a, b):
    return lax.dot_general(a, b, (((1,), (1,)), ((), ())), preferred_element_type=F32)


def _fox_kernel(q_ref, k_ref, v_ref, ccol_ref, crow_ref, o_ref, *, heads, tq):
    qi = pl.program_id(1)
    q0 = pl.multiple_of(qi * tq, tq)
    rows = lax.broadcasted_iota(jnp.int32, (tq, tq), 0)
    cols = lax.broadcasted_iota(jnp.int32, (tq, tq), 1)
    causal = cols <= rows
    for h in range(heads):
        sl = slice(h * HEAD_DIM, (h + 1) * HEAD_DIM)
        q = q_ref[:, sl]
        cc = ccol_ref[:, h:h + 1]

        def scores(k0):
            kk = k_ref[pl.ds(k0, tq), sl]
            cr = crow_ref[0, h:h + 1, pl.ds(k0, tq)]
            return _dot_nt(q, kk) + (cc - cr)

        s = jnp.where(causal, scores(q0), -jnp.inf)
        m = jnp.max(s, axis=-1, keepdims=True)
        p = jnp.exp(s - m)
        l = jnp.sum(p, axis=-1, keepdims=True)
        acc = jnp.dot(p.astype(BF16), v_ref[pl.ds(q0, tq), sl], preferred_element_type=F32)

        def body(ki, carry):
            m, l, acc = carry
            k0 = pl.multiple_of(ki * tq, tq)
            s = scores(k0)
            m_new = jnp.maximum(m, jnp.max(s, axis=-1, keepdims=True))
            a = jnp.exp(m - m_new)
            p = jnp.exp(s - m_new)
            l = a * l + jnp.sum(p, axis=-1, keepdims=True)
            acc = a * acc + jnp.dot(p.astype(BF16), v_ref[pl.ds(k0, tq), sl], preferred_element_type=F32)
            return m_new, l, acc

        m, l, acc = lax.fori_loop(0, qi, body, (m, l, acc))
        o_ref[:, sl] = (acc / l).astype(o_ref.dtype)


def _fox(proj, ccol, crow, *, B, lp, fw, off_q, off_k, off_v, tq):
    T = proj.shape[0]
    nq = lp // tq
    heads = fw // HEAD_DIM
    return pl.pallas_call(
        functools.partial(_fox_kernel, heads=heads, tq=tq),
        grid=(B, nq),
        in_specs=[
            pl.BlockSpec((tq, fw), lambda b, i: (b * nq + i, off_q // fw)),
            pl.BlockSpec((lp, fw), lambda b, i: (b, off_k // fw)),
            pl.BlockSpec((lp, fw), lambda b, i: (b, off_v // fw)),
            pl.BlockSpec((tq, 128), lambda b, i: (b * nq + i, 0)),
            pl.BlockSpec((1, 8, lp), lambda b, i: (b, 0, 0)),
        ],
        out_specs=pl.BlockSpec((tq, fw), lambda b, i: (b * nq + i, 0)),
        out_shape=jax.ShapeDtypeStruct((T, fw), BF16),
        compiler_params=_params(2),
        name="fox_attention",
    )(proj, proj, proj, ccol, crow)


def _lru_kernel(lx_ref, lg_ref, cw_ref, cb_ref, wr_ref, br_ref, wi_ref, bi_ref, lam_ref, o_ref,
                ext_ref, a_ref, u_ref, hc_ref, *, pad, tt, nblk):
    i = pl.program_id(1)

    @pl.when(i == 0)
    def _():
        ext_ref[0:CONV_HALO, :] = jnp.zeros((CONV_HALO, ext_ref.shape[1]), F32)
        hc_ref[...] = jnp.zeros_like(hc_ref)

    @pl.when(i > 0)
    def _():
        ext_ref[0:CONV_HALO, :] = ext_ref[tt:tt + CONV_HALO, :]

    pos = i * tt + lax.broadcasted_iota(jnp.int32, (tt, 1), 0)
    valid = pos >= pad
    ext_ref[CONV_HALO:CONV_HALO + tt, :] = jnp.where(valid, lx_ref[...].astype(F32), 0.0)

    sp = jnp.maximum(-lam_ref[...], 0.0) + jnp.log1p(jnp.exp(-jnp.abs(lam_ref[...])))
    for n in range(nblk):
        sl = slice(n * HEAD_DIM, (n + 1) * HEAD_DIM)
        xc = cb_ref[:, sl]
        for j in range(CONV_W):
            start = CONV_HALO - (CONV_W - 1) + j
            xc = xc + cw_ref[j:j + 1, sl] * ext_ref[start:start + tt, sl]
        xb = xc.astype(BF16)
        r = _sigmoid(jnp.dot(xb, wr_ref[n].astype(BF16), preferred_element_type=F32) + br_ref[:, sl])
        g = _sigmoid(jnp.dot(xb, wi_ref[n].astype(BF16), preferred_element_type=F32) + bi_ref[:, sl])
        log_a = (-LRU_C) * r * sp[:, sl]
        a = jnp.exp(log_a)
        a_ref[:, sl] = a
        u = jnp.sqrt(-jnp.tanh(log_a) * (1.0 + a * a)) * (g * xc)
        u_ref[:, sl] = jnp.where(valid, u, 0.0)

    def body(gidx, h):
        r0 = pl.multiple_of(gidx * 8, 8)
        a8 = a_ref[pl.ds(r0, 8), :]
        u8 = u_ref[pl.ds(r0, 8), :]
        outs = []
        for r in range(8):
            h = a8[r:r + 1] * h + u8[r:r + 1]
            outs.append(h)
        u_ref[pl.ds(r0, 8), :] = jnp.concatenate(outs, axis=0)
        return h

    hc_ref[...] = lax.fori_loop(0, tt // 8, body, hc_ref[...])
    o_ref[...] = (u_ref[...] * _gelu_tanh(lg_ref[...].astype(F32))).astype(o_ref.dtype)


def _lru(proj, conv_w, conv_b, w_r, b_r, w_i, b_i, lam, *, B, lp, lw, off_x, off_g, pad, tt):
    T = proj.shape[0]
    nt = lp // tt
    nblk = lw // HEAD_DIM
    row = lambda b, i: (0, 0)
    return pl.pallas_call(
        functools.partial(_lru_kernel, pad=pad, tt=tt, nblk=nblk),
        grid=(B, nt),
        in_specs=[
            pl.BlockSpec((tt, lw), lambda b, i: (b * nt + i, off_x // lw)),
            pl.BlockSpec((tt, lw), lambda b, i: (b * nt + i, off_g // lw)),
            pl.BlockSpec((CONV_W, lw), row),
            pl.BlockSpec((1, lw), row),
            pl.BlockSpec((nblk, HEAD_DIM, HEAD_DIM), lambda b, i: (0, 0, 0)),
            pl.BlockSpec((1, lw), row),
            pl.BlockSpec((nblk, HEAD_DIM, HEAD_DIM), lambda b, i: (0, 0, 0)),
            pl.BlockSpec((1, lw), row),
            pl.BlockSpec((1, lw), row),
        ],
        out_specs=pl.BlockSpec((tt, lw), lambda b, i: (b * nt + i, 0)),
        out_shape=jax.ShapeDtypeStruct((T, lw), BF16),
        scratch_shapes=[
            pltpu.VMEM((tt + CONV_HALO, lw), F32),
            pltpu.VMEM((tt, lw), F32),
            pltpu.VMEM((tt, lw), F32),
            pltpu.VMEM((1, lw), F32),
        ],
        compiler_params=_params(2),
        name="conv_rglru",
    )(proj, proj, conv_w, conv_b, w_r, b_r, w_i, b_i, lam)


def _gla_kernel(q_ref, k_ref, v_ref, gg_ref, la_ref, ng_ref, o_ref, st_ref, *, pad, tt, heads):
    i = pl.program_id(1)

    @pl.when(i == 0)
    def _():
        st_ref[...] = jnp.zeros_like(st_ref)

    r = lax.broadcasted_iota(jnp.int32, (GLA_CHUNK, GLA_CHUNK), 0)
    c = lax.broadcasted_iota(jnp.int32, (GLA_CHUNK, GLA_CHUNK), 1)
    tri = (r >= c).astype(F32)
    for ci in range(tt // GLA_CHUNK):
        rows = slice(ci * GLA_CHUNK, (ci + 1) * GLA_CHUNK)
        pos = i * tt + ci * GLA_CHUNK + lax.broadcasted_iota(jnp.int32, (GLA_CHUNK, 1), 0)
        valid = pos >= pad
        for hd in range(heads):
            ks = slice(hd * HEAD_DIM, (hd + 1) * HEAD_DIM)
            vs = slice(hd * GLA_DV, (hd + 1) * GLA_DV)
            cs = jnp.dot(tri, la_ref[rows, ks], preferred_element_type=F32, precision=HIGHEST)
            cl = cs[GLA_CHUNK - 1:GLA_CHUNK]
            kdec = jnp.where(valid, k_ref[rows, ks].astype(F32) * jnp.exp(cl - cs), 0.0).astype(BF16)
            ut = lax.dot_general(v_ref[rows, vs], kdec, (((0,), (0,)), ((), ())), preferred_element_type=F32)
            st = st_ref[hd] * jnp.exp(cl) + ut
            st_ref[hd] = st
            o = _dot_nt(q_ref[rows, ks], st.astype(BF16))
            o = o * lax.rsqrt(jnp.mean(o * o, axis=-1, keepdims=True) + RMS_EPS) * ng_ref[:, vs]
            gg = gg_ref[rows, vs].astype(F32)
            o_ref[rows, vs] = (o * (gg * _sigmoid(gg))).astype(o_ref.dtype)


def _gla(proj, proj_g, la, norm_g, *, B, lp, kw, vw, off_q, off_k, off_v, off_g, pad, tt):
    T = proj.shape[0]
    nt = lp // tt
    heads = kw // HEAD_DIM
    return pl.pallas_call(
        functools.partial(_gla_kernel, pad=pad, tt=tt, heads=heads),
        grid=(B, nt),
        in_specs=[
            pl.BlockSpec((tt, kw), lambda b, i: (b * nt + i, off_q // kw)),
            pl.BlockSpec((tt, kw), lambda b, i: (b * nt + i, off_k // kw)),
            pl.BlockSpec((tt, vw), lambda b, i: (b * nt + i, off_v // vw)),
            pl.BlockSpec((tt, vw), lambda b, i: (b * nt + i, off_g // vw)),
            pl.BlockSpec((tt, kw), lambda b, i: (b * nt + i, 0)),
            pl.BlockSpec((1, vw), lambda b, i: (0, 0)),
        ],
        out_specs=pl.BlockSpec((tt, vw), lambda b, i: (b * nt + i, 0)),
        out_shape=jax.ShapeDtypeStruct((T, vw), BF16),
        scratch_shapes=[pltpu.VMEM((heads, GLA_DV, HEAD_DIM), F32)],
        compiler_params=_params(2),
        name="gla_chunked",
    )(proj, proj, proj, proj_g, la, norm_g)


def _merge_kernel(of_ref, ol_ref, og_ref, w_ref, g0_ref, g1_ref, g2_ref, o_ref, wbf_ref, *, fw, lw):
    @pl.when(pl.program_id(1) == 0)
    def _():
        wbf_ref[...] = w_ref[...].astype(BF16)

    y0 = jnp.dot(of_ref[...], wbf_ref[0:fw, :], preferred_element_type=F32)
    y1 = jnp.dot(ol_ref[...], wbf_ref[fw:fw + lw, :], preferred_element_type=F32)
    y2 = jnp.dot(og_ref[...], wbf_ref[fw + lw:, :], preferred_element_type=F32)
    out = (_sigmoid(g0_ref[...].astype(F32)) * y0 + _sigmoid(g1_ref[...].astype(F32)) * y1
           + _sigmoid(g2_ref[...].astype(F32)) * y2)
    o_ref[...] = out.astype(o_ref.dtype)


def _merge(o_fox, o_lru, o_gla, w_branch, proj, *, off_gates, d):
    T, fw = o_fox.shape
    lw = o_lru.shape[1]
    vw = o_gla.shape[1]
    tm = _pick(T, (1056, 1024, 768, 512, 384, 256, 128))
    tn = _pick(math.gcd(d, off_gates), (512, 256, 128))
    gate_spec = lambda b: pl.BlockSpec((tm, tn), lambda j, i: (i, (off_gates + b * d) // tn + j))
    return pl.pallas_call(
        functools.partial(_merge_kernel, fw=fw, lw=lw),
        grid=(d // tn, T // tm),
        in_specs=[
            pl.BlockSpec((tm, fw), lambda j, i: (i, 0)),
            pl.BlockSpec((tm, lw), lambda j, i: (i, 0)),
            pl.BlockSpec((tm, vw), lambda j, i: (i, 0)),
            pl.BlockSpec((fw + lw + vw, tn), lambda j, i: (0, j)),
            gate_spec(0), gate_spec(1), gate_spec(2),
        ],
        out_specs=pl.BlockSpec((tm, tn), lambda j, i: (i, j)),
        out_shape=jax.ShapeDtypeStruct((T, d), BF16),
        scratch_shapes=[pltpu.VMEM((fw + lw + vw, tn), BF16)],
        compiler_params=_params(2),
        name="branch_merge",
    )(o_fox, o_lru, o_gla, w_branch, proj, proj, proj)


def _ln_router_kernel(h_ref, mix_ref, g_ref, b_ref, wr_ref, br_ref, hf_ref, hb_ref, idx_ref, wt_ref, *, alpha):
    y = _layer_norm(alpha * h_ref[...] + mix_ref[...], g_ref[...], b_ref[...])
    hf_ref[...] = y
    hb_ref[...] = y.astype(BF16)
    logits = jnp.dot(y, wr_ref[...], preferred_element_type=F32, precision=HIGHEST) + br_ref[...]
    lane = lax.broadcasted_iota(jnp.int32, logits.shape, 1).astype(F32)
    idx = jnp.zeros_like(logits)
    vals = []
    for k in range(TOP_K):
        mx = jnp.max(logits, axis=-1, keepdims=True)
        sel = jnp.min(jnp.where(logits == mx, lane, float(logits.shape[1])), axis=-1, keepdims=True)
        vals.append(mx)
        idx = jnp.where(lane == float(k), sel, idx)
        logits = jnp.where(lane == sel, -jnp.inf, logits)
    es = [jnp.exp(v - vals[0]) for v in vals]
    tot = es[0]
    for e in es[1:]:
        tot = tot + e
    wt = jnp.zeros_like(logits)
    for k in range(TOP_K):
        wt = jnp.where(lane == float(k), es[k] / tot, wt)
    idx_ref[...] = idx.astype(jnp.int32)
    wt_ref[...] = wt


def _ln_router(h, mix, g, b, wr_pad, br_pad, *, alpha):
    T, D = h.shape
    tm = _pick(T, (192, 128))
    blk = pl.BlockSpec((tm, D), lambda i: (i, 0))
    row = pl.BlockSpec((1, D), lambda i: (0, 0))
    small = pl.BlockSpec((tm, 128), lambda i: (i, 0))
    return pl.pallas_call(
        functools.partial(_ln_router_kernel, alpha=alpha),
        grid=(T // tm,),
        in_specs=[blk, blk, row, row, pl.BlockSpec((D, 128), lambda i: (0, 0)), pl.BlockSpec((1, 128), lambda i: (0, 0))],
        out_specs=[blk, blk, small, small],
        out_shape=[
            jax.ShapeDtypeStruct((T, D), F32),
            jax.ShapeDtypeStruct((T, D), BF16),
            jax.ShapeDtypeStruct((T, 128), jnp.int32),
            jax.ShapeDtypeStruct((T, 128), F32),
        ],
        compiler_params=_params(1),
        name="ln_router",
    )(h, mix, g, b, wr_pad, br_pad)


def _row_gather(src_hbm, idx_ref, base, buf_ref, sem, n):
    def body(r, carry):
        tok = idx_ref[base + r]
        pltpu.make_async_copy(src_hbm.at[pl.ds(tok, 1)], buf_ref.at[pl.ds(r, 1)], sem).start()
        return carry

    lax.fori_loop(0, n, body, 0, unroll=8)


def _moe_kernel(te_ref, na_ref, tok_ref, h_hbm, wup_ref, bup_ref, sel_ref, wd_ref, bd_ref,
                y_ref, xbuf_ref, sem_ref, *, tmx):
    i = pl.program_id(0)
    na = na_ref[0]
    slot = lax.rem(i, 2)

    @pl.when(i == 0)
    def _():
        _row_gather(h_hbm, tok_ref, 0, xbuf_ref.at[0], sem_ref.at[0], tmx)

    @pl.when(i + 1 < na)
    def _():
        _row_gather(h_hbm, tok_ref, (i + 1) * tmx, xbuf_ref.at[1 - slot], sem_ref.at[1 - slot], tmx)

    @pl.when(i < na)
    def _():
        pltpu.make_async_copy(h_hbm.at[pl.ds(0, tmx)], xbuf_ref.at[slot], sem_ref.at[slot]).wait()
        x = xbuf_ref[slot].astype(BF16)
        h = jnp.dot(x, wup_ref[0], preferred_element_type=F32) + bup_ref[0]
        g = jnp.minimum(h, SWIGLU_LIMIT)
        u = jnp.clip(pltpu.roll(h, h.shape[1] - 1, axis=1), -SWIGLU_LIMIT, SWIGLU_LIMIT)
        act = ((u + 1.0) * g * _sigmoid(SWIGLU_ALPHA * g)).astype(BF16)
        act = jnp.dot(act, sel_ref[...], preferred_element_type=F32).astype(BF16)
        y_ref[...] = jnp.dot(act, wd_ref[0], preferred_element_type=F32) + bd_ref[0]

    @pl.when(i >= na)
    def _():
        y_ref[...] = jnp.zeros_like(y_ref)


def _moe(tile_expert, n_active, row_token, hf, wup, bup, sel, wd, bd, *, tmx):
    T, D = hf.shape
    E, _, F2 = wup.shape
    F = F2 // 2
    P = row_token.shape[0]
    ntiles = P // tmx
    grid_spec = pltpu.PrefetchScalarGridSpec(
        num_scalar_prefetch=3,
        grid=(ntiles,),
        in_specs=[
            pl.BlockSpec(memory_space=pl.ANY),
            pl.BlockSpec((1, D, F2), lambda i, te, na, tok: (te[i], 0, 0)),
            pl.BlockSpec((1, 1, F2), lambda i, te, na, tok: (te[i], 0, 0)),
            pl.BlockSpec((F2, F), lambda i, te, na, tok: (0, 0)),
            pl.BlockSpec((1, F, D), lambda i, te, na, tok: (te[i], 0, 0)),
            pl.BlockSpec((1, 1, D), lambda i, te, na, tok: (te[i], 0, 0)),
        ],
        out_specs=pl.BlockSpec((tmx, D), lambda i, te, na, tok: (i, 0)),
        scratch_shapes=[pltpu.VMEM((2, tmx, D), F32), pltpu.SemaphoreType.DMA((2,))],
    )
    return pl.pallas_call(
        functools.partial(_moe_kernel, tmx=tmx),
        grid_spec=grid_spec,
        out_shape=jax.ShapeDtypeStruct((P, D), F32),
        compiler_params=_params(1),
        name="moe_experts",
    )(tile_expert, n_active, row_token, hf, wup, bup, sel, wd, bd)


def _combine_kernel(pos_ref, y_hbm, h_ref, w_ref, g_ref, b_ref, *rest, alpha, tc, nt, final):
    if final:
        out_ref, buf_ref, sem_ref = rest
    else:
        hf_ref, hb_ref, buf_ref, sem_ref = rest
    i = pl.program_id(0)
    slot = lax.rem(i, 2)

    def start(tile, s):
        for k in range(TOP_K):
            _row_gather(y_hbm, pos_ref, (k * nt + tile) * tc, buf_ref.at[s, k], sem_ref.at[s], tc)

    @pl.when(i == 0)
    def _():
        start(0, 0)

    @pl.when(i + 1 < nt)
    def _():
        start(i + 1, 1 - slot)

    for k in range(TOP_K):
        pltpu.make_async_copy(y_hbm.at[pl.ds(0, tc)], buf_ref.at[slot, k], sem_ref.at[slot]).wait()
    ffn = w_ref[:, 0:1] * buf_ref[slot, 0]
    for k in range(1, TOP_K):
        ffn = ffn + w_ref[:, k:k + 1] * buf_ref[slot, k]
    y = _layer_norm(alpha * h_ref[...] + ffn, g_ref[...], b_ref[...])
    if final:
        out_ref[...] = y
    else:
        hf_ref[...] = y
        hb_ref[...] = y.astype(BF16)


def _combine(pos, ys, hf, top_w, g, b, *, alpha, lp, final):
    T, D = hf.shape
    tc = ROW_TILE
    nt = T // tc
    ntb = lp // tc
    if final:
        out_specs = pl.BlockSpec((tc, D), lambda i, pos: ((i // ntb) * (ntb - 1) + jnp.maximum(i % ntb - 1, 0), 0))
        out_shape = jax.ShapeDtypeStruct(((T // lp) * (lp - tc), D), F32)
    else:
        out_specs = [pl.BlockSpec((tc, D), lambda i, pos: (i, 0)), pl.BlockSpec((tc, D), lambda i, pos: (i, 0))]
        out_shape = [jax.ShapeDtypeStruct((T, D), F32), jax.ShapeDtypeStruct((T, D), BF16)]
    grid_spec = pltpu.PrefetchScalarGridSpec(
        num_scalar_prefetch=1,
        grid=(nt,),
        in_specs=[
            pl.BlockSpec(memory_space=pl.ANY),
            pl.BlockSpec((tc, D), lambda i, pos: (i, 0)),
            pl.BlockSpec((tc, 128), lambda i, pos: (i, 0)),
            pl.BlockSpec((1, D), lambda i, pos: (0, 0)),
            pl.BlockSpec((1, D), lambda i, pos: (0, 0)),
        ],
        out_specs=out_specs,
        scratch_shapes=[pltpu.VMEM((2, TOP_K, tc, D), F32), pltpu.SemaphoreType.DMA((2,))],
    )
    return pl.pallas_call(
        functools.partial(_combine_kernel, alpha=alpha, tc=tc, nt=nt, final=final),
        grid_spec=grid_spec,
        out_shape=out_shape,
        compiler_params=_params(1),
        name="moe_combine_ln",
    )(pos, ys, hf, top_w, g, b)


def _route_kernel(idx_ref, pos_ref, te_ref, na_ref, cnt_ref, pst_ref, *, pad, tr, tmx, n_experts, spare_row):
    p = pl.program_id(0)
    i = pl.program_id(2)
    first = jnp.logical_and(pl.program_id(1) == 0, i == 0)

    @pl.when(jnp.logical_and(p == 0, first))
    def _():
        cnt_ref[...] = jnp.zeros_like(cnt_ref)

    @pl.when(jnp.logical_and(p == 1, first))
    def _():
        cnt = cnt_ref[...]
        padded = jnp.floor((cnt + (tmx - 1.0)) * (1.0 / tmx)) * tmx
        r = lax.broadcasted_iota(jnp.int32, (128, 128), 0)
        c = lax.broadcasted_iota(jnp.int32, (128, 128), 1)
        pst = jnp.dot(padded, (r < c).astype(F32), preferred_element_type=F32, precision=HIGHEST)
        pst_ref[...] = pst
        cnt_ref[...] = jnp.zeros_like(cnt_ref)
        tile_end = (pst + padded) * (1.0 / tmx)
        nt_pad = te_ref.shape[0]
        t = lax.broadcasted_iota(jnp.int32, (nt_pad, 128), 0).astype(F32)
        lane = lax.broadcasted_iota(jnp.int32, (nt_pad, 128), 1)
        is_expert = lane < n_experts
        te = jnp.sum(jnp.where(jnp.logical_and(tile_end <= t, is_expert), 1.0, 0.0), axis=-1, keepdims=True)
        lane1 = lax.broadcasted_iota(jnp.int32, (1, 128), 1).astype(F32)
        e_last = jnp.max(jnp.where(cnt > 0.0, lane1, 0.0), axis=-1, keepdims=True)
        te_ref[...] = jnp.broadcast_to(jnp.minimum(te, e_last), (nt_pad, 128)).astype(jnp.int32)
        na_ref[...] = jnp.broadcast_to(jnp.max(tile_end, axis=-1, keepdims=True), (1, 128)).astype(jnp.int32)

    idx = idx_ref[...]
    lane = lax.broadcasted_iota(jnp.int32, (tr, 128), 1)
    valid = (i * tr + lax.broadcasted_iota(jnp.int32, (tr, 1), 0)) >= pad
    onehot = jnp.zeros((tr, 128), F32)
    for k in range(TOP_K):
        onehot = onehot + jnp.where(lane == idx[:, k:k + 1], 1.0, 0.0)
    onehot = jnp.where(valid, onehot, 0.0)
    r = lax.broadcasted_iota(jnp.int32, (tr, tr), 0)
    c = lax.broadcasted_iota(jnp.int32, (tr, tr), 1)
    rank = jnp.dot((r > c).astype(BF16), onehot.astype(BF16), preferred_element_type=F32) + cnt_ref[...]
    cnt_ref[...] = cnt_ref[...] + jnp.sum(onehot, axis=0, keepdims=True)

    @pl.when(p == 0)
    def _():
        pos_ref[...] = jnp.zeros_like(pos_ref)

    @pl.when(p == 1)
    def _():
        dest = rank + pst_ref[...]
        out = jnp.zeros((tr, 128), F32)
        for k in range(TOP_K):
            d = jnp.sum(jnp.where(lane == idx[:, k:k + 1], dest, 0.0), axis=-1, keepdims=True)
            out = jnp.where(lane == k, jnp.where(valid, d, float(spare_row)), out)
        pos_ref[...] = out.astype(jnp.int32)


def _route(top_idx, *, B, lp, pad, tmx, n_experts, n_rows):
    T = top_idx.shape[0]
    tr = _pick(lp, (384, 256, 128))
    nt = lp // tr
    nt_pad = -(-(n_rows // tmx) // 8) * 8
    pos, te, na = pl.pallas_call(
        functools.partial(_route_kernel, pad=pad, tr=tr, tmx=tmx, n_experts=n_experts, spare_row=n_rows - 1),
        grid=(2, B, nt),
        in_specs=[pl.BlockSpec((tr, 128), lambda p, b, i: (b * nt + i, 0))],
        out_specs=[
            pl.BlockSpec((tr, 128), lambda p, b, i: (p * (b * nt + i), 0)),
            pl.BlockSpec((nt_pad, 128), lambda p, b, i: (0, 0)),
            pl.BlockSpec((1, 128), lambda p, b, i: (0, 0)),
        ],
        out_shape=[
            jax.ShapeDtypeStruct((T, 128), jnp.int32),
            jax.ShapeDtypeStruct((nt_pad, 128), jnp.int32),
            jax.ShapeDtypeStruct((1, 128), jnp.int32),
        ],
        scratch_shapes=[pltpu.VMEM((1, 128), F32), pltpu.VMEM((1, 128), F32)],
        compiler_params=_params(3),
        name="route_rank",
    )(top_idx)
    pos_flat = pos[:, :TOP_K].T.reshape(-1)
    return pos_flat, te[:n_rows // tmx, 0], na[0, :1]


def _row_token_kernel(pos_ref, rt_ref, *, t):
    def zero(r, carry):
        rt_ref[r] = 0
        return carry

    lax.fori_loop(0, rt_ref.shape[0], zero, 0, unroll=8)
    for k in range(TOP_K):
        def body(tok, carry):
            rt_ref[pos_ref[k * t + tok]] = tok
            return carry

        lax.fori_loop(0, t, body, 0, unroll=8)


def _row_token(pos_flat, *, t, n_rows):
    return pl.pallas_call(
        functools.partial(_row_token_kernel, t=t),
        grid_spec=pltpu.PrefetchScalarGridSpec(
            num_scalar_prefetch=1, grid=(1,), in_specs=[],
            out_specs=pl.BlockSpec(memory_space=pltpu.SMEM)),
        out_shape=jax.ShapeDtypeStruct((n_rows,), jnp.int32),
        compiler_params=_params(1),
        name="route_row_token",
    )(pos_flat)


def _segments(fw, fh, lw, kw, vw, rank, d):
    names = ("fq", "fk", "fv", "ff", "lx", "lg", "gq", "gk", "gv", "ga", "gg", "gate0", "gate1", "gate2")
    widths = (fw, fw, fw, fh, lw, lw, kw, kw, vw, rank, vw, d, d, d)
    segs, off = {}, 0
    for n, w in zip(names, widths):
        segs[n] = (off, w)
        off += w
    return segs, off


def kernel(x, meta_tokens, emb_ln_g, emb_ln_b, w_in, b_in, conv_w, conv_b, lru_w_r, lru_b_r, lru_w_i, lru_b_i,
           lru_lambda, gla_w_alpha, gla_b_alpha, gla_norm_g, w_branch, w_out, b_out, ln1_g, ln1_b, w_router,
           b_router, w_up, b_up, w_down, b_down, ln2_g, ln2_b):
    B, S, D = x.shape
    n_meta = meta_tokens.shape[0]
    depth = w_in.shape[0]
    L = S + n_meta
    pad = (-L) % ROW_TILE
    lp = L + pad
    assert pad + n_meta == ROW_TILE and S % ROW_TILE == 0
    T = B * lp

    lw = conv_w.shape[2]
    rank, kw = gla_w_alpha.shape[1:]
    vw = gla_norm_g.shape[1]
    n_experts = w_router.shape[2]
    fexp = w_up.shape[3] // 2
    in_cols = w_in.shape[2]
    fh = (in_cols - 2 * lw - 2 * kw - 2 * vw - rank - 3 * D) // (3 * HEAD_DIM + 1)
    fw = fh * HEAD_DIM
    segs, total = _segments(fw, fh, lw, kw, vw, rank, D)
    assert total == in_cols and fh <= 8 and fh + rank <= 128
    alpha = (2.0 * depth) ** 0.25

    windows = (("fq", "fk", "fv"), ("lx", "lg", "gq", "gk", "gv"), ("gg", "gate0", "gate1", "gate2"))
    qscale = HEAD_DIM ** -0.5
    off, win_start, win_scale = {}, [], []
    for names in windows:
        start = segs[names[0]][0]
        for n in names:
            off[n] = segs[n][0] - start
            assert off[n] % segs[n][1] == 0 or n.startswith("gate")
        win_start.append(start)
        win_scale.append(jnp.concatenate(
            [jnp.full((segs[n][1],), qscale if n in ("fq", "gq") else 1.0, F32) for n in names]).reshape(1, -1))

    def cols(a, n):
        o, w = segs[n]
        return a[..., o:o + w]

    tt = _pick(lp, (384, 256, 128))
    tmx = MOE_ROW_TILE if T >= 4096 else 128
    n_rows = (-(-(B * L * TOP_K + n_experts * (tmx - 1)) // tmx) + 1) * tmx

    head = jnp.concatenate([jnp.zeros((pad, D), F32), meta_tokens.astype(F32)], axis=0)
    hf, hb = _embed_ln(x, head, emb_ln_g.reshape(1, D), emb_ln_b.reshape(1, D), pad=pad, lp=lp)
    sel = (jnp.arange(2 * fexp)[:, None] == 2 * jnp.arange(fexp)[None, :]).astype(BF16)

    for l in range(depth):
        w_small = jnp.concatenate(
            [cols(w_in[l], "ff"), jnp.zeros((D, 8 - fh), F32), cols(w_in[l], "ga"),
             jnp.zeros((D, 128 - 8 - rank), F32)], axis=1).astype(BF16)
        b_small = jnp.concatenate(
            [cols(b_in[l], "ff"), jnp.zeros((8 - fh,), F32), cols(b_in[l], "ga"),
             jnp.zeros((128 - 8 - rank,), F32)]).reshape(1, 128)
        wa_ext = jnp.zeros((128, kw), F32).at[8:8 + rank].set(gla_w_alpha[l])

        pa, pb, pc = [
            _in_proj(hb, w_in[l], b_in[l][s:s + sc.shape[1]].reshape(1, -1), sc, start=s, name=f"in_proj_{k}")
            for k, (s, sc) in enumerate(zip(win_start, win_scale))]
        ccol, crow, la = _prep(hb, w_small, b_small, wa_ext, gla_b_alpha[l].reshape(1, kw), B=B, lp=lp, pad=pad, tt=tt)
        o_fox = _fox(pa, ccol, crow, B=B, lp=lp, fw=fw, off_q=off["fq"], off_k=off["fk"], off_v=off["fv"], tq=tt)
        o_lru = _lru(pb, conv_w[l], conv_b[l].reshape(1, lw), lru_w_r[l], lru_b_r[l].reshape(1, lw), lru_w_i[l],
                     lru_b_i[l].reshape(1, lw), lru_lambda[l].reshape(1, lw), B=B, lp=lp, lw=lw, off_x=off["lx"],
                     off_g=off["lg"], pad=pad, tt=tt)
        o_gla = _gla(pb, pc, la, gla_norm_g[l].reshape(1, vw), B=B, lp=lp, kw=kw, vw=vw, off_q=off["gq"],
                     off_k=off["gk"], off_v=off["gv"], off_g=off["gg"], pad=pad, tt=tt)
        merged = _merge(o_fox, o_lru, o_gla, w_branch[l], pc, off_gates=off["gate0"], d=D)
        mix = _matmul(merged, w_out[l], b_out[l].reshape(1, D), jnp.ones((1, D), F32), F32, "out_proj")

        wr_pad = jnp.zeros((D, 128), F32).at[:, :n_experts].set(w_router[l])
        br_pad = jnp.full((1, 128), -MASKED_KEY_BIAS, F32).at[0, :n_experts].set(b_router[l])
        hf, hb, top_idx, top_w = _ln_router(hf, mix, ln1_g[l].reshape(1, D), ln1_b[l].reshape(1, D), wr_pad, br_pad,
                                            alpha=alpha)
        pos, tile_expert, n_active = _route(top_idx, B=B, lp=lp, pad=pad, tmx=tmx, n_experts=n_experts, n_rows=n_rows)
        row_token = _row_token(pos, t=T, n_rows=n_rows)
        ys = _moe(tile_expert, n_active, row_token, hf, w_up[l].astype(BF16), b_up[l].reshape(n_experts, 1, 2 * fexp),
                  sel, w_down[l].astype(BF16), b_down[l].reshape(n_experts, 1, D), tmx=tmx)
        final = l == depth - 1
        res = _combine(pos, ys, hf, top_w, ln2_g[l].reshape(1, D), ln2_b[l].reshape(1, D), alpha=alpha, lp=lp,
                       final=final)
        if final:
            return res.reshape(B, S, D)
        hf, hb = res
```

```python
import functools
import math

import jax
import jax.numpy as jnp
from jax import lax
from jax.experimental import pallas as pl
from jax.experimental.pallas import tpu as pltpu

F32 = jnp.float32
BF16 = jnp.bfloat16
HIGHEST = lax.Precision.HIGHEST

ROW_TILE = 128
HEAD_DIM = 128
GLA_DV = 256
GLA_CHUNK = 64
CONV_W = 4
CONV_HALO = 8
LN_EPS = 1e-5
RMS_EPS = 1e-6
LRU_C = 8.0
GLA_TAU = 16.0
SWIGLU_LIMIT = 7.0
SWIGLU_ALPHA = 1.702
TOP_K = 4
LOG2E = 1.4426950408889634
MASKED_KEY_BIAS = 1e30
MOE_ROW_TILE = 256
VMEM_LIMIT = 56 * 1024 * 1024


def _pick(n, candidates):
    for c in candidates:
        if n % c == 0:
            return c
    raise ValueError(f"no tile in {candidates} divides {n}")


def _params(n_axes, vmem=VMEM_LIMIT):
    return pltpu.CompilerParams(dimension_semantics=("arbitrary",) * n_axes, vmem_limit_bytes=vmem)


def _layer_norm(v, g, b):
    mu = jnp.mean(v, axis=-1, keepdims=True)
    d = v - mu
    var = jnp.mean(d * d, axis=-1, keepdims=True)
    return d * lax.rsqrt(var + LN_EPS) * g + b


def _log_sigmoid(x):
    return jnp.minimum(x, 0.0) - jnp.log1p(jnp.exp(-jnp.abs(x)))


def _sigmoid(x):
    return 1.0 / (1.0 + jnp.exp(-x))


def _gelu_tanh(x):
    return 0.5 * x * (1.0 + jnp.tanh(0.7978845608028654 * (x + 0.044715 * (x * x * x))))


def _embed_ln_kernel(x_ref, head_ref, g_ref, b_ref, hf_ref, hb_ref, *, pad):
    i = pl.program_id(1)

    @pl.when(i == 0)
    def _():
        y = _layer_norm(head_ref[...], g_ref[...], b_ref[...])
        rows = lax.broadcasted_iota(jnp.int32, (ROW_TILE, 1), 0)
        y = jnp.where(rows >= pad, y, 0.0)
        hf_ref[...] = y
        hb_ref[...] = y.astype(BF16)

    @pl.when(i > 0)
    def _():
        y = _layer_norm(x_ref[0], g_ref[...], b_ref[...])
        hf_ref[...] = y
        hb_ref[...] = y.astype(BF16)


def _embed_ln(x, head, g, b, *, pad, lp):
    B, S, D = x.shape
    nt = lp // ROW_TILE
    T = B * lp
    return pl.pallas_call(
        functools.partial(_embed_ln_kernel, pad=pad),
        grid=(B, nt),
        in_specs=[
            pl.BlockSpec((1, ROW_TILE, D), lambda b, i: (b, jnp.maximum(i - 1, 0), 0)),
            pl.BlockSpec((ROW_TILE, D), lambda b, i: (0, 0)),
            pl.BlockSpec((1, D), lambda b, i: (0, 0)),
            pl.BlockSpec((1, D), lambda b, i: (0, 0)),
        ],
        out_specs=[
            pl.BlockSpec((ROW_TILE, D), lambda b, i: (b * nt + i, 0)),
            pl.BlockSpec((ROW_TILE, D), lambda b, i: (b * nt + i, 0)),
        ],
        out_shape=[jax.ShapeDtypeStruct((T, D), F32), jax.ShapeDtypeStruct((T, D), BF16)],
        compiler_params=_params(2),
        name="embed_ln",
    )(x, head, g, b)


def _mm_kernel(a_ref, w_ref, b_ref, s_ref, o_ref, *scratch):
    if scratch:
        (wbf_ref,) = scratch

        @pl.when(pl.program_id(1) == 0)
        def _():
            wbf_ref[...] = w_ref[...].astype(BF16)

        w = wbf_ref[...]
    else:
        w = w_ref[...]
    acc = jnp.dot(a_ref[...], w, preferred_element_type=F32)
    o_ref[...] = ((acc + b_ref[...]) * s_ref[...]).astype(o_ref.dtype)


def _matmul(a, w, layer, bias, scale, out_dtype, name):
    M, K = a.shape
    N = w.shape[2]
    tm = _pick(M, (1056, 1024, 768, 512, 384, 256, 128))
    tn = _pick(N, (512, 256, 128))
    scratch = [pltpu.VMEM((K, tn), BF16)] if w.dtype != BF16 else []
    return pl.pallas_call(
        _mm_kernel,
        grid=(N // tn, M // tm),
        in_specs=[
            pl.BlockSpec((tm, K), lambda j, i: (i, 0)),
            pl.BlockSpec((None, K, tn), lambda j, i: (layer, 0, j)),
            pl.BlockSpec((1, tn), lambda j, i: (0, j)),
            pl.BlockSpec((1, tn), lambda j, i: (0, j)),
        ],
        out_specs=pl.BlockSpec((tm, tn), lambda j, i: (i, j)),
        out_shape=jax.ShapeDtypeStruct((M, N), out_dtype),
        scratch_shapes=scratch,
        compiler_params=_params(2),
        name=name,
    )(a, w, bias, scale)


def _mm_shift_kernel(a_ref, wa_ref, wb_ref, b_ref, s_ref, o_ref, wbf_ref, *, shift, chunk):
    tn = o_ref.shape[1]

    @pl.when(pl.program_id(1) == 0)
    def _():
        def body(c, carry):
            r0 = pl.multiple_of(c * chunk, chunk)
            if shift:
                w = jnp.concatenate([wa_ref[pl.ds(r0, chunk), :], wb_ref[pl.ds(r0, chunk), :]], axis=1)
                w = w[:, shift:shift + tn]
            else:
                w = wa_ref[pl.ds(r0, chunk), :]
            wbf_ref[pl.ds(r0, chunk), :] = w.astype(BF16)
            return carry

        lax.fori_loop(0, wa_ref.shape[0] // chunk, body, 0)

    acc = jnp.dot(a_ref[...], wbf_ref[...], preferred_element_type=F32)
    o_ref[...] = ((acc + b_ref[...]) * s_ref[...]).astype(o_ref.dtype)


def _in_proj(a, w, layer, bias, scale, *, start, name):
    M, K = a.shape
    n = bias.shape[1]
    base = (start // 128) * 128
    shift = start - base
    tm = _pick(M, (1056, 1024, 768, 512, 384, 256, 128))
    tn = _pick(math.gcd(n, base), (512, 256, 128))
    return pl.pallas_call(
        functools.partial(_mm_shift_kernel, shift=shift, chunk=256),
        grid=(n // tn, M // tm),
        in_specs=[
            pl.BlockSpec((tm, K), lambda j, i: (i, 0)),
            pl.BlockSpec((None, K, tn), lambda j, i: (layer, 0, base // tn + j)),
            pl.BlockSpec((None, K, 128), lambda j, i: (layer, 0, (base + (j + 1) * tn) // 128)),
            pl.BlockSpec((1, tn), lambda j, i: (0, j)),
            pl.BlockSpec((1, tn), lambda j, i: (0, j)),
        ],
        out_specs=pl.BlockSpec((tm, tn), lambda j, i: (i, j)),
        out_shape=jax.ShapeDtypeStruct((M, n), BF16),
        scratch_shapes=[pltpu.VMEM((K, tn), BF16)],
        compiler_params=_params(2),
        name=name,
    )(a, w, w, bias, scale)


def _prep_kernel(hb_ref, ws_ref, bs_ref, wa_ref, ba_ref, crow_ref, la_ref, carry_ref, *, pad, tt):
    i = pl.program_id(1)

    @pl.when(i == 0)
    def _():
        carry_ref[...] = jnp.zeros_like(carry_ref)

    z = jnp.dot(hb_ref[...], ws_ref[...], preferred_element_type=F32) + bs_ref[...]
    pos = i * tt + lax.broadcasted_iota(jnp.int32, (tt, 1), 0)
    valid = pos >= pad
    la = _log_sigmoid(jnp.dot(z, wa_ref[...], preferred_element_type=F32, precision=HIGHEST) + ba_ref[...])
    la_ref[...] = jnp.where(valid, la * (1.0 / GLA_TAU), 0.0)
    lf = jnp.where(valid, _log_sigmoid(z), 0.0)

    r = lax.broadcasted_iota(jnp.int32, (ROW_TILE, ROW_TILE), 0)
    c = lax.broadcasted_iota(jnp.int32, (ROW_TILE, ROW_TILE), 1)
    tri = (r >= c).astype(F32)
    carry = carry_ref[...]
    for sb in range(tt // ROW_TILE):
        rows = slice(sb * ROW_TILE, (sb + 1) * ROW_TILE)
        cs = jnp.dot(tri, lf[rows], preferred_element_type=F32, precision=HIGHEST) + carry
        carry = cs[ROW_TILE - 1:ROW_TILE]
        posr = i * tt + sb * ROW_TILE + lax.broadcasted_iota(jnp.int32, (1, ROW_TILE), 1)
        crow_ref[0, :, rows] = jnp.where(posr >= pad, cs.T[0:8] * LOG2E, MASKED_KEY_BIAS)
    carry_ref[...] = carry


def _prep(hb, w_small, b_small, wa_ext, ba, *, B, lp, pad, tt):
    T, D = hb.shape
    KW = wa_ext.shape[1]
    nt = lp // tt
    return pl.pallas_call(
        functools.partial(_prep_kernel, pad=pad, tt=tt),
        grid=(B, nt),
        in_specs=[
            pl.BlockSpec((tt, D), lambda b, i: (b * nt + i, 0)),
            pl.BlockSpec((D, 128), lambda b, i: (0, 0)),
            pl.BlockSpec((1, 128), lambda b, i: (0, 0)),
            pl.BlockSpec((128, KW), lambda b, i: (0, 0)),
            pl.BlockSpec((1, KW), lambda b, i: (0, 0)),
        ],
        out_specs=[
            pl.BlockSpec((1, 8, tt), lambda b, i: (b, 0, i)),
            pl.BlockSpec((tt, KW), lambda b, i: (b * nt + i, 0)),
        ],
        out_shape=[
            jax.ShapeDtypeStruct((B, 8, lp), F32),
            jax.ShapeDtypeStruct((T, KW), F32),
        ],
        scratch_shapes=[pltpu.VMEM((1, 128), F32)],
        compiler_params=_params(2),
        name="gate_prep",
    )(hb, w_small, b_small, wa_ext, ba)


def _dot_nt(a, b):
    return lax.dot_general(a, b, (((1,), (1,)), ((), ())), preferred_element_type=F32)


def _fox_kernel(q_ref, k_ref, v_ref, crow_ref, o_ref, *, heads, tq):
    qi = pl.program_id(1)
    q0 = pl.multiple_of(qi * tq, tq)
    rows = lax.broadcasted_iota(jnp.int32, (tq, tq), 0)
    cols = lax.broadcasted_iota(jnp.int32, (tq, tq), 1)
    causal = cols <= rows
    ones = jnp.ones((tq, HEAD_DIM), BF16)
    for h in range(heads):
        sl = slice(h * HEAD_DIM, (h + 1) * HEAD_DIM)
        q = q_ref[:, sl]

        def scores(k0):
            kk = k_ref[pl.ds(k0, tq), sl]
            cr = crow_ref[0, h:h + 1, pl.ds(k0, tq)]
            return _dot_nt(q, kk) - cr

        def weighted_values(p, k0):
            va = jnp.concatenate([v_ref[pl.ds(k0, tq), sl], ones], axis=1)
            pv = jnp.dot(p, va, preferred_element_type=F32)
            return pv[:, :HEAD_DIM], pv[:, HEAD_DIM:HEAD_DIM + 1]

        s = jnp.where(causal, scores(q0), -jnp.inf)
        m = jnp.max(s, axis=-1, keepdims=True)
        acc, l = weighted_values(jnp.exp2((s - m).astype(BF16)), q0)

        def body(ki, carry):
            m, l, acc = carry
            k0 = pl.multiple_of(ki * tq, tq)
            s = scores(k0)
            m_new = jnp.maximum(m, jnp.max(s, axis=-1, keepdims=True))
            a = jnp.exp2(m - m_new)
            pv, pl_sum = weighted_values(jnp.exp2((s - m_new).astype(BF16)), k0)
            return m_new, a * l + pl_sum, a * acc + pv

        m, l, acc = lax.fori_loop(0, qi, body, (m, l, acc))
        o_ref[:, sl] = (acc / l).astype(o_ref.dtype)


def _fox(proj, crow, *, B, lp, fw, off_q, off_k, off_v, tq):
    T = proj.shape[0]
    nq = lp // tq
    heads = fw // HEAD_DIM
    return pl.pallas_call(
        functools.partial(_fox_kernel, heads=heads, tq=tq),
        grid=(B, nq),
        in_specs=[
            pl.BlockSpec((tq, fw), lambda b, i: (b * nq + i, off_q // fw)),
            pl.BlockSpec((lp, fw), lambda b, i: (b, off_k // fw)),
            pl.BlockSpec((lp, fw), lambda b, i: (b, off_v // fw)),
            pl.BlockSpec((1, 8, lp), lambda b, i: (b, 0, 0)),
        ],
        out_specs=pl.BlockSpec((tq, fw), lambda b, i: (b * nq + i, 0)),
        out_shape=jax.ShapeDtypeStruct((T, fw), BF16),
        compiler_params=_params(2),
        name="fox_attention",
    )(proj, proj, proj, crow)


def _lru_kernel(lx_ref, lg_ref, cw_ref, cb_ref, wr_ref, br_ref, wi_ref, bi_ref, lam_ref, o_ref,
                ext_ref, a_ref, u_ref, hc_ref, *, pad, tt, nblk):
    i = pl.program_id(1)

    @pl.when(i == 0)
    def _():
        ext_ref[0:CONV_HALO, :] = jnp.zeros((CONV_HALO, ext_ref.shape[1]), F32)
        hc_ref[...] = jnp.zeros_like(hc_ref)

    @pl.when(i > 0)
    def _():
        ext_ref[0:CONV_HALO, :] = ext_ref[tt:tt + CONV_HALO, :]

    pos = i * tt + lax.broadcasted_iota(jnp.int32, (tt, 1), 0)
    valid = pos >= pad
    ext_ref[CONV_HALO:CONV_HALO + tt, :] = jnp.where(valid, lx_ref[...].astype(F32), 0.0)

    sp = jnp.maximum(-lam_ref[...], 0.0) + jnp.log1p(jnp.exp(-jnp.abs(lam_ref[...])))
    for n in range(nblk):
        sl = slice(n * HEAD_DIM, (n + 1) * HEAD_DIM)
        xc = cb_ref[:, sl]
        for j in range(CONV_W):
            start = CONV_HALO - (CONV_W - 1) + j
            xc = xc + cw_ref[j:j + 1, sl] * ext_ref[start:start + tt, sl]
        xb = xc.astype(BF16)
        r = _sigmoid(jnp.dot(xb, wr_ref[n].astype(BF16), preferred_element_type=F32) + br_ref[:, sl])
        g = _sigmoid(jnp.dot(xb, wi_ref[n].astype(BF16), preferred_element_type=F32) + bi_ref[:, sl])
        log_a = (-LRU_C) * r * sp[:, sl]
        a = jnp.exp(log_a)
        a_ref[:, sl] = a
        u = jnp.sqrt(-jnp.tanh(log_a) * (1.0 + a * a)) * (g * xc)
        u_ref[:, sl] = jnp.where(valid, u, 0.0)

    def body(gidx, h):
        r0 = pl.multiple_of(gidx * 8, 8)
        a8 = a_ref[pl.ds(r0, 8), :]
        u8 = u_ref[pl.ds(r0, 8), :]
        outs = []
        for r in range(8):
            h = a8[r:r + 1] * h + u8[r:r + 1]
            outs.append(h)
        u_ref[pl.ds(r0, 8), :] = jnp.concatenate(outs, axis=0)
        return h

    hc_ref[...] = lax.fori_loop(0, tt // 8, body, hc_ref[...])
    o_ref[...] = (u_ref[...] * _gelu_tanh(lg_ref[...].astype(F32))).astype(o_ref.dtype)


def _lru(proj, conv_w, conv_b, w_r, b_r, w_i, b_i, lam, *, B, lp, lw, off_x, off_g, pad, tt):
    T = proj.shape[0]
    nt = lp // tt
    nblk = lw // HEAD_DIM
    row = lambda b, i: (0, 0)
    return pl.pallas_call(
        functools.partial(_lru_kernel, pad=pad, tt=tt, nblk=nblk),
        grid=(B, nt),
        in_specs=[
            pl.BlockSpec((tt, lw), lambda b, i: (b * nt + i, off_x // lw)),
            pl.BlockSpec((tt, lw), lambda b, i: (b * nt + i, off_g // lw)),
            pl.BlockSpec((CONV_W, lw), row),
            pl.BlockSpec((1, lw), row),
            pl.BlockSpec((nblk, HEAD_DIM, HEAD_DIM), lambda b, i: (0, 0, 0)),
            pl.BlockSpec((1, lw), row),
            pl.BlockSpec((nblk, HEAD_DIM, HEAD_DIM), lambda b, i: (0, 0, 0)),
            pl.BlockSpec((1, lw), row),
            pl.BlockSpec((1, lw), row),
        ],
        out_specs=pl.BlockSpec((tt, lw), lambda b, i: (b * nt + i, 0)),
        out_shape=jax.ShapeDtypeStruct((T, lw), BF16),
        scratch_shapes=[
            pltpu.VMEM((tt + CONV_HALO, lw), F32),
            pltpu.VMEM((tt, lw), F32),
            pltpu.VMEM((tt, lw), F32),
            pltpu.VMEM((1, lw), F32),
        ],
        compiler_params=_params(2),
        name="conv_rglru",
    )(proj, proj, conv_w, conv_b, w_r, b_r, w_i, b_i, lam)


def _gla_kernel(q_ref, k_ref, v_ref, gg_ref, la_ref, ng_ref, o_ref, st_ref, *, pad, tt, heads):
    i = pl.program_id(1)

    @pl.when(i == 0)
    def _():
        st_ref[...] = jnp.zeros_like(st_ref)

    r = lax.broadcasted_iota(jnp.int32, (GLA_CHUNK, GLA_CHUNK), 0)
    c = lax.broadcasted_iota(jnp.int32, (GLA_CHUNK, GLA_CHUNK), 1)
    tri = (r >= c).astype(F32)
    for ci in range(tt // GLA_CHUNK):
        rows = slice(ci * GLA_CHUNK, (ci + 1) * GLA_CHUNK)
        pos = i * tt + ci * GLA_CHUNK + lax.broadcasted_iota(jnp.int32, (GLA_CHUNK, 1), 0)
        valid = pos >= pad
        for hd in range(heads):
            ks = slice(hd * HEAD_DIM, (hd + 1) * HEAD_DIM)
            vs = slice(hd * GLA_DV, (hd + 1) * GLA_DV)
            cs = jnp.dot(tri, la_ref[rows, ks], preferred_element_type=F32, precision=HIGHEST)
            cl = cs[GLA_CHUNK - 1:GLA_CHUNK]
            kdec = jnp.where(valid, k_ref[rows, ks].astype(F32) * jnp.exp(cl - cs), 0.0).astype(BF16)
            ut = lax.dot_general(v_ref[rows, vs], kdec, (((0,), (0,)), ((), ())), preferred_element_type=F32)
            st = st_ref[hd] * jnp.exp(cl) + ut
            st_ref[hd] = st
            o = _dot_nt(q_ref[rows, ks], st.astype(BF16))
            o = o * lax.rsqrt(jnp.mean(o * o, axis=-1, keepdims=True) + RMS_EPS) * ng_ref[:, vs]
            gg = gg_ref[rows, vs].astype(F32)
            o_ref[rows, vs] = (o * (gg * _sigmoid(gg))).astype(o_ref.dtype)


def _gla(proj, proj_g, la, norm_g, *, B, lp, kw, vw, off_q, off_k, off_v, off_g, pad, tt):
    T = proj.shape[0]
    nt = lp // tt
    heads = kw // HEAD_DIM
    return pl.pallas_call(
        functools.partial(_gla_kernel, pad=pad, tt=tt, heads=heads),
        grid=(B, nt),
        in_specs=[
            pl.BlockSpec((tt, kw), lambda b, i: (b * nt + i, off_q // kw)),
            pl.BlockSpec((tt, kw), lambda b, i: (b * nt + i, off_k // kw)),
            pl.BlockSpec((tt, vw), lambda b, i: (b * nt + i, off_v // vw)),
            pl.BlockSpec((tt, vw), lambda b, i: (b * nt + i, off_g // vw)),
            pl.BlockSpec((tt, kw), lambda b, i: (b * nt + i, 0)),
            pl.BlockSpec((1, vw), lambda b, i: (0, 0)),
        ],
        out_specs=pl.BlockSpec((tt, vw), lambda b, i: (b * nt + i, 0)),
        out_shape=jax.ShapeDtypeStruct((T, vw), BF16),
        scratch_shapes=[pltpu.VMEM((heads, GLA_DV, HEAD_DIM), F32)],
        compiler_params=_params(2),
        name="gla_chunked",
    )(proj, proj, proj, proj_g, la, norm_g)


def _merge_kernel(of_ref, ol_ref, og_ref, w_ref, g0_ref, g1_ref, g2_ref, o_ref, wbf_ref, *, fw, lw):
    @pl.when(pl.program_id(1) == 0)
    def _():
        wbf_ref[...] = w_ref[...].astype(BF16)

    y0 = jnp.dot(of_ref[...], wbf_ref[0:fw, :], preferred_element_type=F32)
    y1 = jnp.dot(ol_ref[...], wbf_ref[fw:fw + lw, :], preferred_element_type=F32)
    y2 = jnp.dot(og_ref[...], wbf_ref[fw + lw:, :], preferred_element_type=F32)
    out = (_sigmoid(g0_ref[...].astype(F32)) * y0 + _sigmoid(g1_ref[...].astype(F32)) * y1
           + _sigmoid(g2_ref[...].astype(F32)) * y2)
    o_ref[...] = out.astype(o_ref.dtype)


def _merge(o_fox, o_lru, o_gla, w_branch, layer, proj, *, off_gates, d):
    T, fw = o_fox.shape
    lw = o_lru.shape[1]
    vw = o_gla.shape[1]
    tm = _pick(T, (1056, 1024, 768, 512, 384, 256, 128))
    tn = _pick(math.gcd(d, off_gates), (512, 256, 128))
    gate_spec = lambda b: pl.BlockSpec((tm, tn), lambda j, i: (i, (off_gates + b * d) // tn + j))
    return pl.pallas_call(
        functools.partial(_merge_kernel, fw=fw, lw=lw),
        grid=(d // tn, T // tm),
        in_specs=[
            pl.BlockSpec((tm, fw), lambda j, i: (i, 0)),
            pl.BlockSpec((tm, lw), lambda j, i: (i, 0)),
            pl.BlockSpec((tm, vw), lambda j, i: (i, 0)),
            pl.BlockSpec((None, fw + lw + vw, tn), lambda j, i: (layer, 0, j)),
            gate_spec(0), gate_spec(1), gate_spec(2),
        ],
        out_specs=pl.BlockSpec((tm, tn), lambda j, i: (i, j)),
        out_shape=jax.ShapeDtypeStruct((T, d), BF16),
        scratch_shapes=[pltpu.VMEM((fw + lw + vw, tn), BF16)],
        compiler_params=_params(2),
        name="branch_merge",
    )(o_fox, o_lru, o_gla, w_branch, proj, proj, proj)


def _ln_router_kernel(h_ref, mix_ref, g_ref, b_ref, wr_ref, br_ref, hf_ref, hb_ref, idx_ref, wt_ref, *, alpha):
    y = _layer_norm(alpha * h_ref[...] + mix_ref[...], g_ref[...], b_ref[...])
    hf_ref[...] = y
    hb_ref[...] = y.astype(BF16)
    logits = jnp.dot(y, wr_ref[...], preferred_element_type=F32, precision=HIGHEST) + br_ref[...]
    lane = lax.broadcasted_iota(jnp.int32, logits.shape, 1).astype(F32)
    idx = jnp.zeros_like(logits)
    vals = []
    for k in range(TOP_K):
        mx = jnp.max(logits, axis=-1, keepdims=True)
        sel = jnp.min(jnp.where(logits == mx, lane, float(logits.shape[1])), axis=-1, keepdims=True)
        vals.append(mx)
        idx = jnp.where(lane == float(k), sel, idx)
        logits = jnp.where(lane == sel, -jnp.inf, logits)
    es = [jnp.exp(v - vals[0]) for v in vals]
    tot = es[0]
    for e in es[1:]:
        tot = tot + e
    wt = jnp.zeros_like(logits)
    for k in range(TOP_K):
        wt = jnp.where(lane == float(k), es[k] / tot, wt)
    idx_ref[...] = idx.astype(jnp.int32)
    wt_ref[...] = wt


def _ln_router(h, mix, g, b, wr_pad, br_pad, *, alpha):
    T, D = h.shape
    tm = _pick(T, (192, 128))
    blk = pl.BlockSpec((tm, D), lambda i: (i, 0))
    row = pl.BlockSpec((1, D), lambda i: (0, 0))
    small = pl.BlockSpec((tm, 128), lambda i: (i, 0))
    return pl.pallas_call(
        functools.partial(_ln_router_kernel, alpha=alpha),
        grid=(T // tm,),
        in_specs=[blk, blk, row, row, pl.BlockSpec((D, 128), lambda i: (0, 0)), pl.BlockSpec((1, 128), lambda i: (0, 0))],
        out_specs=[blk, blk, small, small],
        out_shape=[
            jax.ShapeDtypeStruct((T, D), F32),
            jax.ShapeDtypeStruct((T, D), BF16),
            jax.ShapeDtypeStruct((T, 128), jnp.int32),
            jax.ShapeDtypeStruct((T, 128), F32),
        ],
        compiler_params=_params(1),
        name="ln_router",
    )(h, mix, g, b, wr_pad, br_pad)


def _row_gather(src_hbm, idx_ref, base, buf_ref, sem, n):
    def body(r, carry):
        tok = idx_ref[base + r]
        pltpu.make_async_copy(src_hbm.at[pl.ds(tok, 1)], buf_ref.at[pl.ds(r, 1)], sem).start()
        return carry

    lax.fori_loop(0, n, body, 0, unroll=8)


def _cast_rows(src_ref, dst_ref, chunk):
    def body(c, carry):
        r0 = pl.multiple_of(c * chunk, chunk)
        dst_ref[pl.ds(r0, chunk), :] = src_ref[pl.ds(r0, chunk), :].astype(BF16)
        return carry

    lax.fori_loop(0, src_ref.shape[0] // chunk, body, 0)


def _moe_up_kernel(te_ref, na_ref, first_ref, nxt_ref, tok_ref, h_hbm, wup_hbm, bup_ref, sel_ref, act_ref,
                   xbuf_ref, sem_ref, stage_ref, wbf_ref, wsem_ref, *, tmx, layer):
    i = pl.program_id(0)
    na = na_ref[0]
    slot = lax.rem(i, 2)

    def weight_copy(e):
        return pltpu.make_async_copy(wup_hbm.at[layer, e], stage_ref, wsem_ref.at[0])

    @pl.when(jnp.logical_and(i == 0, na > 0))
    def _():
        _row_gather(h_hbm, tok_ref, 0, xbuf_ref.at[0], sem_ref.at[0], tmx)
        weight_copy(te_ref[0]).start()

    @pl.when(i + 1 < na)
    def _():
        _row_gather(h_hbm, tok_ref, (i + 1) * tmx, xbuf_ref.at[1 - slot], sem_ref.at[1 - slot], tmx)

    @pl.when(jnp.logical_and(i < na, first_ref[i] == 1))
    def _():
        weight_copy(te_ref[i]).wait()
        _cast_rows(stage_ref, wbf_ref, 256)

        @pl.when(nxt_ref[i] >= 0)
        def _():
            weight_copy(nxt_ref[i]).start()

    @pl.when(i < na)
    def _():
        pltpu.make_async_copy(h_hbm.at[pl.ds(0, tmx)], xbuf_ref.at[slot], sem_ref.at[slot]).wait()
        x = xbuf_ref[slot].astype(BF16)
        h = jnp.dot(x, wbf_ref[...], preferred_element_type=F32) + bup_ref[...]
        g = jnp.minimum(h, SWIGLU_LIMIT)
        u = jnp.clip(pltpu.roll(h, h.shape[1] - 1, axis=1), -SWIGLU_LIMIT, SWIGLU_LIMIT)
        act = ((u + 1.0) * g * _sigmoid(SWIGLU_ALPHA * g)).astype(BF16)
        act_ref[...] = jnp.dot(act, sel_ref[...], preferred_element_type=F32).astype(BF16)

    @pl.when(i >= na)
    def _():
        act_ref[...] = jnp.zeros_like(act_ref)


def _moe_down_kernel(te_ref, na_ref, first_ref, act_ref, wd_ref, bd_ref, y_ref, wbf_ref):
    i = pl.program_id(0)
    na = na_ref[0]

    @pl.when(jnp.logical_and(i < na, first_ref[i] == 1))
    def _():
        _cast_rows(wd_ref, wbf_ref, 128)

    @pl.when(i < na)
    def _():
        y_ref[...] = jnp.dot(act_ref[...], wbf_ref[...], preferred_element_type=F32) + bd_ref[...]

    @pl.when(i >= na)
    def _():
        y_ref[...] = jnp.zeros_like(y_ref)


def _moe(tile_expert, n_active, row_token, hf, w_up, b_up, sel, w_down, b_down, *, tmx, layer):
    T, D = hf.shape
    _, E, _, F2 = w_up.shape
    F = F2 // 2
    P = row_token.shape[0]
    ntiles = P // tmx
    tiles = jnp.arange(ntiles, dtype=jnp.int32)
    first = jnp.where(tiles == 0, 1, (tile_expert != jnp.roll(tile_expert, 1)).astype(jnp.int32))
    j = jnp.searchsorted(tile_expert, tile_expert, side="right").astype(jnp.int32)
    nxt = jnp.where(j < ntiles, tile_expert[jnp.minimum(j, ntiles - 1)], -1).astype(jnp.int32)

    act = pl.pallas_call(
        functools.partial(_moe_up_kernel, tmx=tmx, layer=layer),
        grid_spec=pltpu.PrefetchScalarGridSpec(
            num_scalar_prefetch=5,
            grid=(ntiles,),
            in_specs=[
                pl.BlockSpec(memory_space=pl.ANY),
                pl.BlockSpec(memory_space=pl.ANY),
                pl.BlockSpec((None, 1, F2), lambda i, te, *_: (te[i], 0, 0)),
                pl.BlockSpec((F2, F), lambda i, *_: (0, 0)),
            ],
            out_specs=pl.BlockSpec((tmx, F), lambda i, *_: (i, 0)),
            scratch_shapes=[
                pltpu.VMEM((2, tmx, D), F32), pltpu.SemaphoreType.DMA((2,)),
                pltpu.VMEM((D, F2), F32), pltpu.VMEM((D, F2), BF16), pltpu.SemaphoreType.DMA((1,)),
            ],
        ),
        out_shape=jax.ShapeDtypeStruct((P, F), BF16),
        compiler_params=_params(1),
        name="moe_up",
    )(tile_expert, n_active, first, nxt, row_token, hf, w_up, b_up, sel)

    return pl.pallas_call(
        _moe_down_kernel,
        grid_spec=pltpu.PrefetchScalarGridSpec(
            num_scalar_prefetch=3,
            grid=(ntiles,),
            in_specs=[
                pl.BlockSpec((tmx, F), lambda i, *_: (i, 0)),
                pl.BlockSpec((None, None, F, D), lambda i, te, *_: (layer, te[i], 0, 0)),
                pl.BlockSpec((None, 1, D), lambda i, te, *_: (te[i], 0, 0)),
            ],
            out_specs=pl.BlockSpec((tmx, D), lambda i, *_: (i, 0)),
            scratch_shapes=[pltpu.VMEM((F, D), BF16)],
        ),
        out_shape=jax.ShapeDtypeStruct((P, D), F32),
        compiler_params=_params(1),
        name="moe_down",
    )(tile_expert, n_active, first, act, w_down, b_down)


def _combine_kernel(pos_ref, y_hbm, h_ref, w_ref, g_ref, b_ref, *rest, alpha, tc, nt, final):
    if final:
        out_ref, buf_ref, sem_ref = rest
    else:
        hf_ref, hb_ref, buf_ref, sem_ref = rest
    i = pl.program_id(0)
    slot = lax.rem(i, 2)

    def start(tile, s):
        for k in range(TOP_K):
            _row_gather(y_hbm, pos_ref, (k * nt + tile) * tc, buf_ref.at[s, k], sem_ref.at[s], tc)

    @pl.when(i == 0)
    def _():
        start(0, 0)

    @pl.when(i + 1 < nt)
    def _():
        start(i + 1, 1 - slot)

    for k in range(TOP_K):
        pltpu.make_async_copy(y_hbm.at[pl.ds(0, tc)], buf_ref.at[slot, k], sem_ref.at[slot]).wait()
    ffn = w_ref[:, 0:1] * buf_ref[slot, 0]
    for k in range(1, TOP_K):
        ffn = ffn + w_ref[:, k:k + 1] * buf_ref[slot, k]
    y = _layer_norm(alpha * h_ref[...] + ffn, g_ref[...], b_ref[...])
    if final:
        out_ref[...] = y
    else:
        hf_ref[...] = y
        hb_ref[...] = y.astype(BF16)


def _combine(pos, ys, hf, top_w, g, b, *, alpha, lp, final):
    T, D = hf.shape
    tc = ROW_TILE
    nt = T // tc
    ntb = lp // tc
    if final:
        out_specs = pl.BlockSpec((tc, D), lambda i, pos: ((i // ntb) * (ntb - 1) + jnp.maximum(i % ntb - 1, 0), 0))
        out_shape = jax.ShapeDtypeStruct(((T // lp) * (lp - tc), D), F32)
    else:
        out_specs = [pl.BlockSpec((tc, D), lambda i, pos: (i, 0)), pl.BlockSpec((tc, D), lambda i, pos: (i, 0))]
        out_shape = [jax.ShapeDtypeStruct((T, D), F32), jax.ShapeDtypeStruct((T, D), BF16)]
    grid_spec = pltpu.PrefetchScalarGridSpec(
        num_scalar_prefetch=1,
        grid=(nt,),
        in_specs=[
            pl.BlockSpec(memory_space=pl.ANY),
            pl.BlockSpec((tc, D), lambda i, pos: (i, 0)),
            pl.BlockSpec((tc, 128), lambda i, pos: (i, 0)),
            pl.BlockSpec((1, D), lambda i, pos: (0, 0)),
            pl.BlockSpec((1, D), lambda i, pos: (0, 0)),
        ],
        out_specs=out_specs,
        scratch_shapes=[pltpu.VMEM((2, TOP_K, tc, D), F32), pltpu.SemaphoreType.DMA((2,))],
    )
    return pl.pallas_call(
        functools.partial(_combine_kernel, alpha=alpha, tc=tc, nt=nt, final=final),
        grid_spec=grid_spec,
        out_shape=out_shape,
        compiler_params=_params(1),
        name="moe_combine_ln",
    )(pos, ys, hf, top_w, g, b)


def _route_kernel(idx_ref, pos_ref, te_ref, na_ref, cnt_ref, pst_ref, *, pad, tr, tmx, n_experts, spare_row):
    p = pl.program_id(0)
    i = pl.program_id(2)
    first = jnp.logical_and(pl.program_id(1) == 0, i == 0)

    @pl.when(jnp.logical_and(p == 0, first))
    def _():
        cnt_ref[...] = jnp.zeros_like(cnt_ref)

    @pl.when(jnp.logical_and(p == 1, first))
    def _():
        cnt = cnt_ref[...]
        padded = jnp.floor((cnt + (tmx - 1.0)) * (1.0 / tmx)) * tmx
        r = lax.broadcasted_iota(jnp.int32, (128, 128), 0)
        c = lax.broadcasted_iota(jnp.int32, (128, 128), 1)
        pst = jnp.dot(padded, (r < c).astype(F32), preferred_element_type=F32, precision=HIGHEST)
        pst_ref[...] = pst
        cnt_ref[...] = jnp.zeros_like(cnt_ref)
        tile_end = (pst + padded) * (1.0 / tmx)
        nt_pad = te_ref.shape[0]
        t = lax.broadcasted_iota(jnp.int32, (nt_pad, 128), 0).astype(F32)
        lane = lax.broadcasted_iota(jnp.int32, (nt_pad, 128), 1)
        is_expert = lane < n_experts
        te = jnp.sum(jnp.where(jnp.logical_and(tile_end <= t, is_expert), 1.0, 0.0), axis=-1, keepdims=True)
        lane1 = lax.broadcasted_iota(jnp.int32, (1, 128), 1).astype(F32)
        e_last = jnp.max(jnp.where(cnt > 0.0, lane1, 0.0), axis=-1, keepdims=True)
        te_ref[...] = jnp.broadcast_to(jnp.minimum(te, e_last), (nt_pad, 128)).astype(jnp.int32)
        na_ref[...] = jnp.broadcast_to(jnp.max(tile_end, axis=-1, keepdims=True), (1, 128)).astype(jnp.int32)

    idx = idx_ref[...]
    lane = lax.broadcasted_iota(jnp.int32, (tr, 128), 1)
    valid = (i * tr + lax.broadcasted_iota(jnp.int32, (tr, 1), 0)) >= pad
    onehot = jnp.zeros((tr, 128), F32)
    for k in range(TOP_K):
        onehot = onehot + jnp.where(lane == idx[:, k:k + 1], 1.0, 0.0)
    onehot = jnp.where(valid, onehot, 0.0)
    r = lax.broadcasted_iota(jnp.int32, (tr, tr), 0)
    c = lax.broadcasted_iota(jnp.int32, (tr, tr), 1)
    rank = jnp.dot((r > c).astype(BF16), onehot.astype(BF16), preferred_element_type=F32) + cnt_ref[...]
    cnt_ref[...] = cnt_ref[...] + jnp.sum(onehot, axis=0, keepdims=True)

    @pl.when(p == 0)
    def _():
        pos_ref[...] = jnp.zeros_like(pos_ref)

    @pl.when(p == 1)
    def _():
        dest = rank + pst_ref[...]
        out = jnp.zeros((tr, 128), F32)
        for k in range(TOP_K):
            d = jnp.sum(jnp.where(lane == idx[:, k:k + 1], dest, 0.0), axis=-1, keepdims=True)
            out = jnp.where(lane == k, jnp.where(valid, d, float(spare_row)), out)
        pos_ref[...] = out.astype(jnp.int32)


def _route(top_idx, *, B, lp, pad, tmx, n_experts, n_rows):
    T = top_idx.shape[0]
    tr = _pick(lp, (384, 256, 128))
    nt = lp // tr
    nt_pad = -(-(n_rows // tmx) // 8) * 8
    pos, te, na = pl.pallas_call(
        functools.partial(_route_kernel, pad=pad, tr=tr, tmx=tmx, n_experts=n_experts, spare_row=n_rows - 1),
        grid=(2, B, nt),
        in_specs=[pl.BlockSpec((tr, 128), lambda p, b, i: (b * nt + i, 0))],
        out_specs=[
            pl.BlockSpec((tr, 128), lambda p, b, i: (p * (b * nt + i), 0)),
            pl.BlockSpec((nt_pad, 128), lambda p, b, i: (0, 0)),
            pl.BlockSpec((1, 128), lambda p, b, i: (0, 0)),
        ],
        out_shape=[
            jax.ShapeDtypeStruct((T, 128), jnp.int32),
            jax.ShapeDtypeStruct((nt_pad, 128), jnp.int32),
            jax.ShapeDtypeStruct((1, 128), jnp.int32),
        ],
        scratch_shapes=[pltpu.VMEM((1, 128), F32), pltpu.VMEM((1, 128), F32)],
        compiler_params=_params(3),
        name="route_rank",
    )(top_idx)
    pos_flat = pos[:, :TOP_K].T.reshape(-1)
    return pos_flat, te[:n_rows // tmx, 0], na[0, :1]


def _row_token_kernel(pos_ref, rt_ref, *, t):
    def zero(r, carry):
        rt_ref[r] = 0
        return carry

    lax.fori_loop(0, rt_ref.shape[0], zero, 0, unroll=8)
    for k in range(TOP_K):
        def body(tok, carry):
            rt_ref[pos_ref[k * t + tok]] = tok
            return carry

        lax.fori_loop(0, t, body, 0, unroll=8)


def _row_token(pos_flat, *, t, n_rows):
    return pl.pallas_call(
        functools.partial(_row_token_kernel, t=t),
        grid_spec=pltpu.PrefetchScalarGridSpec(
            num_scalar_prefetch=1, grid=(1,), in_specs=[],
            out_specs=pl.BlockSpec(memory_space=pltpu.SMEM)),
        out_shape=jax.ShapeDtypeStruct((n_rows,), jnp.int32),
        compiler_params=_params(1),
        name="route_row_token",
    )(pos_flat)


def _segments(fw, fh, lw, kw, vw, rank, d):
    names = ("fq", "fk", "fv", "ff", "lx", "lg", "gq", "gk", "gv", "ga", "gg", "gate0", "gate1", "gate2")
    widths = (fw, fw, fw, fh, lw, lw, kw, kw, vw, rank, vw, d, d, d)
    segs, off = {}, 0
    for n, w in zip(names, widths):
        segs[n] = (off, w)
        off += w
    return segs, off


def kernel(x, meta_tokens, emb_ln_g, emb_ln_b, w_in, b_in, conv_w, conv_b, lru_w_r, lru_b_r, lru_w_i, lru_b_i,
           lru_lambda, gla_w_alpha, gla_b_alpha, gla_norm_g, w_branch, w_out, b_out, ln1_g, ln1_b, w_router,
           b_router, w_up, b_up, w_down, b_down, ln2_g, ln2_b):
    B, S, D = x.shape
    n_meta = meta_tokens.shape[0]
    depth = w_in.shape[0]
    L = S + n_meta
    pad = (-L) % ROW_TILE
    lp = L + pad
    assert pad + n_meta == ROW_TILE and S % ROW_TILE == 0
    T = B * lp

    lw = conv_w.shape[2]
    rank, kw = gla_w_alpha.shape[1:]
    vw = gla_norm_g.shape[1]
    n_experts = w_router.shape[2]
    fexp = w_up.shape[3] // 2
    in_cols = w_in.shape[2]
    fh = (in_cols - 2 * lw - 2 * kw - 2 * vw - rank - 3 * D) // (3 * HEAD_DIM + 1)
    fw = fh * HEAD_DIM
    segs, total = _segments(fw, fh, lw, kw, vw, rank, D)
    assert total == in_cols and fh <= 8 and fh + rank <= 128
    alpha = (2.0 * depth) ** 0.25

    windows = (("fq", "fk", "fv"), ("lx", "lg", "gq", "gk", "gv"), ("gg", "gate0", "gate1", "gate2"))
    qscale = HEAD_DIM ** -0.5
    off, win_start, win_scale = {}, [], []
    for names in windows:
        start = segs[names[0]][0]
        for n in names:
            off[n] = segs[n][0] - start
            assert off[n] % segs[n][1] == 0 or n.startswith("gate")
        win_start.append(start)
        scales = {"fq": qscale * LOG2E, "gq": qscale}
        win_scale.append(jnp.concatenate(
            [jnp.full((segs[n][1],), scales.get(n, 1.0), F32) for n in names]).reshape(1, -1))

    def cols(a, n):
        o, w = segs[n]
        return a[..., o:o + w]

    tt = _pick(lp, (384, 256, 128))
    tmx = MOE_ROW_TILE if T >= 4096 else 128
    n_rows = (-(-(B * L * TOP_K + n_experts * (tmx - 1)) // tmx) + 1) * tmx

    head = jnp.concatenate([jnp.zeros((pad, D), F32), meta_tokens.astype(F32)], axis=0)
    hf, hb = _embed_ln(x, head, emb_ln_g.reshape(1, D), emb_ln_b.reshape(1, D), pad=pad, lp=lp)
    sel = (jnp.arange(2 * fexp)[:, None] == 2 * jnp.arange(fexp)[None, :]).astype(BF16)

    for l in range(depth):
        w_small = jnp.concatenate(
            [cols(w_in[l], "ff"), jnp.zeros((D, 8 - fh), F32), cols(w_in[l], "ga"),
             jnp.zeros((D, 128 - 8 - rank), F32)], axis=1).astype(BF16)
        b_small = jnp.concatenate(
            [cols(b_in[l], "ff"), jnp.zeros((8 - fh,), F32), cols(b_in[l], "ga"),
             jnp.zeros((128 - 8 - rank,), F32)]).reshape(1, 128)
        wa_ext = jnp.zeros((128, kw), F32).at[8:8 + rank].set(gla_w_alpha[l])

        pa, pb, pc = [
            _in_proj(hb, w_in, l, b_in[l][s:s + sc.shape[1]].reshape(1, -1), sc, start=s, name=f"in_proj_{k}")
            for k, (s, sc) in enumerate(zip(win_start, win_scale))]
        crow, la = _prep(hb, w_small, b_small, wa_ext, gla_b_alpha[l].reshape(1, kw), B=B, lp=lp, pad=pad, tt=tt)
        o_fox = _fox(pa, crow, B=B, lp=lp, fw=fw, off_q=off["fq"], off_k=off["fk"], off_v=off["fv"], tq=tt)
        o_lru = _lru(pb, conv_w[l], conv_b[l].reshape(1, lw), lru_w_r[l], lru_b_r[l].reshape(1, lw), lru_w_i[l],
                     lru_b_i[l].reshape(1, lw), lru_lambda[l].reshape(1, lw), B=B, lp=lp, lw=lw, off_x=off["lx"],
                     off_g=off["lg"], pad=pad, tt=tt)
        o_gla = _gla(pb, pc, la, gla_norm_g[l].reshape(1, vw), B=B, lp=lp, kw=kw, vw=vw, off_q=off["gq"],
                     off_k=off["gk"], off_v=off["gv"], off_g=off["gg"], pad=pad, tt=tt)
        merged = _merge(o_fox, o_lru, o_gla, w_branch, l, pc, off_gates=off["gate0"], d=D)
        mix = _matmul(merged, w_out, l, b_out[l].reshape(1, D), jnp.ones((1, D), F32), F32, "out_proj")

        wr_pad = jnp.zeros((D, 128), F32).at[:, :n_experts].set(w_router[l])
        br_pad = jnp.full((1, 128), -MASKED_KEY_BIAS, F32).at[0, :n_experts].set(b_router[l])
        hf, hb, top_idx, top_w = _ln_router(hf, mix, ln1_g[l].reshape(1, D), ln1_b[l].reshape(1, D), wr_pad, br_pad,
                                            alpha=alpha)
        pos, tile_expert, n_active = _route(top_idx, B=B, lp=lp, pad=pad, tmx=tmx, n_experts=n_experts, n_rows=n_rows)
        row_token = _row_token(pos, t=T, n_rows=n_rows)
        ys = _moe(tile_expert, n_active, row_token, hf, w_up, b_up[l].reshape(n_experts, 1, 2 * fexp), sel, w_down,
                  b_down[l].reshape(n_experts, 1, D), tmx=tmx, layer=l)
        final = l == depth - 1
        res = _combine(pos, ys, hf, top_w, ln2_g[l].reshape(1, D), ln2_b[l].reshape(1, D), alpha=alpha, lp=lp,
                       final=final)
        if final:
            return res.reshape(B, S, D)
        hf, hb = res
```

```python
import functools
import math

import jax
import jax.numpy as jnp
from jax import lax
from jax.experimental import pallas as pl
from jax.experimental.pallas import tpu as pltpu

F32 = jnp.float32
BF16 = jnp.bfloat16
HIGHEST = lax.Precision.HIGHEST

ROW_TILE = 128
HEAD_DIM = 128
GLA_DV = 256
GLA_CHUNK = 64
CONV_W = 4
CONV_HALO = 8
LN_EPS = 1e-5
RMS_EPS = 1e-6
LRU_C = 8.0
GLA_TAU = 16.0
SWIGLU_LIMIT = 7.0
SWIGLU_ALPHA = 1.702
TOP_K = 4
LOG2E = 1.4426950408889634
MASKED_KEY_BIAS = 1e30
MOE_ROW_TILE = 256
FOX_HEAD_GROUP = 4
FOX_Q_TILE = 384
VMEM_LIMIT = 56 * 1024 * 1024


def _pick(n, candidates):
    for c in candidates:
        if n % c == 0:
            return c
    raise ValueError(f"no tile in {candidates} divides {n}")


def _params(n_axes, vmem=VMEM_LIMIT):
    return pltpu.CompilerParams(dimension_semantics=("arbitrary",) * n_axes, vmem_limit_bytes=vmem)


def _layer_norm(v, g, b):
    mu = jnp.mean(v, axis=-1, keepdims=True)
    d = v - mu
    var = jnp.mean(d * d, axis=-1, keepdims=True)
    return d * lax.rsqrt(var + LN_EPS) * g + b


def _log_sigmoid(x):
    return jnp.minimum(x, 0.0) - jnp.log1p(jnp.exp(-jnp.abs(x)))


def _sigmoid(x):
    return 1.0 / (1.0 + jnp.exp(-x))


def _gelu_tanh(x):
    return 0.5 * x * (1.0 + jnp.tanh(0.7978845608028654 * (x + 0.044715 * (x * x * x))))


def _embed_ln_kernel(x_ref, head_ref, g_ref, b_ref, hf_ref, hb_ref, *, pad):
    i = pl.program_id(1)

    @pl.when(i == 0)
    def _():
        y = _layer_norm(head_ref[...], g_ref[...], b_ref[...])
        rows = lax.broadcasted_iota(jnp.int32, (ROW_TILE, 1), 0)
        y = jnp.where(rows >= pad, y, 0.0)
        hf_ref[...] = y
        hb_ref[...] = y.astype(BF16)

    @pl.when(i > 0)
    def _():
        y = _layer_norm(x_ref[0], g_ref[...], b_ref[...])
        hf_ref[...] = y
        hb_ref[...] = y.astype(BF16)


def _embed_ln(x, head, g, b, *, pad, lp):
    B, S, D = x.shape
    nt = lp // ROW_TILE
    T = B * lp
    return pl.pallas_call(
        functools.partial(_embed_ln_kernel, pad=pad),
        grid=(B, nt),
        in_specs=[
            pl.BlockSpec((1, ROW_TILE, D), lambda b, i: (b, jnp.maximum(i - 1, 0), 0)),
            pl.BlockSpec((ROW_TILE, D), lambda b, i: (0, 0)),
            pl.BlockSpec((1, D), lambda b, i: (0, 0)),
            pl.BlockSpec((1, D), lambda b, i: (0, 0)),
        ],
        out_specs=[
            pl.BlockSpec((ROW_TILE, D), lambda b, i: (b * nt + i, 0)),
            pl.BlockSpec((ROW_TILE, D), lambda b, i: (b * nt + i, 0)),
        ],
        out_shape=[jax.ShapeDtypeStruct((T, D), F32), jax.ShapeDtypeStruct((T, D), BF16)],
        compiler_params=_params(2),
        name="embed_ln",
    )(x, head, g, b)


def _dot_nt(a, b):
    return lax.dot_general(a, b, (((1,), (1,)), ((), ())), preferred_element_type=F32)


def _cast_rows(src_ref, dst_ref, chunk):
    def body(c, carry):
        r0 = pl.multiple_of(c * chunk, chunk)
        dst_ref[pl.ds(r0, chunk), :] = src_ref[pl.ds(r0, chunk), :].astype(BF16)
        return carry

    lax.fori_loop(0, src_ref.shape[0] // chunk, body, 0)


def _mm_kernel(a_ref, w_ref, b_ref, s_ref, o_ref, *scratch):
    if scratch:
        (wbf_ref,) = scratch

        @pl.when(pl.program_id(1) == 0)
        def _():
            wbf_ref[...] = w_ref[...].astype(BF16)

        w = wbf_ref[...]
    else:
        w = w_ref[...]
    acc = jnp.dot(a_ref[...], w, preferred_element_type=F32)
    o_ref[...] = ((acc + b_ref[...]) * s_ref[...]).astype(o_ref.dtype)


def _matmul(a, w, layer, bias, scale, out_dtype, name):
    M, K = a.shape
    N = w.shape[2]
    tm = _pick(M, (1056, 1024, 768, 512, 384, 256, 128))
    tn = _pick(N, (512, 256, 128))
    scratch = [pltpu.VMEM((K, tn), BF16)] if w.dtype != BF16 else []
    return pl.pallas_call(
        _mm_kernel,
        grid=(N // tn, M // tm),
        in_specs=[
            pl.BlockSpec((tm, K), lambda j, i: (i, 0)),
            pl.BlockSpec((None, K, tn), lambda j, i: (layer, 0, j)),
            pl.BlockSpec((1, tn), lambda j, i: (0, j)),
            pl.BlockSpec((1, tn), lambda j, i: (0, j)),
        ],
        out_specs=pl.BlockSpec((tm, tn), lambda j, i: (i, j)),
        out_shape=jax.ShapeDtypeStruct((M, N), out_dtype),
        scratch_shapes=scratch,
        compiler_params=_params(2),
        name=name,
    )(a, w, bias, scale)


def _mm_nt_kernel(a_ref, wt_ref, b_ref, s_ref, o_ref, wbf_ref):
    @pl.when(pl.program_id(1) == 0)
    def _():
        _cast_rows(wt_ref.at[0], wbf_ref, 64)

    acc = _dot_nt(a_ref[...], wbf_ref[...])
    o_ref[...] = ((acc + b_ref[...]) * s_ref[...]).astype(o_ref.dtype)


def _in_proj(a, wt, layer, bias, scale, *, start, name):
    M, K = a.shape
    n = bias.shape[1]
    assert start % 8 == 0
    tm = _pick(M, (1056, 1024, 768, 512, 384, 256, 128))
    tn = _pick(n, (512, 256, 128))
    return pl.pallas_call(
        _mm_nt_kernel,
        grid=(n // tn, M // tm),
        in_specs=[
            pl.BlockSpec((tm, K), lambda j, i: (i, 0)),
            pl.BlockSpec((pl.Element(1), pl.Element(tn), pl.Element(K)),
                         lambda j, i: (layer, 8 * (start // 8 + j * (tn // 8)), 0)),
            pl.BlockSpec((1, tn), lambda j, i: (0, j)),
            pl.BlockSpec((1, tn), lambda j, i: (0, j)),
        ],
        out_specs=pl.BlockSpec((tm, tn), lambda j, i: (i, j)),
        out_shape=jax.ShapeDtypeStruct((M, n), BF16),
        scratch_shapes=[pltpu.VMEM((tn, K), BF16)],
        compiler_params=_params(2),
        name=name,
    )(a, wt, bias, scale)


def _prep_kernel(hb_ref, ws_ref, bs_ref, wa_ref, ba_ref, crow_ref, la_ref, carry_ref, *, pad, tt):
    i = pl.program_id(1)

    @pl.when(i == 0)
    def _():
        carry_ref[...] = jnp.zeros_like(carry_ref)

    z = _dot_nt(hb_ref[...], ws_ref[...].astype(BF16)) + bs_ref[...]
    pos = i * tt + lax.broadcasted_iota(jnp.int32, (tt, 1), 0)
    valid = pos >= pad
    la = _log_sigmoid(jnp.dot(z, wa_ref[...], preferred_element_type=F32, precision=HIGHEST) + ba_ref[...])
    la_ref[...] = jnp.where(valid, la * (1.0 / GLA_TAU), 0.0)
    lf = jnp.where(valid, _log_sigmoid(z), 0.0)

    r = lax.broadcasted_iota(jnp.int32, (ROW_TILE, ROW_TILE), 0)
    c = lax.broadcasted_iota(jnp.int32, (ROW_TILE, ROW_TILE), 1)
    tri = (r >= c).astype(F32)
    carry = carry_ref[...]
    for sb in range(tt // ROW_TILE):
        rows = slice(sb * ROW_TILE, (sb + 1) * ROW_TILE)
        cs = jnp.dot(tri, lf[rows], preferred_element_type=F32, precision=HIGHEST) + carry
        carry = cs[ROW_TILE - 1:ROW_TILE]
        posr = i * tt + sb * ROW_TILE + lax.broadcasted_iota(jnp.int32, (1, ROW_TILE), 1)
        crow_ref[0, :, rows] = jnp.where(posr >= pad, cs.T[0:8] * LOG2E, MASKED_KEY_BIAS)
    carry_ref[...] = carry


def _prep(hb, w_small, b_small, wa_ext, ba, *, B, lp, pad, tt):
    T, D = hb.shape
    KW = wa_ext.shape[1]
    nt = lp // tt
    return pl.pallas_call(
        functools.partial(_prep_kernel, pad=pad, tt=tt),
        grid=(B, nt),
        in_specs=[
            pl.BlockSpec((tt, D), lambda b, i: (b * nt + i, 0)),
            pl.BlockSpec((128, D), lambda b, i: (0, 0)),
            pl.BlockSpec((1, 128), lambda b, i: (0, 0)),
            pl.BlockSpec((128, KW), lambda b, i: (0, 0)),
            pl.BlockSpec((1, KW), lambda b, i: (0, 0)),
        ],
        out_specs=[
            pl.BlockSpec((1, 8, tt), lambda b, i: (b, 0, i)),
            pl.BlockSpec((tt, KW), lambda b, i: (b * nt + i, 0)),
        ],
        out_shape=[
            jax.ShapeDtypeStruct((B, 8, lp), F32),
            jax.ShapeDtypeStruct((T, KW), F32),
        ],
        scratch_shapes=[pltpu.VMEM((1, 128), F32)],
        compiler_params=_params(2),
        name="gate_prep",
    )(hb, w_small, b_small, wa_ext, ba)


def _fox_kernel(q_ref, k_ref, v_ref, crow_ref, o_ref, *, heads, tq, tk, group):
    qi = pl.program_id(1)
    n_full = (qi * tq) // tk
    kd = pl.multiple_of(n_full * tk, tk)
    rows = qi * tq + lax.broadcasted_iota(jnp.int32, (tq, tk), 0)
    cols = kd + lax.broadcasted_iota(jnp.int32, (tq, tk), 1)
    causal = cols <= rows
    ones = jnp.ones((tk, HEAD_DIM), BF16)

    def scores(h, k0):
        sl = slice(h * HEAD_DIM, (h + 1) * HEAD_DIM)
        cr = crow_ref[0, h:h + 1, pl.ds(k0, tk)]
        return _dot_nt(q_ref[:, sl], k_ref[pl.ds(k0, tk), sl]) - cr

    def weighted_values(h, p, k0):
        sl = slice(h * HEAD_DIM, (h + 1) * HEAD_DIM)
        va = jnp.concatenate([v_ref[pl.ds(k0, tk), sl], ones], axis=1)
        pv = jnp.dot(p, va, preferred_element_type=F32)
        return pv[:, :HEAD_DIM], pv[:, HEAD_DIM:HEAD_DIM + 1]

    for h0 in range(0, heads, group):
        hs = range(h0, min(h0 + group, heads))
        carry = []
        for h in hs:
            s = jnp.where(causal, scores(h, kd), -jnp.inf)
            m = jnp.max(s, axis=-1, keepdims=True)
            acc, l = weighted_values(h, jnp.exp2((s - m).astype(BF16)), kd)
            carry += [m, l, acc]

        def body(ki, carry):
            k0 = pl.multiple_of(ki * tk, tk)
            out = []
            for n, h in enumerate(hs):
                m, l, acc = carry[3 * n:3 * n + 3]
                s = scores(h, k0)
                m_new = jnp.maximum(m, jnp.max(s, axis=-1, keepdims=True))
                a = jnp.exp2(m - m_new)
                pv, p_sum = weighted_values(h, jnp.exp2((s - m_new).astype(BF16)), k0)
                out += [m_new, a * l + p_sum, a * acc + pv]
            return tuple(out)

        carry = lax.fori_loop(0, n_full, body, tuple(carry))
        for n, h in enumerate(hs):
            m, l, acc = carry[3 * n:3 * n + 3]
            o_ref[:, h * HEAD_DIM:(h + 1) * HEAD_DIM] = (acc / l).astype(o_ref.dtype)


def _fox(proj, crow, *, B, lp, fw, off_q, off_k, off_v, tq, tk):
    T = proj.shape[0]
    nq = lp // tq
    heads = fw // HEAD_DIM
    assert tk % tq == 0
    return pl.pallas_call(
        functools.partial(_fox_kernel, heads=heads, tq=tq, tk=tk, group=FOX_HEAD_GROUP),
        grid=(B, nq),
        in_specs=[
            pl.BlockSpec((tq, fw), lambda b, i: (b * nq + i, off_q // fw)),
            pl.BlockSpec((lp, fw), lambda b, i: (b, off_k // fw)),
            pl.BlockSpec((lp, fw), lambda b, i: (b, off_v // fw)),
            pl.BlockSpec((1, 8, lp), lambda b, i: (b, 0, 0)),
        ],
        out_specs=pl.BlockSpec((tq, fw), lambda b, i: (b * nq + i, 0)),
        out_shape=jax.ShapeDtypeStruct((T, fw), BF16),
        compiler_params=_params(2),
        name="fox_attention",
    )(proj, proj, proj, crow)


def _lru_kernel(lx_ref, lg_ref, cw_ref, cb_ref, wr_ref, br_ref, wi_ref, bi_ref, lam_ref, o_ref,
                ext_ref, a_ref, u_ref, hc_ref, *, pad, tt, nblk):
    i = pl.program_id(1)

    @pl.when(i == 0)
    def _():
        ext_ref[0:CONV_HALO, :] = jnp.zeros((CONV_HALO, ext_ref.shape[1]), F32)
        hc_ref[...] = jnp.zeros_like(hc_ref)

    @pl.when(i > 0)
    def _():
        ext_ref[0:CONV_HALO, :] = ext_ref[tt:tt + CONV_HALO, :]

    pos = i * tt + lax.broadcasted_iota(jnp.int32, (tt, 1), 0)
    valid = pos >= pad
    ext_ref[CONV_HALO:CONV_HALO + tt, :] = jnp.where(valid, lx_ref[...].astype(F32), 0.0)

    sp = jnp.maximum(-lam_ref[...], 0.0) + jnp.log1p(jnp.exp(-jnp.abs(lam_ref[...])))
    for n in range(nblk):
        sl = slice(n * HEAD_DIM, (n + 1) * HEAD_DIM)
        xc = cb_ref[:, sl]
        for j in range(CONV_W):
            start = CONV_HALO - (CONV_W - 1) + j
            xc = xc + cw_ref[j:j + 1, sl] * ext_ref[start:start + tt, sl]
        xb = xc.astype(BF16)
        r = _sigmoid(jnp.dot(xb, wr_ref[n].astype(BF16), preferred_element_type=F32) + br_ref[:, sl])
        g = _sigmoid(jnp.dot(xb, wi_ref[n].astype(BF16), preferred_element_type=F32) + bi_ref[:, sl])
        log_a = (-LRU_C) * r * sp[:, sl]
        a = jnp.exp(log_a)
        a_ref[:, sl] = a
        u = jnp.sqrt(-jnp.tanh(log_a) * (1.0 + a * a)) * (g * xc)
        u_ref[:, sl] = jnp.where(valid, u, 0.0)

    def body(gidx, h):
        r0 = pl.multiple_of(gidx * 8, 8)
        a8 = a_ref[pl.ds(r0, 8), :]
        u8 = u_ref[pl.ds(r0, 8), :]
        outs = []
        for r in range(8):
            h = a8[r:r + 1] * h + u8[r:r + 1]
            outs.append(h)
        u_ref[pl.ds(r0, 8), :] = jnp.concatenate(outs, axis=0)
        return h

    hc_ref[...] = lax.fori_loop(0, tt // 8, body, hc_ref[...])
    o_ref[...] = (u_ref[...] * _gelu_tanh(lg_ref[...].astype(F32))).astype(o_ref.dtype)


def _lru(proj, conv_w, conv_b, w_r, b_r, w_i, b_i, lam, *, B, lp, lw, off_x, off_g, pad, tt):
    T = proj.shape[0]
    nt = lp // tt
    nblk = lw // HEAD_DIM
    row = lambda b, i: (0, 0)
    return pl.pallas_call(
        functools.partial(_lru_kernel, pad=pad, tt=tt, nblk=nblk),
        grid=(B, nt),
        in_specs=[
            pl.BlockSpec((tt, lw), lambda b, i: (b * nt + i, off_x // lw)),
            pl.BlockSpec((tt, lw), lambda b, i: (b * nt + i, off_g // lw)),
            pl.BlockSpec((CONV_W, lw), row),
            pl.BlockSpec((1, lw), row),
            pl.BlockSpec((nblk, HEAD_DIM, HEAD_DIM), lambda b, i: (0, 0, 0)),
            pl.BlockSpec((1, lw), row),
            pl.BlockSpec((nblk, HEAD_DIM, HEAD_DIM), lambda b, i: (0, 0, 0)),
            pl.BlockSpec((1, lw), row),
            pl.BlockSpec((1, lw), row),
        ],
        out_specs=pl.BlockSpec((tt, lw), lambda b, i: (b * nt + i, 0)),
        out_shape=jax.ShapeDtypeStruct((T, lw), BF16),
        scratch_shapes=[
            pltpu.VMEM((tt + CONV_HALO, lw), F32),
            pltpu.VMEM((tt, lw), F32),
            pltpu.VMEM((tt, lw), F32),
            pltpu.VMEM((1, lw), F32),
        ],
        compiler_params=_params(2),
        name="conv_rglru",
    )(proj, proj, conv_w, conv_b, w_r, b_r, w_i, b_i, lam)


def _gla_kernel(q_ref, k_ref, v_ref, gg_ref, la_ref, ng_ref, o_ref, st_ref, *, pad, tt, heads):
    i = pl.program_id(1)

    @pl.when(i == 0)
    def _():
        st_ref[...] = jnp.zeros_like(st_ref)

    r = lax.broadcasted_iota(jnp.int32, (GLA_CHUNK, GLA_CHUNK), 0)
    c = lax.broadcasted_iota(jnp.int32, (GLA_CHUNK, GLA_CHUNK), 1)
    tri = (r >= c).astype(F32)
    for ci in range(tt // GLA_CHUNK):
        rows = slice(ci * GLA_CHUNK, (ci + 1) * GLA_CHUNK)
        pos = i * tt + ci * GLA_CHUNK + lax.broadcasted_iota(jnp.int32, (GLA_CHUNK, 1), 0)
        valid = pos >= pad
        for hd in range(heads):
            ks = slice(hd * HEAD_DIM, (hd + 1) * HEAD_DIM)
            vs = slice(hd * GLA_DV, (hd + 1) * GLA_DV)
            cs = jnp.dot(tri, la_ref[rows, ks], preferred_element_type=F32, precision=HIGHEST)
            cl = cs[GLA_CHUNK - 1:GLA_CHUNK]
            kdec = jnp.where(valid, k_ref[rows, ks].astype(F32) * jnp.exp(cl - cs), 0.0).astype(BF16)
            ut = lax.dot_general(v_ref[rows, vs], kdec, (((0,), (0,)), ((), ())), preferred_element_type=F32)
            st = st_ref[hd] * jnp.exp(cl) + ut
            st_ref[hd] = st
            o = _dot_nt(q_ref[rows, ks], st.astype(BF16))
            o = o * lax.rsqrt(jnp.mean(o * o, axis=-1, keepdims=True) + RMS_EPS) * ng_ref[:, vs]
            gg = gg_ref[rows, vs].astype(F32)
            o_ref[rows, vs] = (o * (gg * _sigmoid(gg))).astype(o_ref.dtype)


def _gla(proj, proj_g, la, norm_g, *, B, lp, kw, vw, off_q, off_k, off_v, off_g, pad, tt):
    T = proj.shape[0]
    nt = lp // tt
    heads = kw // HEAD_DIM
    return pl.pallas_call(
        functools.partial(_gla_kernel, pad=pad, tt=tt, heads=heads),
        grid=(B, nt),
        in_specs=[
            pl.BlockSpec((tt, kw), lambda b, i: (b * nt + i, off_q // kw)),
            pl.BlockSpec((tt, kw), lambda b, i: (b * nt + i, off_k // kw)),
            pl.BlockSpec((tt, vw), lambda b, i: (b * nt + i, off_v // vw)),
            pl.BlockSpec((tt, vw), lambda b, i: (b * nt + i, off_g // vw)),
            pl.BlockSpec((tt, kw), lambda b, i: (b * nt + i, 0)),
            pl.BlockSpec((1, vw), lambda b, i: (0, 0)),
        ],
        out_specs=pl.BlockSpec((tt, vw), lambda b, i: (b * nt + i, 0)),
        out_shape=jax.ShapeDtypeStruct((T, vw), BF16),
        scratch_shapes=[pltpu.VMEM((heads, GLA_DV, HEAD_DIM), F32)],
        compiler_params=_params(2),
        name="gla_chunked",
    )(proj, proj, proj, proj_g, la, norm_g)


def _merge_kernel(of_ref, ol_ref, og_ref, w_ref, g0_ref, g1_ref, g2_ref, o_ref, wbf_ref, *, fw, lw):
    @pl.when(pl.program_id(1) == 0)
    def _():
        wbf_ref[...] = w_ref[...].astype(BF16)

    y0 = jnp.dot(of_ref[...], wbf_ref[0:fw, :], preferred_element_type=F32)
    y1 = jnp.dot(ol_ref[...], wbf_ref[fw:fw + lw, :], preferred_element_type=F32)
    y2 = jnp.dot(og_ref[...], wbf_ref[fw + lw:, :], preferred_element_type=F32)
    out = (_sigmoid(g0_ref[...].astype(F32)) * y0 + _sigmoid(g1_ref[...].astype(F32)) * y1
           + _sigmoid(g2_ref[...].astype(F32)) * y2)
    o_ref[...] = out.astype(o_ref.dtype)


def _merge(o_fox, o_lru, o_gla, w_branch, layer, proj, *, off_gates, d):
    T, fw = o_fox.shape
    lw = o_lru.shape[1]
    vw = o_gla.shape[1]
    tm = _pick(T, (1056, 1024, 768, 512, 384, 256, 128))
    tn = _pick(math.gcd(d, off_gates), (512, 256, 128))
    gate_spec = lambda b: pl.BlockSpec((tm, tn), lambda j, i: (i, (off_gates + b * d) // tn + j))
    return pl.pallas_call(
        functools.partial(_merge_kernel, fw=fw, lw=lw),
        grid=(d // tn, T // tm),
        in_specs=[
            pl.BlockSpec((tm, fw), lambda j, i: (i, 0)),
            pl.BlockSpec((tm, lw), lambda j, i: (i, 0)),
            pl.BlockSpec((tm, vw), lambda j, i: (i, 0)),
            pl.BlockSpec((None, fw + lw + vw, tn), lambda j, i: (layer, 0, j)),
            gate_spec(0), gate_spec(1), gate_spec(2),
        ],
        out_specs=pl.BlockSpec((tm, tn), lambda j, i: (i, j)),
        out_shape=jax.ShapeDtypeStruct((T, d), BF16),
        scratch_shapes=[pltpu.VMEM((fw + lw + vw, tn), BF16)],
        compiler_params=_params(2),
        name="branch_merge",
    )(o_fox, o_lru, o_gla, w_branch, proj, proj, proj)


def _ln_router_kernel(h_ref, mix_ref, g_ref, b_ref, wr_ref, br_ref, hf_ref, hb_ref, idx_ref, wt_ref, *, alpha):
    y = _layer_norm(alpha * h_ref[...] + mix_ref[...], g_ref[...], b_ref[...])
    hf_ref[...] = y
    hb_ref[...] = y.astype(BF16)
    logits = jnp.dot(y, wr_ref[...], preferred_element_type=F32, precision=HIGHEST) + br_ref[...]
    lane = lax.broadcasted_iota(jnp.int32, logits.shape, 1).astype(F32)
    idx = jnp.zeros_like(logits)
    vals = []
    for k in range(TOP_K):
        mx = jnp.max(logits, axis=-1, keepdims=True)
        sel = jnp.min(jnp.where(logits == mx, lane, float(logits.shape[1])), axis=-1, keepdims=True)
        vals.append(mx)
        idx = jnp.where(lane == float(k), sel, idx)
        logits = jnp.where(lane == sel, -jnp.inf, logits)
    es = [jnp.exp(v - vals[0]) for v in vals]
    tot = es[0]
    for e in es[1:]:
        tot = tot + e
    wt = jnp.zeros_like(logits)
    for k in range(TOP_K):
        wt = jnp.where(lane == float(k), es[k] / tot, wt)
    idx_ref[...] = idx.astype(jnp.int32)
    wt_ref[...] = wt


def _ln_router(h, mix, g, b, wr_pad, br_pad, *, alpha):
    T, D = h.shape
    tm = _pick(T, (192, 128))
    blk = pl.BlockSpec((tm, D), lambda i: (i, 0))
    row = pl.BlockSpec((1, D), lambda i: (0, 0))
    small = pl.BlockSpec((tm, 128), lambda i: (i, 0))
    return pl.pallas_call(
        functools.partial(_ln_router_kernel, alpha=alpha),
        grid=(T // tm,),
        in_specs=[blk, blk, row, row, pl.BlockSpec((D, 128), lambda i: (0, 0)), pl.BlockSpec((1, 128), lambda i: (0, 0))],
        out_specs=[blk, blk, small, small],
        out_shape=[
            jax.ShapeDtypeStruct((T, D), F32),
            jax.ShapeDtypeStruct((T, D), BF16),
            jax.ShapeDtypeStruct((T, 128), jnp.int32),
            jax.ShapeDtypeStruct((T, 128), F32),
        ],
        compiler_params=_params(1),
        name="ln_router",
    )(h, mix, g, b, wr_pad, br_pad)


def _row_gather(src_hbm, idx_ref, base, buf_ref, sem, n):
    def body(r, carry):
        tok = idx_ref[base + r]
        pltpu.make_async_copy(src_hbm.at[pl.ds(tok, 1)], buf_ref.at[pl.ds(r, 1)], sem).start()
        return carry

    lax.fori_loop(0, n, body, 0, unroll=8)


def _moe_up_kernel(te_ref, na_ref, first_ref, nxt_ref, tok_ref, h_hbm, wup_hbm, bup_ref, sel_ref, act_ref,
                   xbuf_ref, sem_ref, stage_ref, wbf_ref, wsem_ref, *, tmx, layer):
    i = pl.program_id(0)
    na = na_ref[0]
    slot = lax.rem(i, 2)

    def weight_copy(e):
        return pltpu.make_async_copy(wup_hbm.at[layer, e], stage_ref, wsem_ref.at[0])

    @pl.when(jnp.logical_and(i == 0, na > 0))
    def _():
        _row_gather(h_hbm, tok_ref, 0, xbuf_ref.at[0], sem_ref.at[0], tmx)
        weight_copy(te_ref[0]).start()

    @pl.when(i + 1 < na)
    def _():
        _row_gather(h_hbm, tok_ref, (i + 1) * tmx, xbuf_ref.at[1 - slot], sem_ref.at[1 - slot], tmx)

    @pl.when(jnp.logical_and(i < na, first_ref[i] == 1))
    def _():
        weight_copy(te_ref[i]).wait()
        _cast_rows(stage_ref, wbf_ref, 256)

        @pl.when(nxt_ref[i] >= 0)
        def _():
            weight_copy(nxt_ref[i]).start()

    @pl.when(i < na)
    def _():
        pltpu.make_async_copy(h_hbm.at[pl.ds(0, tmx)], xbuf_ref.at[slot], sem_ref.at[slot]).wait()
        x = xbuf_ref[slot].astype(BF16)
        h = jnp.dot(x, wbf_ref[...], preferred_element_type=F32) + bup_ref[...]
        g = jnp.minimum(h, SWIGLU_LIMIT)
        u = jnp.clip(pltpu.roll(h, h.shape[1] - 1, axis=1), -SWIGLU_LIMIT, SWIGLU_LIMIT)
        act = ((u + 1.0) * g * _sigmoid(SWIGLU_ALPHA * g)).astype(BF16)
        act_ref[...] = jnp.dot(act, sel_ref[...], preferred_element_type=F32).astype(BF16)

    @pl.when(i >= na)
    def _():
        act_ref[...] = jnp.zeros_like(act_ref)


def _moe_down_kernel(te_ref, na_ref, first_ref, act_ref, wd_ref, bd_ref, y_ref, wbf_ref):
    i = pl.program_id(0)
    na = na_ref[0]

    @pl.when(jnp.logical_and(i < na, first_ref[i] == 1))
    def _():
        _cast_rows(wd_ref, wbf_ref, 128)

    @pl.when(i < na)
    def _():
        y_ref[...] = jnp.dot(act_ref[...], wbf_ref[...], preferred_element_type=F32) + bd_ref[...]

    @pl.when(i >= na)
    def _():
        y_ref[...] = jnp.zeros_like(y_ref)


def _moe(tile_expert, n_active, row_token, hf, w_up, b_up, sel, w_down, b_down, *, tmx, layer):
    T, D = hf.shape
    _, E, _, F2 = w_up.shape
    F = F2 // 2
    P = row_token.shape[0]
    ntiles = P // tmx
    tiles = jnp.arange(ntiles, dtype=jnp.int32)
    first = jnp.where(tiles == 0, 1, (tile_expert != jnp.roll(tile_expert, 1)).astype(jnp.int32))
    j = jnp.searchsorted(tile_expert, tile_expert, side="right").astype(jnp.int32)
    nxt = jnp.where(j < ntiles, tile_expert[jnp.minimum(j, ntiles - 1)], -1).astype(jnp.int32)

    act = pl.pallas_call(
        functools.partial(_moe_up_kernel, tmx=tmx, layer=layer),
        grid_spec=pltpu.PrefetchScalarGridSpec(
            num_scalar_prefetch=5,
            grid=(ntiles,),
            in_specs=[
                pl.BlockSpec(memory_space=pl.ANY),
                pl.BlockSpec(memory_space=pl.ANY),
                pl.BlockSpec((None, 1, F2), lambda i, te, *_: (te[i], 0, 0)),
                pl.BlockSpec((F2, F), lambda i, *_: (0, 0)),
            ],
            out_specs=pl.BlockSpec((tmx, F), lambda i, *_: (i, 0)),
            scratch_shapes=[
                pltpu.VMEM((2, tmx, D), F32), pltpu.SemaphoreType.DMA((2,)),
                pltpu.VMEM((D, F2), F32), pltpu.VMEM((D, F2), BF16), pltpu.SemaphoreType.DMA((1,)),
            ],
        ),
        out_shape=jax.ShapeDtypeStruct((P, F), BF16),
        compiler_params=_params(1),
        name="moe_up",
    )(tile_expert, n_active, first, nxt, row_token, hf, w_up, b_up, sel)

    return pl.pallas_call(
        _moe_down_kernel,
        grid_spec=pltpu.PrefetchScalarGridSpec(
            num_scalar_prefetch=3,
            grid=(ntiles,),
            in_specs=[
                pl.BlockSpec((tmx, F), lambda i, *_: (i, 0)),
                pl.BlockSpec((None, None, F, D), lambda i, te, *_: (layer, te[i], 0, 0)),
                pl.BlockSpec((None, 1, D), lambda i, te, *_: (te[i], 0, 0)),
            ],
            out_specs=pl.BlockSpec((tmx, D), lambda i, *_: (i, 0)),
            scratch_shapes=[pltpu.VMEM((F, D), BF16)],
        ),
        out_shape=jax.ShapeDtypeStruct((P, D), F32),
        compiler_params=_params(1),
        name="moe_down",
    )(tile_expert, n_active, first, act, w_down, b_down)


def _combine_kernel(pos_ref, y_hbm, h_ref, w_ref, g_ref, b_ref, *rest, alpha, tc, nt, final):
    if final:
        out_ref, buf_ref, sem_ref = rest
    else:
        hf_ref, hb_ref, buf_ref, sem_ref = rest
    i = pl.program_id(0)
    slot = lax.rem(i, 2)

    def start(tile, s):
        for k in range(TOP_K):
            _row_gather(y_hbm, pos_ref, (k * nt + tile) * tc, buf_ref.at[s, k], sem_ref.at[s], tc)

    @pl.when(i == 0)
    def _():
        start(0, 0)

    @pl.when(i + 1 < nt)
    def _():
        start(i + 1, 1 - slot)

    for k in range(TOP_K):
        pltpu.make_async_copy(y_hbm.at[pl.ds(0, tc)], buf_ref.at[slot, k], sem_ref.at[slot]).wait()
    ffn = w_ref[:, 0:1] * buf_ref[slot, 0]
    for k in range(1, TOP_K):
        ffn = ffn + w_ref[:, k:k + 1] * buf_ref[slot, k]
    y = _layer_norm(alpha * h_ref[...] + ffn, g_ref[...], b_ref[...])
    if final:
        out_ref[...] = y
    else:
        hf_ref[...] = y
        hb_ref[...] = y.astype(BF16)


def _combine(pos, ys, hf, top_w, g, b, *, alpha, lp, final):
    T, D = hf.shape
    tc = ROW_TILE
    nt = T // tc
    ntb = lp // tc
    if final:
        out_specs = pl.BlockSpec((tc, D), lambda i, pos: ((i // ntb) * (ntb - 1) + jnp.maximum(i % ntb - 1, 0), 0))
        out_shape = jax.ShapeDtypeStruct(((T // lp) * (lp - tc), D), F32)
    else:
        out_specs = [pl.BlockSpec((tc, D), lambda i, pos: (i, 0)), pl.BlockSpec((tc, D), lambda i, pos: (i, 0))]
        out_shape = [jax.ShapeDtypeStruct((T, D), F32), jax.ShapeDtypeStruct((T, D), BF16)]
    grid_spec = pltpu.PrefetchScalarGridSpec(
        num_scalar_prefetch=1,
        grid=(nt,),
        in_specs=[
            pl.BlockSpec(memory_space=pl.ANY),
            pl.BlockSpec((tc, D), lambda i, pos: (i, 0)),
            pl.BlockSpec((tc, 128), lambda i, pos: (i, 0)),
            pl.BlockSpec((1, D), lambda i, pos: (0, 0)),
            pl.BlockSpec((1, D), lambda i, pos: (0, 0)),
        ],
        out_specs=out_specs,
        scratch_shapes=[pltpu.VMEM((2, TOP_K, tc, D), F32), pltpu.SemaphoreType.DMA((2,))],
    )
    return pl.pallas_call(
        functools.partial(_combine_kernel, alpha=alpha, tc=tc, nt=nt, final=final),
        grid_spec=grid_spec,
        out_shape=out_shape,
        compiler_params=_params(1),
        name="moe_combine_ln",
    )(pos, ys, hf, top_w, g, b)


def _route_kernel(idx_ref, pos_ref, te_ref, na_ref, cnt_ref, pst_ref, *, pad, tr, tmx, n_experts, spare_row):
    p = pl.program_id(0)
    i = pl.program_id(2)
    first = jnp.logical_and(pl.program_id(1) == 0, i == 0)

    @pl.when(jnp.logical_and(p == 0, first))
    def _():
        cnt_ref[...] = jnp.zeros_like(cnt_ref)

    @pl.when(jnp.logical_and(p == 1, first))
    def _():
        cnt = cnt_ref[...]
        padded = jnp.floor((cnt + (tmx - 1.0)) * (1.0 / tmx)) * tmx
        r = lax.broadcasted_iota(jnp.int32, (128, 128), 0)
        c = lax.broadcasted_iota(jnp.int32, (128, 128), 1)
        pst = jnp.dot(padded, (r < c).astype(F32), preferred_element_type=F32, precision=HIGHEST)
        pst_ref[...] = pst
        cnt_ref[...] = jnp.zeros_like(cnt_ref)
        tile_end = (pst + padded) * (1.0 / tmx)
        nt_pad = te_ref.shape[0]
        t = lax.broadcasted_iota(jnp.int32, (nt_pad, 128), 0).astype(F32)
        lane = lax.broadcasted_iota(jnp.int32, (nt_pad, 128), 1)
        is_expert = lane < n_experts
        te = jnp.sum(jnp.where(jnp.logical_and(tile_end <= t, is_expert), 1.0, 0.0), axis=-1, keepdims=True)
        lane1 = lax.broadcasted_iota(jnp.int32, (1, 128), 1).astype(F32)
        e_last = jnp.max(jnp.where(cnt > 0.0, lane1, 0.0), axis=-1, keepdims=True)
        te_ref[...] = jnp.broadcast_to(jnp.minimum(te, e_last), (nt_pad, 128)).astype(jnp.int32)
        na_ref[...] = jnp.broadcast_to(jnp.max(tile_end, axis=-1, keepdims=True), (1, 128)).astype(jnp.int32)

    idx = idx_ref[...]
    lane = lax.broadcasted_iota(jnp.int32, (tr, 128), 1)
    valid = (i * tr + lax.broadcasted_iota(jnp.int32, (tr, 1), 0)) >= pad
    onehot = jnp.zeros((tr, 128), F32)
    for k in range(TOP_K):
        onehot = onehot + jnp.where(lane == idx[:, k:k + 1], 1.0, 0.0)
    onehot = jnp.where(valid, onehot, 0.0)
    r = lax.broadcasted_iota(jnp.int32, (tr, tr), 0)
    c = lax.broadcasted_iota(jnp.int32, (tr, tr), 1)
    rank = jnp.dot((r > c).astype(BF16), onehot.astype(BF16), preferred_element_type=F32) + cnt_ref[...]
    cnt_ref[...] = cnt_ref[...] + jnp.sum(onehot, axis=0, keepdims=True)

    @pl.when(p == 0)
    def _():
        pos_ref[...] = jnp.zeros_like(pos_ref)

    @pl.when(p == 1)
    def _():
        dest = rank + pst_ref[...]
        out = jnp.zeros((tr, 128), F32)
        for k in range(TOP_K):
            d = jnp.sum(jnp.where(lane == idx[:, k:k + 1], dest, 0.0), axis=-1, keepdims=True)
            out = jnp.where(lane == k, jnp.where(valid, d, float(spare_row)), out)
        pos_ref[...] = out.astype(jnp.int32)


def _route(top_idx, *, B, lp, pad, tmx, n_experts, n_rows):
    T = top_idx.shape[0]
    tr = _pick(lp, (384, 256, 128))
    nt = lp // tr
    nt_pad = -(-(n_rows // tmx) // 8) * 8
    pos, te, na = pl.pallas_call(
        functools.partial(_route_kernel, pad=pad, tr=tr, tmx=tmx, n_experts=n_experts, spare_row=n_rows - 1),
        grid=(2, B, nt),
        in_specs=[pl.BlockSpec((tr, 128), lambda p, b, i: (b * nt + i, 0))],
        out_specs=[
            pl.BlockSpec((tr, 128), lambda p, b, i: (p * (b * nt + i), 0)),
            pl.BlockSpec((nt_pad, 128), lambda p, b, i: (0, 0)),
            pl.BlockSpec((1, 128), lambda p, b, i: (0, 0)),
        ],
        out_shape=[
            jax.ShapeDtypeStruct((T, 128), jnp.int32),
            jax.ShapeDtypeStruct((nt_pad, 128), jnp.int32),
            jax.ShapeDtypeStruct((1, 128), jnp.int32),
        ],
        scratch_shapes=[pltpu.VMEM((1, 128), F32), pltpu.VMEM((1, 128), F32)],
        compiler_params=_params(3),
        name="route_rank",
    )(top_idx)
    pos_flat = pos[:, :TOP_K].T.reshape(-1)
    return pos_flat, te[:n_rows // tmx, 0], na[0, :1]


def _row_token_kernel(pos_ref, rt_ref, *, t):
    def zero(r, carry):
        rt_ref[r] = 0
        return carry

    lax.fori_loop(0, rt_ref.shape[0], zero, 0, unroll=8)
    for k in range(TOP_K):
        def body(tok, carry):
            rt_ref[pos_ref[k * t + tok]] = tok
            return carry

        lax.fori_loop(0, t, body, 0, unroll=8)


def _row_token(pos_flat, *, t, n_rows):
    return pl.pallas_call(
        functools.partial(_row_token_kernel, t=t),
        grid_spec=pltpu.PrefetchScalarGridSpec(
            num_scalar_prefetch=1, grid=(1,), in_specs=[],
            out_specs=pl.BlockSpec(memory_space=pltpu.SMEM)),
        out_shape=jax.ShapeDtypeStruct((n_rows,), jnp.int32),
        compiler_params=_params(1),
        name="route_row_token",
    )(pos_flat)


def _segments(fw, fh, lw, kw, vw, rank, d):
    names = ("fq", "fk", "fv", "ff", "lx", "lg", "gq", "gk", "gv", "ga", "gg", "gate0", "gate1", "gate2")
    widths = (fw, fw, fw, fh, lw, lw, kw, kw, vw, rank, vw, d, d, d)
    segs, off = {}, 0
    for n, w in zip(names, widths):
        segs[n] = (off, w)
        off += w
    return segs, off


def kernel(x, meta_tokens, emb_ln_g, emb_ln_b, w_in, b_in, conv_w, conv_b, lru_w_r, lru_b_r, lru_w_i, lru_b_i,
           lru_lambda, gla_w_alpha, gla_b_alpha, gla_norm_g, w_branch, w_out, b_out, ln1_g, ln1_b, w_router,
           b_router, w_up, b_up, w_down, b_down, ln2_g, ln2_b):
    B, S, D = x.shape
    n_meta = meta_tokens.shape[0]
    depth = w_in.shape[0]
    L = S + n_meta
    pad = (-L) % ROW_TILE
    lp = L + pad
    assert pad + n_meta == ROW_TILE and S % ROW_TILE == 0
    T = B * lp

    lw = conv_w.shape[2]
    rank, kw = gla_w_alpha.shape[1:]
    vw = gla_norm_g.shape[1]
    n_experts = w_router.shape[2]
    fexp = w_up.shape[3] // 2
    in_cols = w_in.shape[2]
    fh = (in_cols - 2 * lw - 2 * kw - 2 * vw - rank - 3 * D) // (3 * HEAD_DIM + 1)
    fw = fh * HEAD_DIM
    segs, total = _segments(fw, fh, lw, kw, vw, rank, D)
    assert total == in_cols and fh <= 8 and fh + rank <= 128
    alpha = (2.0 * depth) ** 0.25

    windows = (("fq", "fk", "fv"), ("lx", "lg", "gq", "gk", "gv"), ("gg", "gate0", "gate1", "gate2"))
    qscale = HEAD_DIM ** -0.5
    off, win_start, win_scale = {}, [], []
    for names in windows:
        start = segs[names[0]][0]
        for n in names:
            off[n] = segs[n][0] - start
            assert off[n] % segs[n][1] == 0 or n.startswith("gate")
        win_start.append(start)
        scales = {"fq": qscale * LOG2E, "gq": qscale}
        win_scale.append(jnp.concatenate(
            [jnp.full((segs[n][1],), scales.get(n, 1.0), F32) for n in names]).reshape(1, -1))

    def cols(a, n):
        o, w = segs[n]
        return a[..., o:o + w]

    tt = _pick(lp, (384, 256, 128))
    tmx = MOE_ROW_TILE if T >= 4096 else 128
    n_rows = (-(-(B * L * TOP_K + n_experts * (tmx - 1)) // tmx) + 1) * tmx

    head = jnp.concatenate([jnp.zeros((pad, D), F32), meta_tokens.astype(F32)], axis=0)
    hf, hb = _embed_ln(x, head, emb_ln_g.reshape(1, D), emb_ln_b.reshape(1, D), pad=pad, lp=lp)
    sel = (jnp.arange(2 * fexp)[:, None] == 2 * jnp.arange(fexp)[None, :]).astype(BF16)

    w_in_t = jnp.swapaxes(w_in, 1, 2)

    def rows(l, n):
        o, w = segs[n]
        return w_in_t[l, o:o + w]

    for l in range(depth):
        w_small = jnp.concatenate(
            [rows(l, "ff"), jnp.zeros((8 - fh, D), F32), rows(l, "ga"), jnp.zeros((128 - 8 - rank, D), F32)], axis=0)
        b_small = jnp.concatenate(
            [cols(b_in[l], "ff"), jnp.zeros((8 - fh,), F32), cols(b_in[l], "ga"),
             jnp.zeros((128 - 8 - rank,), F32)]).reshape(1, 128)
        wa_ext = jnp.zeros((128, kw), F32).at[8:8 + rank].set(gla_w_alpha[l])

        pa, pb, pc = [
            _in_proj(hb, w_in_t, l, b_in[l][s:s + sc.shape[1]].reshape(1, -1), sc, start=s, name=f"in_proj_{k}")
            for k, (s, sc) in enumerate(zip(win_start, win_scale))]
        crow, la = _prep(hb, w_small, b_small, wa_ext, gla_b_alpha[l].reshape(1, kw), B=B, lp=lp, pad=pad, tt=tt)
        o_fox = _fox(pa, crow, B=B, lp=lp, fw=fw, off_q=off["fq"], off_k=off["fk"], off_v=off["fv"], tq=FOX_Q_TILE,
                     tk=tt)
        o_lru = _lru(pb, conv_w[l], conv_b[l].reshape(1, lw), lru_w_r[l], lru_b_r[l].reshape(1, lw), lru_w_i[l],
                     lru_b_i[l].reshape(1, lw), lru_lambda[l].reshape(1, lw), B=B, lp=lp, lw=lw, off_x=off["lx"],
                     off_g=off["lg"], pad=pad, tt=tt)
        o_gla = _gla(pb, pc, la, gla_norm_g[l].reshape(1, vw), B=B, lp=lp, kw=kw, vw=vw, off_q=off["gq"],
                     off_k=off["gk"], off_v=off["gv"], off_g=off["gg"], pad=pad, tt=tt)
        merged = _merge(o_fox, o_lru, o_gla, w_branch, l, pc, off_gates=off["gate0"], d=D)
        mix = _matmul(merged, w_out, l, b_out[l].reshape(1, D), jnp.ones((1, D), F32), F32, "out_proj")

        wr_pad = jnp.zeros((D, 128), F32).at[:, :n_experts].set(w_router[l])
        br_pad = jnp.full((1, 128), -MASKED_KEY_BIAS, F32).at[0, :n_experts].set(b_router[l])
        hf, hb, top_idx, top_w = _ln_router(hf, mix, ln1_g[l].reshape(1, D), ln1_b[l].reshape(1, D), wr_pad, br_pad,
                                            alpha=alpha)
        pos, tile_expert, n_active = _route(top_idx, B=B, lp=lp, pad=pad, tmx=tmx, n_experts=n_experts, n_rows=n_rows)
        row_token = _row_token(pos, t=T, n_rows=n_rows)
        ys = _moe(tile_expert, n_active, row_token, hf, w_up, b_up[l].reshape(n_experts, 1, 2 * fexp), sel, w_down,
                  b_down[l].reshape(n_experts, 1, D), tmx=tmx, layer=l)
        final = l == depth - 1
        res = _combine(pos, ys, hf, top_w, ln2_g[l].reshape(1, D), ln2_b[l].reshape(1, D), alpha=alpha, lp=lp,
                       final=final)
        if final:
            return res.reshape(B, S, D)
        hf, hb = res
```

```python
import functools
import math

import jax
import jax.numpy as jnp
from jax import lax
from jax.experimental import pallas as pl
from jax.experimental.pallas import tpu as pltpu

F32 = jnp.float32
BF16 = jnp.bfloat16
U32 = jnp.uint32
HIGHEST = lax.Precision.HIGHEST

ROW_TILE = 128
HEAD_DIM = 128
GLA_DV = 256
GLA_CHUNK = 64
CONV_W = 4
CONV_HALO = 8
LN_EPS = 1e-5
RMS_EPS = 1e-6
LRU_C = 8.0
GLA_TAU = 16.0
SWIGLU_LIMIT = 7.0
SWIGLU_ALPHA = 1.702
TOP_K = 4
LOG2E = 1.4426950408889634
MASKED_KEY_BIAS = 1e30
MOE_ROW_TILE = 256
FOX_HEAD_GROUP = 4
FOX_Q_TILE = 384
VMEM_LIMIT = 56 * 1024 * 1024


def _pick(n, candidates):
    for c in candidates:
        if n % c == 0:
            return c
    raise ValueError(f"no tile in {candidates} divides {n}")


def _params(n_axes, vmem=VMEM_LIMIT):
    return pltpu.CompilerParams(dimension_semantics=("arbitrary",) * n_axes, vmem_limit_bytes=vmem)


def _layer_norm(v, g, b):
    mu = jnp.mean(v, axis=-1, keepdims=True)
    d = v - mu
    var = jnp.mean(d * d, axis=-1, keepdims=True)
    return d * lax.rsqrt(var + LN_EPS) * g + b


def _log_sigmoid(x):
    return jnp.minimum(x, 0.0) - jnp.log1p(jnp.exp(-jnp.abs(x)))


def _sigmoid(x):
    return 1.0 / (1.0 + jnp.exp(-x))


def _gelu_tanh(x):
    return 0.5 * x * (1.0 + jnp.tanh(0.7978845608028654 * (x + 0.044715 * (x * x * x))))


def _embed_ln_kernel(x_ref, head_ref, g_ref, b_ref, hf_ref, hb_ref, *, pad):
    i = pl.program_id(1)

    @pl.when(i == 0)
    def _():
        y = _layer_norm(head_ref[...], g_ref[...], b_ref[...])
        rows = lax.broadcasted_iota(jnp.int32, (ROW_TILE, 1), 0)
        y = jnp.where(rows >= pad, y, 0.0)
        hf_ref[...] = y
        hb_ref[...] = y.astype(BF16)

    @pl.when(i > 0)
    def _():
        y = _layer_norm(x_ref[0], g_ref[...], b_ref[...])
        hf_ref[...] = y
        hb_ref[...] = y.astype(BF16)


def _embed_ln(x, head, g, b, *, pad, lp):
    B, S, D = x.shape
    nt = lp // ROW_TILE
    T = B * lp
    return pl.pallas_call(
        functools.partial(_embed_ln_kernel, pad=pad),
        grid=(B, nt),
        in_specs=[
            pl.BlockSpec((1, ROW_TILE, D), lambda b, i: (b, jnp.maximum(i - 1, 0), 0)),
            pl.BlockSpec((ROW_TILE, D), lambda b, i: (0, 0)),
            pl.BlockSpec((1, D), lambda b, i: (0, 0)),
            pl.BlockSpec((1, D), lambda b, i: (0, 0)),
        ],
        out_specs=[
            pl.BlockSpec((ROW_TILE, D), lambda b, i: (b * nt + i, 0)),
            pl.BlockSpec((ROW_TILE, D), lambda b, i: (b * nt + i, 0)),
        ],
        out_shape=[jax.ShapeDtypeStruct((T, D), F32), jax.ShapeDtypeStruct((T, D), BF16)],
        compiler_params=_params(2),
        name="embed_ln",
    )(x, head, g, b)


def _dot_nt(a, b):
    return lax.dot_general(a, b, (((1,), (1,)), ((), ())), preferred_element_type=F32)


def _cast_rows(src_ref, dst_ref, chunk):
    def body(c, carry):
        r0 = pl.multiple_of(c * chunk, chunk)
        dst_ref[pl.ds(r0, chunk), :] = src_ref[pl.ds(r0, chunk), :].astype(BF16)
        return carry

    lax.fori_loop(0, src_ref.shape[0] // chunk, body, 0)


def _mm_kernel(a_ref, w_ref, b_ref, s_ref, o_ref, *scratch):
    if scratch:
        (wbf_ref,) = scratch

        @pl.when(pl.program_id(1) == 0)
        def _():
            wbf_ref[...] = w_ref[...].astype(BF16)

        w = wbf_ref[...]
    else:
        w = w_ref[...]
    acc = jnp.dot(a_ref[...], w, preferred_element_type=F32)
    o_ref[...] = ((acc + b_ref[...]) * s_ref[...]).astype(o_ref.dtype)


def _matmul(a, w, layer, bias, scale, out_dtype, name):
    M, K = a.shape
    N = w.shape[2]
    tm = _pick(M, (1056, 1024, 768, 512, 384, 256, 128))
    tn = _pick(N, (512, 256, 128))
    scratch = [pltpu.VMEM((K, tn), BF16)] if w.dtype != BF16 else []
    return pl.pallas_call(
        _mm_kernel,
        grid=(N // tn, M // tm),
        in_specs=[
            pl.BlockSpec((tm, K), lambda j, i: (i, 0)),
            pl.BlockSpec((None, K, tn), lambda j, i: (layer, 0, j)),
            pl.BlockSpec((1, tn), lambda j, i: (0, j)),
            pl.BlockSpec((1, tn), lambda j, i: (0, j)),
        ],
        out_specs=pl.BlockSpec((tm, tn), lambda j, i: (i, j)),
        out_shape=jax.ShapeDtypeStruct((M, N), out_dtype),
        scratch_shapes=scratch,
        compiler_params=_params(2),
        name=name,
    )(a, w, bias, scale)


def _mm_nt_kernel(a_ref, wt_ref, b_ref, s_ref, o_ref, wbf_ref):
    @pl.when(pl.program_id(1) == 0)
    def _():
        _cast_rows(wt_ref.at[0], wbf_ref, 64)

    acc = _dot_nt(a_ref[...], wbf_ref[...])
    o_ref[...] = ((acc + b_ref[...]) * s_ref[...]).astype(o_ref.dtype)


def _in_proj(a, wt, layer, bias, scale, *, start, name):
    M, K = a.shape
    n = bias.shape[1]
    assert start % 8 == 0
    tm = _pick(M, (1056, 1024, 768, 512, 384, 256, 128))
    tn = _pick(n, (512, 256, 128))
    return pl.pallas_call(
        _mm_nt_kernel,
        grid=(n // tn, M // tm),
        in_specs=[
            pl.BlockSpec((tm, K), lambda j, i: (i, 0)),
            pl.BlockSpec((pl.Element(1), pl.Element(tn), pl.Element(K)),
                         lambda j, i: (layer, 8 * (start // 8 + j * (tn // 8)), 0)),
            pl.BlockSpec((1, tn), lambda j, i: (0, j)),
            pl.BlockSpec((1, tn), lambda j, i: (0, j)),
        ],
        out_specs=pl.BlockSpec((tm, tn), lambda j, i: (i, j)),
        out_shape=jax.ShapeDtypeStruct((M, n), BF16),
        scratch_shapes=[pltpu.VMEM((tn, K), BF16)],
        compiler_params=_params(2),
        name=name,
    )(a, wt, bias, scale)


def _prep_kernel(hb_ref, ws_ref, bs_ref, wa_ref, ba_ref, crow_ref, la_ref, carry_ref, *, pad, tt):
    i = pl.program_id(1)

    @pl.when(i == 0)
    def _():
        carry_ref[...] = jnp.zeros_like(carry_ref)

    z = _dot_nt(hb_ref[...], ws_ref[...].astype(BF16)) + bs_ref[...]
    pos = i * tt + lax.broadcasted_iota(jnp.int32, (tt, 1), 0)
    valid = pos >= pad
    la = _log_sigmoid(jnp.dot(z, wa_ref[...], preferred_element_type=F32, precision=HIGHEST) + ba_ref[...])
    la_ref[...] = jnp.where(valid, la * (1.0 / GLA_TAU), 0.0)
    lf = jnp.where(valid, _log_sigmoid(z), 0.0)

    r = lax.broadcasted_iota(jnp.int32, (ROW_TILE, ROW_TILE), 0)
    c = lax.broadcasted_iota(jnp.int32, (ROW_TILE, ROW_TILE), 1)
    tri = (r >= c).astype(F32)
    carry = carry_ref[...]
    for sb in range(tt // ROW_TILE):
        rows = slice(sb * ROW_TILE, (sb + 1) * ROW_TILE)
        cs = jnp.dot(tri, lf[rows], preferred_element_type=F32, precision=HIGHEST) + carry
        carry = cs[ROW_TILE - 1:ROW_TILE]
        posr = i * tt + sb * ROW_TILE + lax.broadcasted_iota(jnp.int32, (1, ROW_TILE), 1)
        crow_ref[0, :, rows] = jnp.where(posr >= pad, cs.T[0:8] * LOG2E, MASKED_KEY_BIAS)
    carry_ref[...] = carry


def _prep(hb, w_small, b_small, wa_ext, ba, *, B, lp, pad, tt):
    T, D = hb.shape
    KW = wa_ext.shape[1]
    nt = lp // tt
    return pl.pallas_call(
        functools.partial(_prep_kernel, pad=pad, tt=tt),
        grid=(B, nt),
        in_specs=[
            pl.BlockSpec((tt, D), lambda b, i: (b * nt + i, 0)),
            pl.BlockSpec((128, D), lambda b, i: (0, 0)),
            pl.BlockSpec((1, 128), lambda b, i: (0, 0)),
            pl.BlockSpec((128, KW), lambda b, i: (0, 0)),
            pl.BlockSpec((1, KW), lambda b, i: (0, 0)),
        ],
        out_specs=[
            pl.BlockSpec((1, 8, tt), lambda b, i: (b, 0, i)),
            pl.BlockSpec((tt, KW), lambda b, i: (b * nt + i, 0)),
        ],
        out_shape=[
            jax.ShapeDtypeStruct((B, 8, lp), F32),
            jax.ShapeDtypeStruct((T, KW), F32),
        ],
        scratch_shapes=[pltpu.VMEM((1, 128), F32)],
        compiler_params=_params(2),
        name="gate_prep",
    )(hb, w_small, b_small, wa_ext, ba)


def _fox_kernel(q_ref, k_ref, v_ref, crow_ref, o_ref, *, heads, tq, tk, group):
    qi = pl.program_id(1)
    n_full = (qi * tq) // tk
    kd = pl.multiple_of(n_full * tk, tk)
    rows = qi * tq + lax.broadcasted_iota(jnp.int32, (tq, tk), 0)
    cols = kd + lax.broadcasted_iota(jnp.int32, (tq, tk), 1)
    causal = cols <= rows
    ones = jnp.ones((tk, HEAD_DIM), BF16)

    def scores(h, k0):
        sl = slice(h * HEAD_DIM, (h + 1) * HEAD_DIM)
        cr = crow_ref[0, h:h + 1, pl.ds(k0, tk)]
        return _dot_nt(q_ref[:, sl], k_ref[pl.ds(k0, tk), sl]) - cr

    def weighted_values(h, p, k0):
        sl = slice(h * HEAD_DIM, (h + 1) * HEAD_DIM)
        va = jnp.concatenate([v_ref[pl.ds(k0, tk), sl], ones], axis=1)
        pv = jnp.dot(p, va, preferred_element_type=F32)
        return pv[:, :HEAD_DIM], pv[:, HEAD_DIM:HEAD_DIM + 1]

    for h0 in range(0, heads, group):
        hs = range(h0, min(h0 + group, heads))
        carry = []
        for h in hs:
            s = jnp.where(causal, scores(h, kd), -jnp.inf)
            m = jnp.max(s, axis=-1, keepdims=True)
            acc, l = weighted_values(h, jnp.exp2((s - m).astype(BF16)), kd)
            carry += [m, l, acc]

        def body(ki, carry):
            k0 = pl.multiple_of(ki * tk, tk)
            out = []
            for n, h in enumerate(hs):
                m, l, acc = carry[3 * n:3 * n + 3]
                s = scores(h, k0)
                m_new = jnp.maximum(m, jnp.max(s, axis=-1, keepdims=True))
                a = jnp.exp2(m - m_new)
                pv, p_sum = weighted_values(h, jnp.exp2((s - m_new).astype(BF16)), k0)
                out += [m_new, a * l + p_sum, a * acc + pv]
            return tuple(out)

        carry = lax.fori_loop(0, n_full, body, tuple(carry))
        for n, h in enumerate(hs):
            m, l, acc = carry[3 * n:3 * n + 3]
            o_ref[:, h * HEAD_DIM:(h + 1) * HEAD_DIM] = (acc / l).astype(o_ref.dtype)


def _fox(proj, crow, *, B, lp, fw, off_q, off_k, off_v, tq, tk):
    T = proj.shape[0]
    nq = lp // tq
    heads = fw // HEAD_DIM
    assert tk % tq == 0
    return pl.pallas_call(
        functools.partial(_fox_kernel, heads=heads, tq=tq, tk=tk, group=FOX_HEAD_GROUP),
        grid=(B, nq),
        in_specs=[
            pl.BlockSpec((tq, fw), lambda b, i: (b * nq + i, off_q // fw)),
            pl.BlockSpec((lp, fw), lambda b, i: (b, off_k // fw)),
            pl.BlockSpec((lp, fw), lambda b, i: (b, off_v // fw)),
            pl.BlockSpec((1, 8, lp), lambda b, i: (b, 0, 0)),
        ],
        out_specs=pl.BlockSpec((tq, fw), lambda b, i: (b * nq + i, 0)),
        out_shape=jax.ShapeDtypeStruct((T, fw), BF16),
        compiler_params=_params(2),
        name="fox_attention",
    )(proj, proj, proj, crow)


def _lru_kernel(lx_ref, lg_ref, cw_ref, cb_ref, wr_ref, br_ref, wi_ref, bi_ref, lam_ref, o_ref,
                ext_ref, a_ref, u_ref, hc_ref, *, pad, tt, nblk):
    i = pl.program_id(1)

    @pl.when(i == 0)
    def _():
        ext_ref[0:CONV_HALO, :] = jnp.zeros((CONV_HALO, ext_ref.shape[1]), F32)
        hc_ref[...] = jnp.zeros_like(hc_ref)

    @pl.when(i > 0)
    def _():
        ext_ref[0:CONV_HALO, :] = ext_ref[tt:tt + CONV_HALO, :]

    pos = i * tt + lax.broadcasted_iota(jnp.int32, (tt, 1), 0)
    valid = pos >= pad
    ext_ref[CONV_HALO:CONV_HALO + tt, :] = jnp.where(valid, lx_ref[...].astype(F32), 0.0)

    sp = jnp.maximum(-lam_ref[...], 0.0) + jnp.log1p(jnp.exp(-jnp.abs(lam_ref[...])))
    for n in range(nblk):
        sl = slice(n * HEAD_DIM, (n + 1) * HEAD_DIM)
        xc = cb_ref[:, sl]
        for j in range(CONV_W):
            start = CONV_HALO - (CONV_W - 1) + j
            xc = xc + cw_ref[j:j + 1, sl] * ext_ref[start:start + tt, sl]
        xb = xc.astype(BF16)
        r = _sigmoid(jnp.dot(xb, wr_ref[n].astype(BF16), preferred_element_type=F32) + br_ref[:, sl])
        g = _sigmoid(jnp.dot(xb, wi_ref[n].astype(BF16), preferred_element_type=F32) + bi_ref[:, sl])
        log_a = (-LRU_C) * r * sp[:, sl]
        a = jnp.exp(log_a)
        a_ref[:, sl] = a
        u = jnp.sqrt(-jnp.tanh(log_a) * (1.0 + a * a)) * (g * xc)
        u_ref[:, sl] = jnp.where(valid, u, 0.0)

    def body(gidx, h):
        r0 = pl.multiple_of(gidx * 8, 8)
        a8 = a_ref[pl.ds(r0, 8), :]
        u8 = u_ref[pl.ds(r0, 8), :]
        outs = []
        for r in range(8):
            h = a8[r:r + 1] * h + u8[r:r + 1]
            outs.append(h)
        u_ref[pl.ds(r0, 8), :] = jnp.concatenate(outs, axis=0)
        return h

    hc_ref[...] = lax.fori_loop(0, tt // 8, body, hc_ref[...])
    o_ref[...] = (u_ref[...] * _gelu_tanh(lg_ref[...].astype(F32))).astype(o_ref.dtype)


def _lru(proj, conv_w, conv_b, w_r, b_r, w_i, b_i, lam, *, B, lp, lw, off_x, off_g, pad, tt):
    T = proj.shape[0]
    nt = lp // tt
    nblk = lw // HEAD_DIM
    row = lambda b, i: (0, 0)
    return pl.pallas_call(
        functools.partial(_lru_kernel, pad=pad, tt=tt, nblk=nblk),
        grid=(B, nt),
        in_specs=[
            pl.BlockSpec((tt, lw), lambda b, i: (b * nt + i, off_x // lw)),
            pl.BlockSpec((tt, lw), lambda b, i: (b * nt + i, off_g // lw)),
            pl.BlockSpec((CONV_W, lw), row),
            pl.BlockSpec((1, lw), row),
            pl.BlockSpec((nblk, HEAD_DIM, HEAD_DIM), lambda b, i: (0, 0, 0)),
            pl.BlockSpec((1, lw), row),
            pl.BlockSpec((nblk, HEAD_DIM, HEAD_DIM), lambda b, i: (0, 0, 0)),
            pl.BlockSpec((1, lw), row),
            pl.BlockSpec((1, lw), row),
        ],
        out_specs=pl.BlockSpec((tt, lw), lambda b, i: (b * nt + i, 0)),
        out_shape=jax.ShapeDtypeStruct((T, lw), BF16),
        scratch_shapes=[
            pltpu.VMEM((tt + CONV_HALO, lw), F32),
            pltpu.VMEM((tt, lw), F32),
            pltpu.VMEM((tt, lw), F32),
            pltpu.VMEM((1, lw), F32),
        ],
        compiler_params=_params(2),
        name="conv_rglru",
    )(proj, proj, conv_w, conv_b, w_r, b_r, w_i, b_i, lam)


def _gla_kernel(q_ref, k_ref, v_ref, gg_ref, la_ref, ng_ref, o_ref, st_ref, *, pad, tt, heads):
    i = pl.program_id(1)

    @pl.when(i == 0)
    def _():
        st_ref[...] = jnp.zeros_like(st_ref)

    r = lax.broadcasted_iota(jnp.int32, (GLA_CHUNK, GLA_CHUNK), 0)
    c = lax.broadcasted_iota(jnp.int32, (GLA_CHUNK, GLA_CHUNK), 1)
    tri = (r >= c).astype(F32)
    for ci in range(tt // GLA_CHUNK):
        rows = slice(ci * GLA_CHUNK, (ci + 1) * GLA_CHUNK)
        pos = i * tt + ci * GLA_CHUNK + lax.broadcasted_iota(jnp.int32, (GLA_CHUNK, 1), 0)
        valid = pos >= pad
        for hd in range(heads):
            ks = slice(hd * HEAD_DIM, (hd + 1) * HEAD_DIM)
            vs = slice(hd * GLA_DV, (hd + 1) * GLA_DV)
            cs = jnp.dot(tri, la_ref[rows, ks], preferred_element_type=F32, precision=HIGHEST)
            cl = cs[GLA_CHUNK - 1:GLA_CHUNK]
            kdec = jnp.where(valid, k_ref[rows, ks].astype(F32) * jnp.exp(cl - cs), 0.0).astype(BF16)
            ut = lax.dot_general(v_ref[rows, vs], kdec, (((0,), (0,)), ((), ())), preferred_element_type=F32)
            st = st_ref[hd] * jnp.exp(cl) + ut
            st_ref[hd] = st
            o = _dot_nt(q_ref[rows, ks], st.astype(BF16))
            o = o * lax.rsqrt(jnp.mean(o * o, axis=-1, keepdims=True) + RMS_EPS) * ng_ref[:, vs]
            gg = gg_ref[rows, vs].astype(F32)
            o_ref[rows, vs] = (o * (gg * _sigmoid(gg))).astype(o_ref.dtype)


def _gla(proj, proj_g, la, norm_g, *, B, lp, kw, vw, off_q, off_k, off_v, off_g, pad, tt):
    T = proj.shape[0]
    nt = lp // tt
    heads = kw // HEAD_DIM
    return pl.pallas_call(
        functools.partial(_gla_kernel, pad=pad, tt=tt, heads=heads),
        grid=(B, nt),
        in_specs=[
            pl.BlockSpec((tt, kw), lambda b, i: (b * nt + i, off_q // kw)),
            pl.BlockSpec((tt, kw), lambda b, i: (b * nt + i, off_k // kw)),
            pl.BlockSpec((tt, vw), lambda b, i: (b * nt + i, off_v // vw)),
            pl.BlockSpec((tt, vw), lambda b, i: (b * nt + i, off_g // vw)),
            pl.BlockSpec((tt, kw), lambda b, i: (b * nt + i, 0)),
            pl.BlockSpec((1, vw), lambda b, i: (0, 0)),
        ],
        out_specs=pl.BlockSpec((tt, vw), lambda b, i: (b * nt + i, 0)),
        out_shape=jax.ShapeDtypeStruct((T, vw), BF16),
        scratch_shapes=[pltpu.VMEM((heads, GLA_DV, HEAD_DIM), F32)],
        compiler_params=_params(2),
        name="gla_chunked",
    )(proj, proj, proj, proj_g, la, norm_g)


def _merge_kernel(of_ref, ol_ref, og_ref, w_ref, g0_ref, g1_ref, g2_ref, o_ref, wbf_ref, *, fw, lw):
    @pl.when(pl.program_id(1) == 0)
    def _():
        wbf_ref[...] = w_ref[...].astype(BF16)

    y0 = jnp.dot(of_ref[...], wbf_ref[0:fw, :], preferred_element_type=F32)
    y1 = jnp.dot(ol_ref[...], wbf_ref[fw:fw + lw, :], preferred_element_type=F32)
    y2 = jnp.dot(og_ref[...], wbf_ref[fw + lw:, :], preferred_element_type=F32)
    out = (_sigmoid(g0_ref[...].astype(F32)) * y0 + _sigmoid(g1_ref[...].astype(F32)) * y1
           + _sigmoid(g2_ref[...].astype(F32)) * y2)
    o_ref[...] = out.astype(o_ref.dtype)


def _merge(o_fox, o_lru, o_gla, w_branch, layer, proj, *, off_gates, d):
    T, fw = o_fox.shape
    lw = o_lru.shape[1]
    vw = o_gla.shape[1]
    tm = _pick(T, (1056, 1024, 768, 512, 384, 256, 128))
    tn = _pick(math.gcd(d, off_gates), (512, 256, 128))
    gate_spec = lambda b: pl.BlockSpec((tm, tn), lambda j, i: (i, (off_gates + b * d) // tn + j))
    return pl.pallas_call(
        functools.partial(_merge_kernel, fw=fw, lw=lw),
        grid=(d // tn, T // tm),
        in_specs=[
            pl.BlockSpec((tm, fw), lambda j, i: (i, 0)),
            pl.BlockSpec((tm, lw), lambda j, i: (i, 0)),
            pl.BlockSpec((tm, vw), lambda j, i: (i, 0)),
            pl.BlockSpec((None, fw + lw + vw, tn), lambda j, i: (layer, 0, j)),
            gate_spec(0), gate_spec(1), gate_spec(2),
        ],
        out_specs=pl.BlockSpec((tm, tn), lambda j, i: (i, j)),
        out_shape=jax.ShapeDtypeStruct((T, d), BF16),
        scratch_shapes=[pltpu.VMEM((fw + lw + vw, tn), BF16)],
        compiler_params=_params(2),
        name="branch_merge",
    )(o_fox, o_lru, o_gla, w_branch, proj, proj, proj)


def _ln_router_kernel(h_ref, mix_ref, g_ref, b_ref, wr_ref, br_ref, hf_ref, hx_ref, idx_ref, wt_ref, *, alpha):
    y = _layer_norm(alpha * h_ref[...] + mix_ref[...], g_ref[...], b_ref[...])
    hf_ref[...] = y
    hx_ref[...] = _pack_halves(y)
    logits = jnp.dot(y, wr_ref[...], preferred_element_type=F32, precision=HIGHEST) + br_ref[...]
    lane = lax.broadcasted_iota(jnp.int32, logits.shape, 1).astype(F32)
    idx = jnp.zeros_like(logits)
    vals = []
    for k in range(TOP_K):
        mx = jnp.max(logits, axis=-1, keepdims=True)
        sel = jnp.min(jnp.where(logits == mx, lane, float(logits.shape[1])), axis=-1, keepdims=True)
        vals.append(mx)
        idx = jnp.where(lane == float(k), sel, idx)
        logits = jnp.where(lane == sel, -jnp.inf, logits)
    es = [jnp.exp(v - vals[0]) for v in vals]
    tot = es[0]
    for e in es[1:]:
        tot = tot + e
    wt = jnp.zeros_like(logits)
    for k in range(TOP_K):
        wt = jnp.where(lane == float(k), es[k] / tot, wt)
    idx_ref[...] = idx.astype(jnp.int32)
    wt_ref[...] = wt


def _ln_router(h, mix, g, b, wr_pad, br_pad, *, alpha):
    T, D = h.shape
    tm = _pick(T, (192, 128))
    blk = pl.BlockSpec((tm, D), lambda i: (i, 0))
    row = pl.BlockSpec((1, D), lambda i: (0, 0))
    small = pl.BlockSpec((tm, 128), lambda i: (i, 0))
    return pl.pallas_call(
        functools.partial(_ln_router_kernel, alpha=alpha),
        grid=(T // tm,),
        in_specs=[blk, blk, row, row, pl.BlockSpec((D, 128), lambda i: (0, 0)), pl.BlockSpec((1, 128), lambda i: (0, 0))],
        out_specs=[blk, pl.BlockSpec((tm, D // 2), lambda i: (i, 0)), small, small],
        out_shape=[
            jax.ShapeDtypeStruct((T, D), F32),
            jax.ShapeDtypeStruct((T, D // 2), U32),
            jax.ShapeDtypeStruct((T, 128), jnp.int32),
            jax.ShapeDtypeStruct((T, 128), F32),
        ],
        compiler_params=_params(1),
        name="ln_router",
    )(h, mix, g, b, wr_pad, br_pad)


def _pack_halves(y):
    half = y.shape[1] // 2
    lo = pltpu.bitcast(y[:, :half].astype(BF16).astype(F32), U32)
    hi = pltpu.bitcast(y[:, half:].astype(BF16).astype(F32), U32)
    return lax.shift_right_logical(lo, jnp.uint32(16)) | (hi & jnp.uint32(0xFFFF0000))


def _unpack_halves(w):
    lo = pltpu.bitcast(lax.shift_left(w, jnp.uint32(16)), F32)
    hi = pltpu.bitcast(w & jnp.uint32(0xFFFF0000), F32)
    return lo, hi


def _row_gather(src_hbm, idx_ref, base, buf_ref, sem, n, static=False):
    def start(r):
        pltpu.make_async_copy(src_hbm.at[pl.ds(idx_ref[base + r], 1)], buf_ref.at[pl.ds(r, 1)], sem).start()

    if static:
        for r in range(n):
            start(r)
    else:
        def body(r, carry):
            start(r)
            return carry

        lax.fori_loop(0, n, body, 0, unroll=8)


def _tile_wait(src_hbm, buf_ref, sem):
    pltpu.make_async_copy(src_hbm.at[pl.ds(0, buf_ref.shape[0])], buf_ref, sem).wait()


def _moe_up_kernel(te_ref, na_ref, first_ref, nxt_ref, tok_ref, h_hbm, wup_hbm, bup_ref, sel_ref, act_ref,
                   xa_ref, xb_ref, sem_ref, stage_ref, wbf_ref, wsem_ref, *, tmx, layer):
    i = pl.program_id(0)
    na = na_ref[0]
    odd = lax.rem(i, 2)

    def weight_copy(e):
        return pltpu.make_async_copy(wup_hbm.at[layer, e], stage_ref, wsem_ref.at[0])

    @pl.when(jnp.logical_and(i == 0, na > 0))
    def _():
        _row_gather(h_hbm, tok_ref, 0, xa_ref, sem_ref.at[0], tmx)
        weight_copy(te_ref[0]).start()

    @pl.when(jnp.logical_and(i < na, first_ref[i] == 1))
    def _():
        weight_copy(te_ref[i]).wait()
        _cast_rows(stage_ref, wbf_ref, 256)

        @pl.when(nxt_ref[i] >= 0)
        def _():
            weight_copy(nxt_ref[i]).start()

    def tile(cur_ref, cur_sem, nxt_ref_, nxt_sem):
        _tile_wait(h_hbm, cur_ref, cur_sem)
        _row_gather(h_hbm, tok_ref, jnp.minimum(i + 1, na - 1) * tmx, nxt_ref_, nxt_sem, tmx, static=True)
        lo, hi = _unpack_halves(cur_ref[...])
        x = jnp.concatenate([lo.astype(BF16), hi.astype(BF16)], axis=1)
        h = jnp.dot(x, wbf_ref[...], preferred_element_type=F32) + bup_ref[...]
        g = jnp.minimum(h, SWIGLU_LIMIT)
        u = jnp.clip(pltpu.roll(h, h.shape[1] - 1, axis=1), -SWIGLU_LIMIT, SWIGLU_LIMIT)
        act = ((u + 1.0) * g * _sigmoid(SWIGLU_ALPHA * g)).astype(BF16)
        act_ref[...] = jnp.dot(act, sel_ref[...], preferred_element_type=F32).astype(BF16)

        @pl.when(i + 1 == na)
        def _():
            _tile_wait(h_hbm, nxt_ref_, nxt_sem)

    @pl.when(jnp.logical_and(i < na, odd == 0))
    def _():
        tile(xa_ref, sem_ref.at[0], xb_ref, sem_ref.at[1])

    @pl.when(jnp.logical_and(i < na, odd == 1))
    def _():
        tile(xb_ref, sem_ref.at[1], xa_ref, sem_ref.at[0])

    @pl.when(i >= na)
    def _():
        act_ref[...] = jnp.zeros_like(act_ref)


def _moe_down_kernel(te_ref, na_ref, first_ref, act_ref, wd_ref, bd_ref, y_ref, wbf_ref):
    i = pl.program_id(0)
    na = na_ref[0]

    @pl.when(jnp.logical_and(i < na, first_ref[i] == 1))
    def _():
        _cast_rows(wd_ref, wbf_ref, 128)

    @pl.when(i < na)
    def _():
        y = jnp.dot(act_ref[...], wbf_ref[...], preferred_element_type=F32) + bd_ref[...]
        y_ref[...] = _pack_halves(y)

    @pl.when(i >= na)
    def _():
        y_ref[...] = jnp.zeros_like(y_ref)


def _moe(tile_expert, n_active, row_token, hx, w_up, b_up, sel, w_down, b_down, *, tmx, layer):
    T, D = hx.shape[0], 2 * hx.shape[1]
    _, E, _, F2 = w_up.shape
    F = F2 // 2
    P = row_token.shape[0]
    ntiles = P // tmx
    tiles = jnp.arange(ntiles, dtype=jnp.int32)
    first = jnp.where(tiles == 0, 1, (tile_expert != jnp.roll(tile_expert, 1)).astype(jnp.int32))
    j = jnp.searchsorted(tile_expert, tile_expert, side="right").astype(jnp.int32)
    nxt = jnp.where(j < ntiles, tile_expert[jnp.minimum(j, ntiles - 1)], -1).astype(jnp.int32)

    act = pl.pallas_call(
        functools.partial(_moe_up_kernel, tmx=tmx, layer=layer),
        grid_spec=pltpu.PrefetchScalarGridSpec(
            num_scalar_prefetch=5,
            grid=(ntiles,),
            in_specs=[
                pl.BlockSpec(memory_space=pl.ANY),
                pl.BlockSpec(memory_space=pl.ANY),
                pl.BlockSpec((None, 1, F2), lambda i, te, *_: (te[i], 0, 0)),
                pl.BlockSpec((F2, F), lambda i, *_: (0, 0)),
            ],
            out_specs=pl.BlockSpec((tmx, F), lambda i, *_: (i, 0)),
            scratch_shapes=[
                pltpu.VMEM((tmx, D // 2), U32), pltpu.VMEM((tmx, D // 2), U32), pltpu.SemaphoreType.DMA((2,)),
                pltpu.VMEM((D, F2), F32), pltpu.VMEM((D, F2), BF16), pltpu.SemaphoreType.DMA((1,)),
            ],
        ),
        out_shape=jax.ShapeDtypeStruct((P, F), BF16),
        compiler_params=_params(1),
        name="moe_up",
    )(tile_expert, n_active, first, nxt, row_token, hx, w_up, b_up, sel)

    return pl.pallas_call(
        _moe_down_kernel,
        grid_spec=pltpu.PrefetchScalarGridSpec(
            num_scalar_prefetch=3,
            grid=(ntiles,),
            in_specs=[
                pl.BlockSpec((tmx, F), lambda i, *_: (i, 0)),
                pl.BlockSpec((None, None, F, D), lambda i, te, *_: (layer, te[i], 0, 0)),
                pl.BlockSpec((None, 1, D), lambda i, te, *_: (te[i], 0, 0)),
            ],
            out_specs=pl.BlockSpec((tmx, D // 2), lambda i, *_: (i, 0)),
            scratch_shapes=[pltpu.VMEM((F, D), BF16)],
        ),
        out_shape=jax.ShapeDtypeStruct((P, D // 2), U32),
        compiler_params=_params(1),
        name="moe_down",
    )(tile_expert, n_active, first, act, w_down, b_down)


def _combine_kernel(pos_ref, y_hbm, h_ref, w_ref, g_ref, b_ref, *rest, alpha, tc, nt, final):
    if final:
        out_ref, bufa_ref, bufb_ref, sem_ref = rest
    else:
        hf_ref, hb_ref, bufa_ref, bufb_ref, sem_ref = rest
    i = pl.program_id(0)
    odd = lax.rem(i, 2)

    def start(tile, buf_ref, sem, static):
        for k in range(TOP_K):
            _row_gather(y_hbm, pos_ref, (k * nt + tile) * tc, buf_ref.at[k], sem, tc, static=static)

    @pl.when(i == 0)
    def _():
        start(0, bufa_ref, sem_ref.at[0], False)

    def tile(cur_ref, cur_sem, nxt_ref, nxt_sem):
        for k in range(TOP_K):
            _tile_wait(y_hbm, cur_ref.at[k], cur_sem)
        start(jnp.minimum(i + 1, nt - 1), nxt_ref, nxt_sem, True)
        lo = hi = None
        for k in range(TOP_K):
            l, h = _unpack_halves(cur_ref[k])
            wk = w_ref[:, k:k + 1]
            lo = wk * l if lo is None else lo + wk * l
            hi = wk * h if hi is None else hi + wk * h
        ffn = jnp.concatenate([lo, hi], axis=1)
        y = _layer_norm(alpha * h_ref[...] + ffn, g_ref[...], b_ref[...])
        if final:
            out_ref[...] = y
        else:
            hf_ref[...] = y
            hb_ref[...] = y.astype(BF16)

        @pl.when(i + 1 == nt)
        def _():
            for k in range(TOP_K):
                _tile_wait(y_hbm, nxt_ref.at[k], nxt_sem)

    @pl.when(odd == 0)
    def _():
        tile(bufa_ref, sem_ref.at[0], bufb_ref, sem_ref.at[1])

    @pl.when(odd == 1)
    def _():
        tile(bufb_ref, sem_ref.at[1], bufa_ref, sem_ref.at[0])


def _combine(pos, ys, hf, top_w, g, b, *, alpha, lp, final):
    T, D = hf.shape
    tc = ROW_TILE
    nt = T // tc
    ntb = lp // tc
    if final:
        out_specs = pl.BlockSpec((tc, D), lambda i, pos: ((i // ntb) * (ntb - 1) + jnp.maximum(i % ntb - 1, 0), 0))
        out_shape = jax.ShapeDtypeStruct(((T // lp) * (lp - tc), D), F32)
    else:
        out_specs = [pl.BlockSpec((tc, D), lambda i, pos: (i, 0)), pl.BlockSpec((tc, D), lambda i, pos: (i, 0))]
        out_shape = [jax.ShapeDtypeStruct((T, D), F32), jax.ShapeDtypeStruct((T, D), BF16)]
    grid_spec = pltpu.PrefetchScalarGridSpec(
        num_scalar_prefetch=1,
        grid=(nt,),
        in_specs=[
            pl.BlockSpec(memory_space=pl.ANY),
            pl.BlockSpec((tc, D), lambda i, pos: (i, 0)),
            pl.BlockSpec((tc, 128), lambda i, pos: (i, 0)),
            pl.BlockSpec((1, D), lambda i, pos: (0, 0)),
            pl.BlockSpec((1, D), lambda i, pos: (0, 0)),
        ],
        out_specs=out_specs,
        scratch_shapes=[pltpu.VMEM((TOP_K, tc, D // 2), U32), pltpu.VMEM((TOP_K, tc, D // 2), U32),
                        pltpu.SemaphoreType.DMA((2,))],
    )
    return pl.pallas_call(
        functools.partial(_combine_kernel, alpha=alpha, tc=tc, nt=nt, final=final),
        grid_spec=grid_spec,
        out_shape=out_shape,
        compiler_params=_params(1),
        name="moe_combine_ln",
    )(pos, ys, hf, top_w, g, b)


def _route_kernel(idx_ref, pos_ref, te_ref, na_ref, cnt_ref, pst_ref, *, pad, tr, tmx, n_experts, spare_row):
    p = pl.program_id(0)
    i = pl.program_id(2)
    first = jnp.logical_and(pl.program_id(1) == 0, i == 0)

    @pl.when(jnp.logical_and(p == 0, first))
    def _():
        cnt_ref[...] = jnp.zeros_like(cnt_ref)

    @pl.when(jnp.logical_and(p == 1, first))
    def _():
        cnt = cnt_ref[...]
        padded = jnp.floor((cnt + (tmx - 1.0)) * (1.0 / tmx)) * tmx
        r = lax.broadcasted_iota(jnp.int32, (128, 128), 0)
        c = lax.broadcasted_iota(jnp.int32, (128, 128), 1)
        pst = jnp.dot(padded, (r < c).astype(F32), preferred_element_type=F32, precision=HIGHEST)
        pst_ref[...] = pst
        cnt_ref[...] = jnp.zeros_like(cnt_ref)
        tile_end = (pst + padded) * (1.0 / tmx)
        nt_pad = te_ref.shape[0]
        t = lax.broadcasted_iota(jnp.int32, (nt_pad, 128), 0).astype(F32)
        lane = lax.broadcasted_iota(jnp.int32, (nt_pad, 128), 1)
        is_expert = lane < n_experts
        te = jnp.sum(jnp.where(jnp.logical_and(tile_end <= t, is_expert), 1.0, 0.0), axis=-1, keepdims=True)
        lane1 = lax.broadcasted_iota(jnp.int32, (1, 128), 1).astype(F32)
        e_last = jnp.max(jnp.where(cnt > 0.0, lane1, 0.0), axis=-1, keepdims=True)
        te_ref[...] = jnp.broadcast_to(jnp.minimum(te, e_last), (nt_pad, 128)).astype(jnp.int32)
        na_ref[...] = jnp.broadcast_to(jnp.max(tile_end, axis=-1, keepdims=True), (1, 128)).astype(jnp.int32)

    idx = idx_ref[...]
    lane = lax.broadcasted_iota(jnp.int32, (tr, 128), 1)
    valid = (i * tr + lax.broadcasted_iota(jnp.int32, (tr, 1), 0)) >= pad
    onehot = jnp.zeros((tr, 128), F32)
    for k in range(TOP_K):
        onehot = onehot + jnp.where(lane == idx[:, k:k + 1], 1.0, 0.0)
    onehot = jnp.where(valid, onehot, 0.0)
    r = lax.broadcasted_iota(jnp.int32, (tr, tr), 0)
    c = lax.broadcasted_iota(jnp.int32, (tr, tr), 1)
    rank = jnp.dot((r > c).astype(BF16), onehot.astype(BF16), preferred_element_type=F32) + cnt_ref[...]
    cnt_ref[...] = cnt_ref[...] + jnp.sum(onehot, axis=0, keepdims=True)

    @pl.when(p == 0)
    def _():
        pos_ref[...] = jnp.zeros_like(pos_ref)

    @pl.when(p == 1)
    def _():
        dest = rank + pst_ref[...]
        out = jnp.zeros((tr, 128), F32)
        for k in range(TOP_K):
            d = jnp.sum(jnp.where(lane == idx[:, k:k + 1], dest, 0.0), axis=-1, keepdims=True)
            out = jnp.where(lane == k, jnp.where(valid, d, float(spare_row)), out)
        pos_ref[...] = out.astype(jnp.int32)


def _route(top_idx, *, B, lp, pad, tmx, n_experts, n_rows):
    T = top_idx.shape[0]
    tr = _pick(lp, (384, 256, 128))
    nt = lp // tr
    nt_pad = -(-(n_rows // tmx) // 8) * 8
    pos, te, na = pl.pallas_call(
        functools.partial(_route_kernel, pad=pad, tr=tr, tmx=tmx, n_experts=n_experts, spare_row=n_rows - 1),
        grid=(2, B, nt),
        in_specs=[pl.BlockSpec((tr, 128), lambda p, b, i: (b * nt + i, 0))],
        out_specs=[
            pl.BlockSpec((tr, 128), lambda p, b, i: (p * (b * nt + i), 0)),
            pl.BlockSpec((nt_pad, 128), lambda p, b, i: (0, 0)),
            pl.BlockSpec((1, 128), lambda p, b, i: (0, 0)),
        ],
        out_shape=[
            jax.ShapeDtypeStruct((T, 128), jnp.int32),
            jax.ShapeDtypeStruct((nt_pad, 128), jnp.int32),
            jax.ShapeDtypeStruct((1, 128), jnp.int32),
        ],
        scratch_shapes=[pltpu.VMEM((1, 128), F32), pltpu.VMEM((1, 128), F32)],
        compiler_params=_params(3),
        name="route_rank",
    )(top_idx)
    pos_flat = pos[:, :TOP_K].T.reshape(-1)
    return pos_flat, te[:n_rows // tmx, 0], na[0, :1]


def _row_token_kernel(pos_ref, rt_ref, *, t):
    def zero(r, carry):
        rt_ref[r] = 0
        return carry

    lax.fori_loop(0, rt_ref.shape[0], zero, 0, unroll=8)
    for k in range(TOP_K):
        def body(tok, carry):
            rt_ref[pos_ref[k * t + tok]] = tok
            return carry

        lax.fori_loop(0, t, body, 0, unroll=8)


def _row_token(pos_flat, *, t, n_rows):
    return pl.pallas_call(
        functools.partial(_row_token_kernel, t=t),
        grid_spec=pltpu.PrefetchScalarGridSpec(
            num_scalar_prefetch=1, grid=(1,), in_specs=[],
            out_specs=pl.BlockSpec(memory_space=pltpu.SMEM)),
        out_shape=jax.ShapeDtypeStruct((n_rows,), jnp.int32),
        compiler_params=_params(1),
        name="route_row_token",
    )(pos_flat)


def _segments(fw, fh, lw, kw, vw, rank, d):
    names = ("fq", "fk", "fv", "ff", "lx", "lg", "gq", "gk", "gv", "ga", "gg", "gate0", "gate1", "gate2")
    widths = (fw, fw, fw, fh, lw, lw, kw, kw, vw, rank, vw, d, d, d)
    segs, off = {}, 0
    for n, w in zip(names, widths):
        segs[n] = (off, w)
        off += w
    return segs, off


def kernel(x, meta_tokens, emb_ln_g, emb_ln_b, w_in, b_in, conv_w, conv_b, lru_w_r, lru_b_r, lru_w_i, lru_b_i,
           lru_lambda, gla_w_alpha, gla_b_alpha, gla_norm_g, w_branch, w_out, b_out, ln1_g, ln1_b, w_router,
           b_router, w_up, b_up, w_down, b_down, ln2_g, ln2_b):
    B, S, D = x.shape
    n_meta = meta_tokens.shape[0]
    depth = w_in.shape[0]
    L = S + n_meta
    pad = (-L) % ROW_TILE
    lp = L + pad
    assert pad + n_meta == ROW_TILE and S % ROW_TILE == 0
    T = B * lp

    lw = conv_w.shape[2]
    rank, kw = gla_w_alpha.shape[1:]
    vw = gla_norm_g.shape[1]
    n_experts = w_router.shape[2]
    fexp = w_up.shape[3] // 2
    in_cols = w_in.shape[2]
    fh = (in_cols - 2 * lw - 2 * kw - 2 * vw - rank - 3 * D) // (3 * HEAD_DIM + 1)
    fw = fh * HEAD_DIM
    segs, total = _segments(fw, fh, lw, kw, vw, rank, D)
    assert total == in_cols and fh <= 8 and fh + rank <= 128
    alpha = (2.0 * depth) ** 0.25

    windows = (("fq", "fk", "fv"), ("lx", "lg", "gq", "gk", "gv"), ("gg", "gate0", "gate1", "gate2"))
    qscale = HEAD_DIM ** -0.5
    off, win_start, win_scale = {}, [], []
    for names in windows:
        start = segs[names[0]][0]
        for n in names:
            off[n] = segs[n][0] - start
            assert off[n] % segs[n][1] == 0 or n.startswith("gate")
        win_start.append(start)
        scales = {"fq": qscale * LOG2E, "gq": qscale}
        win_scale.append(jnp.concatenate(
            [jnp.full((segs[n][1],), scales.get(n, 1.0), F32) for n in names]).reshape(1, -1))

    def cols(a, n):
        o, w = segs[n]
        return a[..., o:o + w]

    tt = _pick(lp, (384, 256, 128))
    tmx = MOE_ROW_TILE if T >= 4096 else 128
    n_rows = (-(-(B * L * TOP_K + n_experts * (tmx - 1)) // tmx) + 1) * tmx

    head = jnp.concatenate([jnp.zeros((pad, D), F32), meta_tokens.astype(F32)], axis=0)
    hf, hb = _embed_ln(x, head, emb_ln_g.reshape(1, D), emb_ln_b.reshape(1, D), pad=pad, lp=lp)
    sel = (jnp.arange(2 * fexp)[:, None] == 2 * jnp.arange(fexp)[None, :]).astype(BF16)

    w_in_t = jnp.swapaxes(w_in, 1, 2)

    def rows(l, n):
        o, w = segs[n]
        return w_in_t[l, o:o + w]

    for l in range(depth):
        w_small = jnp.concatenate(
            [rows(l, "ff"), jnp.zeros((8 - fh, D), F32), rows(l, "ga"), jnp.zeros((128 - 8 - rank, D), F32)], axis=0)
        b_small = jnp.concatenate(
            [cols(b_in[l], "ff"), jnp.zeros((8 - fh,), F32), cols(b_in[l], "ga"),
             jnp.zeros((128 - 8 - rank,), F32)]).reshape(1, 128)
        wa_ext = jnp.zeros((128, kw), F32).at[8:8 + rank].set(gla_w_alpha[l])

        pa, pb, pc = [
            _in_proj(hb, w_in_t, l, b_in[l][s:s + sc.shape[1]].reshape(1, -1), sc, start=s, name=f"in_proj_{k}")
            for k, (s, sc) in enumerate(zip(win_start, win_scale))]
        crow, la = _prep(hb, w_small, b_small, wa_ext, gla_b_alpha[l].reshape(1, kw), B=B, lp=lp, pad=pad, tt=tt)
        o_fox = _fox(pa, crow, B=B, lp=lp, fw=fw, off_q=off["fq"], off_k=off["fk"], off_v=off["fv"], tq=FOX_Q_TILE,
                     tk=tt)
        o_lru = _lru(pb, conv_w[l], conv_b[l].reshape(1, lw), lru_w_r[l], lru_b_r[l].reshape(1, lw), lru_w_i[l],
                     lru_b_i[l].reshape(1, lw), lru_lambda[l].reshape(1, lw), B=B, lp=lp, lw=lw, off_x=off["lx"],
                     off_g=off["lg"], pad=pad, tt=tt)
        o_gla = _gla(pb, pc, la, gla_norm_g[l].reshape(1, vw), B=B, lp=lp, kw=kw, vw=vw, off_q=off["gq"],
                     off_k=off["gk"], off_v=off["gv"], off_g=off["gg"], pad=pad, tt=tt)
        merged = _merge(o_fox, o_lru, o_gla, w_branch, l, pc, off_gates=off["gate0"], d=D)
        mix = _matmul(merged, w_out, l, b_out[l].reshape(1, D), jnp.ones((1, D), F32), F32, "out_proj")

        wr_pad = jnp.zeros((D, 128), F32).at[:, :n_experts].set(w_router[l])
        br_pad = jnp.full((1, 128), -MASKED_KEY_BIAS, F32).at[0, :n_experts].set(b_router[l])
        hf, hx, top_idx, top_w = _ln_router(hf, mix, ln1_g[l].reshape(1, D), ln1_b[l].reshape(1, D), wr_pad, br_pad,
                                            alpha=alpha)
        pos, tile_expert, n_active = _route(top_idx, B=B, lp=lp, pad=pad, tmx=tmx, n_experts=n_experts, n_rows=n_rows)
        row_token = _row_token(pos, t=T, n_rows=n_rows)
        ys = _moe(tile_expert, n_active, row_token, hx, w_up, b_up[l].reshape(n_experts, 1, 2 * fexp), sel, w_down,
                  b_down[l].reshape(n_experts, 1, D), tmx=tmx, layer=l)
        final = l == depth - 1
        res = _combine(pos, ys, hf, top_w, ln2_g[l].reshape(1, D), ln2_b[l].reshape(1, D), alpha=alpha, lp=lp,
                       final=final)
        if final:
            return res.reshape(B, S, D)
        hf, hb = res
```

```python
import functools
import math

import jax
import jax.numpy as jnp
from jax import lax
from jax.experimental import pallas as pl
from jax.experimental.pallas import tpu as pltpu

F32 = jnp.float32
BF16 = jnp.bfloat16
U32 = jnp.uint32
HIGHEST = lax.Precision.HIGHEST

ROW_TILE = 128
HEAD_DIM = 128
GLA_DV = 256
GLA_CHUNK = 64
CONV_W = 4
CONV_HALO = 8
LN_EPS = 1e-5
RMS_EPS = 1e-6
LRU_C = 8.0
GLA_TAU = 16.0
SWIGLU_LIMIT = 7.0
SWIGLU_ALPHA = 1.702
TOP_K = 4
LOG2E = 1.4426950408889634
MASKED_KEY_BIAS = 1e30
MOE_ROW_TILE = 256
FOX_HEAD_GROUP = 4
FOX_Q_TILE = 384
GATHER_RING = 3
VMEM_LIMIT = 56 * 1024 * 1024


def _pick(n, candidates):
    for c in candidates:
        if n % c == 0:
            return c
    raise ValueError(f"no tile in {candidates} divides {n}")


def _params(n_axes, vmem=VMEM_LIMIT):
    return pltpu.CompilerParams(dimension_semantics=("arbitrary",) * n_axes, vmem_limit_bytes=vmem)


def _layer_norm(v, g, b):
    mu = jnp.mean(v, axis=-1, keepdims=True)
    d = v - mu
    var = jnp.mean(d * d, axis=-1, keepdims=True)
    return d * lax.rsqrt(var + LN_EPS) * g + b


def _log_sigmoid(x):
    return jnp.minimum(x, 0.0) - jnp.log1p(jnp.exp(-jnp.abs(x)))


def _sigmoid(x):
    return 1.0 / (1.0 + jnp.exp(-x))


def _gelu_tanh(x):
    return 0.5 * x * (1.0 + jnp.tanh(0.7978845608028654 * (x + 0.044715 * (x * x * x))))


def _embed_ln_kernel(x_ref, head_ref, g_ref, b_ref, hf_ref, hb_ref, *, pad):
    i = pl.program_id(1)

    @pl.when(i == 0)
    def _():
        y = _layer_norm(head_ref[...], g_ref[...], b_ref[...])
        rows = lax.broadcasted_iota(jnp.int32, (ROW_TILE, 1), 0)
        y = jnp.where(rows >= pad, y, 0.0)
        hf_ref[...] = y
        hb_ref[...] = y.astype(BF16)

    @pl.when(i > 0)
    def _():
        y = _layer_norm(x_ref[0], g_ref[...], b_ref[...])
        hf_ref[...] = y
        hb_ref[...] = y.astype(BF16)


def _embed_ln(x, head, g, b, *, pad, lp):
    B, S, D = x.shape
    nt = lp // ROW_TILE
    T = B * lp
    return pl.pallas_call(
        functools.partial(_embed_ln_kernel, pad=pad),
        grid=(B, nt),
        in_specs=[
            pl.BlockSpec((1, ROW_TILE, D), lambda b, i: (b, jnp.maximum(i - 1, 0), 0)),
            pl.BlockSpec((ROW_TILE, D), lambda b, i: (0, 0)),
            pl.BlockSpec((1, D), lambda b, i: (0, 0)),
            pl.BlockSpec((1, D), lambda b, i: (0, 0)),
        ],
        out_specs=[
            pl.BlockSpec((ROW_TILE, D), lambda b, i: (b * nt + i, 0)),
            pl.BlockSpec((ROW_TILE, D), lambda b, i: (b * nt + i, 0)),
        ],
        out_shape=[jax.ShapeDtypeStruct((T, D), F32), jax.ShapeDtypeStruct((T, D), BF16)],
        compiler_params=_params(2),
        name="embed_ln",
    )(x, head, g, b)


def _dot_nt(a, b):
    return lax.dot_general(a, b, (((1,), (1,)), ((), ())), preferred_element_type=F32)


def _cast_rows(src_ref, dst_ref, chunk):
    def body(c, carry):
        r0 = pl.multiple_of(c * chunk, chunk)
        dst_ref[pl.ds(r0, chunk), :] = src_ref[pl.ds(r0, chunk), :].astype(BF16)
        return carry

    lax.fori_loop(0, src_ref.shape[0] // chunk, body, 0)


def _mm_kernel(a_ref, w_ref, b_ref, s_ref, o_ref, *scratch):
    if scratch:
        (wbf_ref,) = scratch

        @pl.when(pl.program_id(1) == 0)
        def _():
            wbf_ref[...] = w_ref[...].astype(BF16)

        w = wbf_ref[...]
    else:
        w = w_ref[...]
    acc = jnp.dot(a_ref[...], w, preferred_element_type=F32)
    o_ref[...] = ((acc + b_ref[...]) * s_ref[...]).astype(o_ref.dtype)


def _matmul(a, w, layer, bias, scale, out_dtype, name):
    M, K = a.shape
    N = w.shape[2]
    tm = _pick(M, (1056, 1024, 768, 512, 384, 256, 128))
    tn = _pick(N, (512, 256, 128))
    scratch = [pltpu.VMEM((K, tn), BF16)] if w.dtype != BF16 else []
    return pl.pallas_call(
        _mm_kernel,
        grid=(N // tn, M // tm),
        in_specs=[
            pl.BlockSpec((tm, K), lambda j, i: (i, 0)),
            pl.BlockSpec((None, K, tn), lambda j, i: (layer, 0, j)),
            pl.BlockSpec((1, tn), lambda j, i: (0, j)),
            pl.BlockSpec((1, tn), lambda j, i: (0, j)),
        ],
        out_specs=pl.BlockSpec((tm, tn), lambda j, i: (i, j)),
        out_shape=jax.ShapeDtypeStruct((M, N), out_dtype),
        scratch_shapes=scratch,
        compiler_params=_params(2),
        name=name,
    )(a, w, bias, scale)


def _mm_nt_kernel(a_ref, wt_ref, b_ref, s_ref, o_ref, wbf_ref):
    @pl.when(pl.program_id(1) == 0)
    def _():
        _cast_rows(wt_ref.at[0], wbf_ref, 64)

    acc = _dot_nt(a_ref[...], wbf_ref[...])
    o_ref[...] = ((acc + b_ref[...]) * s_ref[...]).astype(o_ref.dtype)


def _in_proj(a, wt, layer, bias, scale, *, start, name):
    M, K = a.shape
    n = bias.shape[1]
    assert start % 8 == 0
    tm = _pick(M, (1056, 1024, 768, 512, 384, 256, 128))
    tn = _pick(n, (512, 256, 128))
    return pl.pallas_call(
        _mm_nt_kernel,
        grid=(n // tn, M // tm),
        in_specs=[
            pl.BlockSpec((tm, K), lambda j, i: (i, 0)),
            pl.BlockSpec((pl.Element(1), pl.Element(tn), pl.Element(K)),
                         lambda j, i: (layer, 8 * (start // 8 + j * (tn // 8)), 0)),
            pl.BlockSpec((1, tn), lambda j, i: (0, j)),
            pl.BlockSpec((1, tn), lambda j, i: (0, j)),
        ],
        out_specs=pl.BlockSpec((tm, tn), lambda j, i: (i, j)),
        out_shape=jax.ShapeDtypeStruct((M, n), BF16),
        scratch_shapes=[pltpu.VMEM((tn, K), BF16)],
        compiler_params=_params(2),
        name=name,
    )(a, wt, bias, scale)


def _prep_kernel(hb_ref, ws_ref, bs_ref, wa_ref, ba_ref, crow_ref, la_ref, carry_ref, *, pad, tt):
    i = pl.program_id(1)

    @pl.when(i == 0)
    def _():
        carry_ref[...] = jnp.zeros_like(carry_ref)

    z = _dot_nt(hb_ref[...], ws_ref[...].astype(BF16)) + bs_ref[...]
    pos = i * tt + lax.broadcasted_iota(jnp.int32, (tt, 1), 0)
    valid = pos >= pad
    la = _log_sigmoid(jnp.dot(z, wa_ref[...], preferred_element_type=F32, precision=HIGHEST) + ba_ref[...])
    la_ref[...] = jnp.where(valid, la * (1.0 / GLA_TAU), 0.0)
    lf = jnp.where(valid, _log_sigmoid(z), 0.0)

    r = lax.broadcasted_iota(jnp.int32, (ROW_TILE, ROW_TILE), 0)
    c = lax.broadcasted_iota(jnp.int32, (ROW_TILE, ROW_TILE), 1)
    tri = (r >= c).astype(F32)
    carry = carry_ref[...]
    for sb in range(tt // ROW_TILE):
        rows = slice(sb * ROW_TILE, (sb + 1) * ROW_TILE)
        cs = jnp.dot(tri, lf[rows], preferred_element_type=F32, precision=HIGHEST) + carry
        carry = cs[ROW_TILE - 1:ROW_TILE]
        posr = i * tt + sb * ROW_TILE + lax.broadcasted_iota(jnp.int32, (1, ROW_TILE), 1)
        crow_ref[0, :, rows] = jnp.where(posr >= pad, cs.T[0:8] * LOG2E, MASKED_KEY_BIAS)
    carry_ref[...] = carry


def _prep(hb, w_small, b_small, wa_ext, ba, *, B, lp, pad, tt):
    T, D = hb.shape
    KW = wa_ext.shape[1]
    nt = lp // tt
    return pl.pallas_call(
        functools.partial(_prep_kernel, pad=pad, tt=tt),
        grid=(B, nt),
        in_specs=[
            pl.BlockSpec((tt, D), lambda b, i: (b * nt + i, 0)),
            pl.BlockSpec((128, D), lambda b, i: (0, 0)),
            pl.BlockSpec((1, 128), lambda b, i: (0, 0)),
            pl.BlockSpec((128, KW), lambda b, i: (0, 0)),
            pl.BlockSpec((1, KW), lambda b, i: (0, 0)),
        ],
        out_specs=[
            pl.BlockSpec((1, 8, tt), lambda b, i: (b, 0, i)),
            pl.BlockSpec((tt, KW), lambda b, i: (b * nt + i, 0)),
        ],
        out_shape=[
            jax.ShapeDtypeStruct((B, 8, lp), F32),
            jax.ShapeDtypeStruct((T, KW), F32),
        ],
        scratch_shapes=[pltpu.VMEM((1, 128), F32)],
        compiler_params=_params(2),
        name="gate_prep",
    )(hb, w_small, b_small, wa_ext, ba)


def _fox_kernel(q_ref, k_ref, v_ref, crow_ref, o_ref, *, heads, tq, tk, group):
    qi = pl.program_id(1)
    n_full = (qi * tq) // tk
    kd = pl.multiple_of(n_full * tk, tk)
    rows = qi * tq + lax.broadcasted_iota(jnp.int32, (tq, tk), 0)
    cols = kd + lax.broadcasted_iota(jnp.int32, (tq, tk), 1)
    causal = cols <= rows
    ones = jnp.ones((tk, HEAD_DIM), BF16)

    def scores(h, k0):
        sl = slice(h * HEAD_DIM, (h + 1) * HEAD_DIM)
        cr = crow_ref[0, h:h + 1, pl.ds(k0, tk)]
        return _dot_nt(q_ref[:, sl], k_ref[pl.ds(k0, tk), sl]) - cr

    def weighted_values(h, p, k0):
        sl = slice(h * HEAD_DIM, (h + 1) * HEAD_DIM)
        va = jnp.concatenate([v_ref[pl.ds(k0, tk), sl], ones], axis=1)
        pv = jnp.dot(p, va, preferred_element_type=F32)
        return pv[:, :HEAD_DIM], pv[:, HEAD_DIM:HEAD_DIM + 1]

    for h0 in range(0, heads, group):
        hs = range(h0, min(h0 + group, heads))
        carry = []
        for h in hs:
            s = jnp.where(causal, scores(h, kd), -jnp.inf)
            m = jnp.max(s, axis=-1, keepdims=True)
            acc, l = weighted_values(h, jnp.exp2((s - m).astype(BF16)), kd)
            carry += [m, l, acc]

        def body(ki, carry):
            k0 = pl.multiple_of(ki * tk, tk)
            out = []
            for n, h in enumerate(hs):
                m, l, acc = carry[3 * n:3 * n + 3]
                s = scores(h, k0)
                m_new = jnp.maximum(m, jnp.max(s, axis=-1, keepdims=True))
                a = jnp.exp2(m - m_new)
                pv, p_sum = weighted_values(h, jnp.exp2((s - m_new).astype(BF16)), k0)
                out += [m_new, a * l + p_sum, a * acc + pv]
            return tuple(out)

        carry = lax.fori_loop(0, n_full, body, tuple(carry))
        for n, h in enumerate(hs):
            m, l, acc = carry[3 * n:3 * n + 3]
            o_ref[:, h * HEAD_DIM:(h + 1) * HEAD_DIM] = (acc / l).astype(o_ref.dtype)


def _fox(proj, crow, *, B, lp, fw, off_q, off_k, off_v, tq, tk):
    T = proj.shape[0]
    nq = lp // tq
    heads = fw // HEAD_DIM
    assert tk % tq == 0
    return pl.pallas_call(
        functools.partial(_fox_kernel, heads=heads, tq=tq, tk=tk, group=FOX_HEAD_GROUP),
        grid=(B, nq),
        in_specs=[
            pl.BlockSpec((tq, fw), lambda b, i: (b * nq + i, off_q // fw)),
            pl.BlockSpec((lp, fw), lambda b, i: (b, off_k // fw)),
            pl.BlockSpec((lp, fw), lambda b, i: (b, off_v // fw)),
            pl.BlockSpec((1, 8, lp), lambda b, i: (b, 0, 0)),
        ],
        out_specs=pl.BlockSpec((tq, fw), lambda b, i: (b * nq + i, 0)),
        out_shape=jax.ShapeDtypeStruct((T, fw), BF16),
        compiler_params=_params(2),
        name="fox_attention",
    )(proj, proj, proj, crow)


def _lru_kernel(lx_ref, lg_ref, cw_ref, cb_ref, wr_ref, br_ref, wi_ref, bi_ref, lam_ref, o_ref,
                ext_ref, a_ref, u_ref, hc_ref, *, pad, tt, nblk):
    i = pl.program_id(1)

    @pl.when(i == 0)
    def _():
        ext_ref[0:CONV_HALO, :] = jnp.zeros((CONV_HALO, ext_ref.shape[1]), F32)
        hc_ref[...] = jnp.zeros_like(hc_ref)

    @pl.when(i > 0)
    def _():
        ext_ref[0:CONV_HALO, :] = ext_ref[tt:tt + CONV_HALO, :]

    pos = i * tt + lax.broadcasted_iota(jnp.int32, (tt, 1), 0)
    valid = pos >= pad
    ext_ref[CONV_HALO:CONV_HALO + tt, :] = jnp.where(valid, lx_ref[...].astype(F32), 0.0)

    sp = jnp.maximum(-lam_ref[...], 0.0) + jnp.log1p(jnp.exp(-jnp.abs(lam_ref[...])))
    for n in range(nblk):
        sl = slice(n * HEAD_DIM, (n + 1) * HEAD_DIM)
        xc = cb_ref[:, sl]
        for j in range(CONV_W):
            start = CONV_HALO - (CONV_W - 1) + j
            xc = xc + cw_ref[j:j + 1, sl] * ext_ref[start:start + tt, sl]
        xb = xc.astype(BF16)
        r = _sigmoid(jnp.dot(xb, wr_ref[n].astype(BF16), preferred_element_type=F32) + br_ref[:, sl])
        g = _sigmoid(jnp.dot(xb, wi_ref[n].astype(BF16), preferred_element_type=F32) + bi_ref[:, sl])
        log_a = (-LRU_C) * r * sp[:, sl]
        a = jnp.exp(log_a)
        a_ref[:, sl] = a
        u = jnp.sqrt(-jnp.tanh(log_a) * (1.0 + a * a)) * (g * xc)
        u_ref[:, sl] = jnp.where(valid, u, 0.0)

    def body(gidx, h):
        r0 = pl.multiple_of(gidx * 8, 8)
        a8 = a_ref[pl.ds(r0, 8), :]
        u8 = u_ref[pl.ds(r0, 8), :]
        outs = []
        for r in range(8):
            h = a8[r:r + 1] * h + u8[r:r + 1]
            outs.append(h)
        u_ref[pl.ds(r0, 8), :] = jnp.concatenate(outs, axis=0)
        return h

    hc_ref[...] = lax.fori_loop(0, tt // 8, body, hc_ref[...])
    o_ref[...] = (u_ref[...] * _gelu_tanh(lg_ref[...].astype(F32))).astype(o_ref.dtype)


def _lru(proj, conv_w, conv_b, w_r, b_r, w_i, b_i, lam, *, B, lp, lw, off_x, off_g, pad, tt):
    T = proj.shape[0]
    nt = lp // tt
    nblk = lw // HEAD_DIM
    row = lambda b, i: (0, 0)
    return pl.pallas_call(
        functools.partial(_lru_kernel, pad=pad, tt=tt, nblk=nblk),
        grid=(B, nt),
        in_specs=[
            pl.BlockSpec((tt, lw), lambda b, i: (b * nt + i, off_x // lw)),
            pl.BlockSpec((tt, lw), lambda b, i: (b * nt + i, off_g // lw)),
            pl.BlockSpec((CONV_W, lw), row),
            pl.BlockSpec((1, lw), row),
            pl.BlockSpec((nblk, HEAD_DIM, HEAD_DIM), lambda b, i: (0, 0, 0)),
            pl.BlockSpec((1, lw), row),
            pl.BlockSpec((nblk, HEAD_DIM, HEAD_DIM), lambda b, i: (0, 0, 0)),
            pl.BlockSpec((1, lw), row),
            pl.BlockSpec((1, lw), row),
        ],
        out_specs=pl.BlockSpec((tt, lw), lambda b, i: (b * nt + i, 0)),
        out_shape=jax.ShapeDtypeStruct((T, lw), BF16),
        scratch_shapes=[
            pltpu.VMEM((tt + CONV_HALO, lw), F32),
            pltpu.VMEM((tt, lw), F32),
            pltpu.VMEM((tt, lw), F32),
            pltpu.VMEM((1, lw), F32),
        ],
        compiler_params=_params(2),
        name="conv_rglru",
    )(proj, proj, conv_w, conv_b, w_r, b_r, w_i, b_i, lam)


def _gla_kernel(q_ref, k_ref, v_ref, gg_ref, la_ref, ng_ref, o_ref, st_ref, *, pad, tt, heads):
    i = pl.program_id(1)

    @pl.when(i == 0)
    def _():
        st_ref[...] = jnp.zeros_like(st_ref)

    r = lax.broadcasted_iota(jnp.int32, (GLA_CHUNK, GLA_CHUNK), 0)
    c = lax.broadcasted_iota(jnp.int32, (GLA_CHUNK, GLA_CHUNK), 1)
    tri = (r >= c).astype(F32)
    for ci in range(tt // GLA_CHUNK):
        rows = slice(ci * GLA_CHUNK, (ci + 1) * GLA_CHUNK)
        pos = i * tt + ci * GLA_CHUNK + lax.broadcasted_iota(jnp.int32, (GLA_CHUNK, 1), 0)
        valid = pos >= pad
        for hd in range(heads):
            ks = slice(hd * HEAD_DIM, (hd + 1) * HEAD_DIM)
            vs = slice(hd * GLA_DV, (hd + 1) * GLA_DV)
            cs = jnp.dot(tri, la_ref[rows, ks], preferred_element_type=F32, precision=HIGHEST)
            cl = cs[GLA_CHUNK - 1:GLA_CHUNK]
            kdec = jnp.where(valid, k_ref[rows, ks].astype(F32) * jnp.exp(cl - cs), 0.0).astype(BF16)
            ut = lax.dot_general(v_ref[rows, vs], kdec, (((0,), (0,)), ((), ())), preferred_element_type=F32)
            st = st_ref[hd] * jnp.exp(cl) + ut
            st_ref[hd] = st
            o = _dot_nt(q_ref[rows, ks], st.astype(BF16))
            o = o * lax.rsqrt(jnp.mean(o * o, axis=-1, keepdims=True) + RMS_EPS) * ng_ref[:, vs]
            gg = gg_ref[rows, vs].astype(F32)
            o_ref[rows, vs] = (o * (gg * _sigmoid(gg))).astype(o_ref.dtype)


def _gla(proj, proj_g, la, norm_g, *, B, lp, kw, vw, off_q, off_k, off_v, off_g, pad, tt):
    T = proj.shape[0]
    nt = lp // tt
    heads = kw // HEAD_DIM
    return pl.pallas_call(
        functools.partial(_gla_kernel, pad=pad, tt=tt, heads=heads),
        grid=(B, nt),
        in_specs=[
            pl.BlockSpec((tt, kw), lambda b, i: (b * nt + i, off_q // kw)),
            pl.BlockSpec((tt, kw), lambda b, i: (b * nt + i, off_k // kw)),
            pl.BlockSpec((tt, vw), lambda b, i: (b * nt + i, off_v // vw)),
            pl.BlockSpec((tt, vw), lambda b, i: (b * nt + i, off_g // vw)),
            pl.BlockSpec((tt, kw), lambda b, i: (b * nt + i, 0)),
            pl.BlockSpec((1, vw), lambda b, i: (0, 0)),
        ],
        out_specs=pl.BlockSpec((tt, vw), lambda b, i: (b * nt + i, 0)),
        out_shape=jax.ShapeDtypeStruct((T, vw), BF16),
        scratch_shapes=[pltpu.VMEM((heads, GLA_DV, HEAD_DIM), F32)],
        compiler_params=_params(2),
        name="gla_chunked",
    )(proj, proj, proj, proj_g, la, norm_g)


def _merge_kernel(of_ref, ol_ref, og_ref, w_ref, g0_ref, g1_ref, g2_ref, o_ref, wbf_ref, *, fw, lw):
    @pl.when(pl.program_id(1) == 0)
    def _():
        wbf_ref[...] = w_ref[...].astype(BF16)

    y0 = jnp.dot(of_ref[...], wbf_ref[0:fw, :], preferred_element_type=F32)
    y1 = jnp.dot(ol_ref[...], wbf_ref[fw:fw + lw, :], preferred_element_type=F32)
    y2 = jnp.dot(og_ref[...], wbf_ref[fw + lw:, :], preferred_element_type=F32)
    out = (_sigmoid(g0_ref[...].astype(F32)) * y0 + _sigmoid(g1_ref[...].astype(F32)) * y1
           + _sigmoid(g2_ref[...].astype(F32)) * y2)
    o_ref[...] = out.astype(o_ref.dtype)


def _merge(o_fox, o_lru, o_gla, w_branch, layer, proj, *, off_gates, d):
    T, fw = o_fox.shape
    lw = o_lru.shape[1]
    vw = o_gla.shape[1]
    tm = _pick(T, (1056, 1024, 768, 512, 384, 256, 128))
    tn = _pick(math.gcd(d, off_gates), (512, 256, 128))
    gate_spec = lambda b: pl.BlockSpec((tm, tn), lambda j, i: (i, (off_gates + b * d) // tn + j))
    return pl.pallas_call(
        functools.partial(_merge_kernel, fw=fw, lw=lw),
        grid=(d // tn, T // tm),
        in_specs=[
            pl.BlockSpec((tm, fw), lambda j, i: (i, 0)),
            pl.BlockSpec((tm, lw), lambda j, i: (i, 0)),
            pl.BlockSpec((tm, vw), lambda j, i: (i, 0)),
            pl.BlockSpec((None, fw + lw + vw, tn), lambda j, i: (layer, 0, j)),
            gate_spec(0), gate_spec(1), gate_spec(2),
        ],
        out_specs=pl.BlockSpec((tm, tn), lambda j, i: (i, j)),
        out_shape=jax.ShapeDtypeStruct((T, d), BF16),
        scratch_shapes=[pltpu.VMEM((fw + lw + vw, tn), BF16)],
        compiler_params=_params(2),
        name="branch_merge",
    )(o_fox, o_lru, o_gla, w_branch, proj, proj, proj)


def _ln_router_kernel(h_ref, mix_ref, g_ref, b_ref, wr_ref, br_ref, hf_ref, hx_ref, idx_ref, wt_ref, *, alpha):
    y = _layer_norm(alpha * h_ref[...] + mix_ref[...], g_ref[...], b_ref[...])
    hf_ref[...] = y
    hx_ref[...] = _pack_halves(y)
    logits = jnp.dot(y, wr_ref[...], preferred_element_type=F32, precision=HIGHEST) + br_ref[...]
    lane = lax.broadcasted_iota(jnp.int32, logits.shape, 1).astype(F32)
    idx = jnp.zeros_like(logits)
    vals = []
    for k in range(TOP_K):
        mx = jnp.max(logits, axis=-1, keepdims=True)
        sel = jnp.min(jnp.where(logits == mx, lane, float(logits.shape[1])), axis=-1, keepdims=True)
        vals.append(mx)
        idx = jnp.where(lane == float(k), sel, idx)
        logits = jnp.where(lane == sel, -jnp.inf, logits)
    es = [jnp.exp(v - vals[0]) for v in vals]
    tot = es[0]
    for e in es[1:]:
        tot = tot + e
    wt = jnp.zeros_like(logits)
    for k in range(TOP_K):
        wt = jnp.where(lane == float(k), es[k] / tot, wt)
    idx_ref[...] = idx.astype(jnp.int32)
    wt_ref[...] = wt


def _ln_router(h, mix, g, b, wr_pad, br_pad, *, alpha):
    T, D = h.shape
    tm = _pick(T, (192, 128))
    blk = pl.BlockSpec((tm, D), lambda i: (i, 0))
    row = pl.BlockSpec((1, D), lambda i: (0, 0))
    small = pl.BlockSpec((tm, 128), lambda i: (i, 0))
    return pl.pallas_call(
        functools.partial(_ln_router_kernel, alpha=alpha),
        grid=(T // tm,),
        in_specs=[blk, blk, row, row, pl.BlockSpec((D, 128), lambda i: (0, 0)), pl.BlockSpec((1, 128), lambda i: (0, 0))],
        out_specs=[blk, pl.BlockSpec((tm, D // 2), lambda i: (i, 0)), small, small],
        out_shape=[
            jax.ShapeDtypeStruct((T, D), F32),
            jax.ShapeDtypeStruct((T, D // 2), U32),
            jax.ShapeDtypeStruct((T, 128), jnp.int32),
            jax.ShapeDtypeStruct((T, 128), F32),
        ],
        compiler_params=_params(1),
        name="ln_router",
    )(h, mix, g, b, wr_pad, br_pad)


def _pack_halves(y):
    half = y.shape[1] // 2
    lo = pltpu.bitcast(y[:, :half].astype(BF16).astype(F32), U32)
    hi = pltpu.bitcast(y[:, half:].astype(BF16).astype(F32), U32)
    return lax.shift_right_logical(lo, jnp.uint32(16)) | (hi & jnp.uint32(0xFFFF0000))


def _unpack_halves(w):
    lo = pltpu.bitcast(lax.shift_left(w, jnp.uint32(16)), F32)
    hi = pltpu.bitcast(w & jnp.uint32(0xFFFF0000), F32)
    return lo, hi


def _row_gather(src_hbm, idx_ref, base, buf_ref, sem, n, static=False):
    def start(r):
        pltpu.make_async_copy(src_hbm.at[pl.ds(idx_ref[base + r], 1)], buf_ref.at[pl.ds(r, 1)], sem).start()

    if static:
        for r in range(n):
            start(r)
    else:
        def body(r, carry):
            start(r)
            return carry

        lax.fori_loop(0, n, body, 0, unroll=8)


def _tile_wait(src_hbm, buf_ref, sem):
    pltpu.make_async_copy(src_hbm.at[pl.ds(0, buf_ref.shape[0])], buf_ref, sem).wait()


def _moe_up_kernel(te_ref, na_ref, first_ref, nxt_ref, tok_ref, h_hbm, wup_hbm, bup_ref, sel_ref, act_ref,
                   xbuf_ref, sem_ref, stage_ref, wbf_ref, wsem_ref, *, tmx, layer):
    i = pl.program_id(0)
    na = na_ref[0]
    cur = lax.rem(i, GATHER_RING)
    ahead = lax.rem(i + GATHER_RING - 1, GATHER_RING)

    def weight_copy(e):
        return pltpu.make_async_copy(wup_hbm.at[layer, e], stage_ref, wsem_ref.at[0])

    for t in range(GATHER_RING - 1):
        @pl.when(jnp.logical_and(i == 0, na > t))
        def _():
            _row_gather(h_hbm, tok_ref, t * tmx, xbuf_ref.at[t], sem_ref.at[t], tmx)

    @pl.when(jnp.logical_and(i == 0, na > 0))
    def _():
        weight_copy(te_ref[0]).start(priority=1)

    @pl.when(jnp.logical_and(i < na, first_ref[i] == 1))
    def _():
        weight_copy(te_ref[i]).wait()
        _cast_rows(stage_ref, wbf_ref, 256)

        @pl.when(nxt_ref[i] >= 0)
        def _():
            weight_copy(nxt_ref[i]).start(priority=1)

    @pl.when(i < na)
    def _():
        _tile_wait(h_hbm, xbuf_ref.at[cur], sem_ref.at[cur])
        lo, hi = _unpack_halves(xbuf_ref[cur])
        x = jnp.concatenate([lo.astype(BF16), hi.astype(BF16)], axis=1)
        h = jnp.dot(x, wbf_ref[...], preferred_element_type=F32) + bup_ref[...]
        g = jnp.minimum(h, SWIGLU_LIMIT)
        u = jnp.clip(pltpu.roll(h, h.shape[1] - 1, axis=1), -SWIGLU_LIMIT, SWIGLU_LIMIT)
        act = ((u + 1.0) * g * _sigmoid(SWIGLU_ALPHA * g)).astype(BF16)
        act_ref[...] = jnp.dot(act, sel_ref[...], preferred_element_type=F32).astype(BF16)

    @pl.when(i + GATHER_RING - 1 < na)
    def _():
        _row_gather(h_hbm, tok_ref, (i + GATHER_RING - 1) * tmx, xbuf_ref.at[ahead], sem_ref.at[ahead], tmx,
                    static=True)

    @pl.when(i >= na)
    def _():
        act_ref[...] = jnp.zeros_like(act_ref)


def _moe_down_kernel(te_ref, na_ref, first_ref, act_ref, wd_ref, bd_ref, y_ref, wbf_ref):
    i = pl.program_id(0)
    na = na_ref[0]

    @pl.when(jnp.logical_and(i < na, first_ref[i] == 1))
    def _():
        _cast_rows(wd_ref, wbf_ref, 128)

    @pl.when(i < na)
    def _():
        y = jnp.dot(act_ref[...], wbf_ref[...], preferred_element_type=F32) + bd_ref[...]
        y_ref[...] = _pack_halves(y)

    @pl.when(i >= na)
    def _():
        y_ref[...] = jnp.zeros_like(y_ref)


def _moe(tile_expert, n_active, row_token, hx, w_up, b_up, sel, w_down, b_down, *, tmx, layer):
    T, D = hx.shape[0], 2 * hx.shape[1]
    _, E, _, F2 = w_up.shape
    F = F2 // 2
    P = row_token.shape[0]
    ntiles = P // tmx
    tiles = jnp.arange(ntiles, dtype=jnp.int32)
    first = jnp.where(tiles == 0, 1, (tile_expert != jnp.roll(tile_expert, 1)).astype(jnp.int32))
    j = jnp.searchsorted(tile_expert, tile_expert, side="right").astype(jnp.int32)
    nxt = jnp.where(j < ntiles, tile_expert[jnp.minimum(j, ntiles - 1)], -1).astype(jnp.int32)

    act = pl.pallas_call(
        functools.partial(_moe_up_kernel, tmx=tmx, layer=layer),
        grid_spec=pltpu.PrefetchScalarGridSpec(
            num_scalar_prefetch=5,
            grid=(ntiles,),
            in_specs=[
                pl.BlockSpec(memory_space=pl.ANY),
                pl.BlockSpec(memory_space=pl.ANY),
                pl.BlockSpec((None, 1, F2), lambda i, te, *_: (te[i], 0, 0)),
                pl.BlockSpec((F2, F), lambda i, *_: (0, 0)),
            ],
            out_specs=pl.BlockSpec((tmx, F), lambda i, *_: (i, 0)),
            scratch_shapes=[
                pltpu.VMEM((GATHER_RING, tmx, D // 2), U32), pltpu.SemaphoreType.DMA((GATHER_RING,)),
                pltpu.VMEM((D, F2), F32), pltpu.VMEM((D, F2), BF16), pltpu.SemaphoreType.DMA((1,)),
            ],
        ),
        out_shape=jax.ShapeDtypeStruct((P, F), BF16),
        compiler_params=_params(1),
        name="moe_up",
    )(tile_expert, n_active, first, nxt, row_token, hx, w_up, b_up, sel)

    return pl.pallas_call(
        _moe_down_kernel,
        grid_spec=pltpu.PrefetchScalarGridSpec(
            num_scalar_prefetch=3,
            grid=(ntiles,),
            in_specs=[
                pl.BlockSpec((tmx, F), lambda i, *_: (i, 0)),
                pl.BlockSpec((None, None, F, D), lambda i, te, *_: (layer, te[i], 0, 0)),
                pl.BlockSpec((None, 1, D), lambda i, te, *_: (te[i], 0, 0)),
            ],
            out_specs=pl.BlockSpec((tmx, D // 2), lambda i, *_: (i, 0)),
            scratch_shapes=[pltpu.VMEM((F, D), BF16)],
        ),
        out_shape=jax.ShapeDtypeStruct((P, D // 2), U32),
        compiler_params=_params(1),
        name="moe_down",
    )(tile_expert, n_active, first, act, w_down, b_down)


def _combine_kernel(pos_ref, y_hbm, h_ref, w_ref, g_ref, b_ref, *rest, alpha, tc, nt, final):
    if final:
        out_ref, buf_ref, sem_ref = rest
    else:
        hf_ref, hb_ref, buf_ref, sem_ref = rest
    i = pl.program_id(0)
    cur = lax.rem(i, GATHER_RING)
    ahead = lax.rem(i + GATHER_RING - 1, GATHER_RING)

    def start(tile, slot, static):
        for k in range(TOP_K):
            _row_gather(y_hbm, pos_ref, (k * nt + tile) * tc, buf_ref.at[slot, k], sem_ref.at[slot], tc, static=static)

    @pl.when(i == 0)
    def _():
        for t in range(min(GATHER_RING - 1, nt)):
            start(t, t, False)

    for k in range(TOP_K):
        _tile_wait(y_hbm, buf_ref.at[cur, k], sem_ref.at[cur])
    lo = hi = None
    for k in range(TOP_K):
        l, h = _unpack_halves(buf_ref[cur, k])
        wk = w_ref[:, k:k + 1]
        lo = wk * l if lo is None else lo + wk * l
        hi = wk * h if hi is None else hi + wk * h
    ffn = jnp.concatenate([lo, hi], axis=1)
    y = _layer_norm(alpha * h_ref[...] + ffn, g_ref[...], b_ref[...])
    if final:
        out_ref[...] = y
    else:
        hf_ref[...] = y
        hb_ref[...] = y.astype(BF16)

    @pl.when(i + GATHER_RING - 1 < nt)
    def _():
        start(i + GATHER_RING - 1, ahead, True)


def _combine(pos, ys, hf, top_w, g, b, *, alpha, lp, final):
    T, D = hf.shape
    tc = ROW_TILE
    nt = T // tc
    ntb = lp // tc
    if final:
        out_specs = pl.BlockSpec((tc, D), lambda i, pos: ((i // ntb) * (ntb - 1) + jnp.maximum(i % ntb - 1, 0), 0))
        out_shape = jax.ShapeDtypeStruct(((T // lp) * (lp - tc), D), F32)
    else:
        out_specs = [pl.BlockSpec((tc, D), lambda i, pos: (i, 0)), pl.BlockSpec((tc, D), lambda i, pos: (i, 0))]
        out_shape = [jax.ShapeDtypeStruct((T, D), F32), jax.ShapeDtypeStruct((T, D), BF16)]
    grid_spec = pltpu.PrefetchScalarGridSpec(
        num_scalar_prefetch=1,
        grid=(nt,),
        in_specs=[
            pl.BlockSpec(memory_space=pl.ANY),
            pl.BlockSpec((tc, D), lambda i, pos: (i, 0)),
            pl.BlockSpec((tc, 128), lambda i, pos: (i, 0)),
            pl.BlockSpec((1, D), lambda i, pos: (0, 0)),
            pl.BlockSpec((1, D), lambda i, pos: (0, 0)),
        ],
        out_specs=out_specs,
        scratch_shapes=[pltpu.VMEM((GATHER_RING, TOP_K, tc, D // 2), U32), pltpu.SemaphoreType.DMA((GATHER_RING,))],
    )
    return pl.pallas_call(
        functools.partial(_combine_kernel, alpha=alpha, tc=tc, nt=nt, final=final),
        grid_spec=grid_spec,
        out_shape=out_shape,
        compiler_params=_params(1),
        name="moe_combine_ln",
    )(pos, ys, hf, top_w, g, b)


def _route_kernel(idx_ref, pos_ref, te_ref, na_ref, cnt_ref, pst_ref, *, pad, tr, tmx, n_experts, spare_row):
    p = pl.program_id(0)
    i = pl.program_id(2)
    first = jnp.logical_and(pl.program_id(1) == 0, i == 0)

    @pl.when(jnp.logical_and(p == 0, first))
    def _():
        cnt_ref[...] = jnp.zeros_like(cnt_ref)

    @pl.when(jnp.logical_and(p == 1, first))
    def _():
        cnt = cnt_ref[...]
        padded = jnp.floor((cnt + (tmx - 1.0)) * (1.0 / tmx)) * tmx
        r = lax.broadcasted_iota(jnp.int32, (128, 128), 0)
        c = lax.broadcasted_iota(jnp.int32, (128, 128), 1)
        pst = jnp.dot(padded, (r < c).astype(F32), preferred_element_type=F32, precision=HIGHEST)
        pst_ref[...] = pst
        cnt_ref[...] = jnp.zeros_like(cnt_ref)
        tile_end = (pst + padded) * (1.0 / tmx)
        nt_pad = te_ref.shape[0]
        t = lax.broadcasted_iota(jnp.int32, (nt_pad, 128), 0).astype(F32)
        lane = lax.broadcasted_iota(jnp.int32, (nt_pad, 128), 1)
        is_expert = lane < n_experts
        te = jnp.sum(jnp.where(jnp.logical_and(tile_end <= t, is_expert), 1.0, 0.0), axis=-1, keepdims=True)
        lane1 = lax.broadcasted_iota(jnp.int32, (1, 128), 1).astype(F32)
        e_last = jnp.max(jnp.where(cnt > 0.0, lane1, 0.0), axis=-1, keepdims=True)
        te_ref[...] = jnp.broadcast_to(jnp.minimum(te, e_last), (nt_pad, 128)).astype(jnp.int32)
        na_ref[...] = jnp.broadcast_to(jnp.max(tile_end, axis=-1, keepdims=True), (1, 128)).astype(jnp.int32)

    idx = idx_ref[...]
    lane = lax.broadcasted_iota(jnp.int32, (tr, 128), 1)
    valid = (i * tr + lax.broadcasted_iota(jnp.int32, (tr, 1), 0)) >= pad
    onehot = jnp.zeros((tr, 128), F32)
    for k in range(TOP_K):
        onehot = onehot + jnp.where(lane == idx[:, k:k + 1], 1.0, 0.0)
    onehot = jnp.where(valid, onehot, 0.0)
    r = lax.broadcasted_iota(jnp.int32, (tr, tr), 0)
    c = lax.broadcasted_iota(jnp.int32, (tr, tr), 1)
    rank = jnp.dot((r > c).astype(BF16), onehot.astype(BF16), preferred_element_type=F32) + cnt_ref[...]
    cnt_ref[...] = cnt_ref[...] + jnp.sum(onehot, axis=0, keepdims=True)

    @pl.when(p == 0)
    def _():
        pos_ref[...] = jnp.zeros_like(pos_ref)

    @pl.when(p == 1)
    def _():
        dest = rank + pst_ref[...]
        out = jnp.zeros((tr, 128), F32)
        for k in range(TOP_K):
            d = jnp.sum(jnp.where(lane == idx[:, k:k + 1], dest, 0.0), axis=-1, keepdims=True)
            out = jnp.where(lane == k, jnp.where(valid, d, float(spare_row)), out)
        pos_ref[...] = out.astype(jnp.int32)


def _route(top_idx, *, B, lp, pad, tmx, n_experts, n_rows):
    T = top_idx.shape[0]
    tr = _pick(lp, (384, 256, 128))
    nt = lp // tr
    nt_pad = -(-(n_rows // tmx) // 8) * 8
    pos, te, na = pl.pallas_call(
        functools.partial(_route_kernel, pad=pad, tr=tr, tmx=tmx, n_experts=n_experts, spare_row=n_rows - 1),
        grid=(2, B, nt),
        in_specs=[pl.BlockSpec((tr, 128), lambda p, b, i: (b * nt + i, 0))],
        out_specs=[
            pl.BlockSpec((tr, 128), lambda p, b, i: (p * (b * nt + i), 0)),
            pl.BlockSpec((nt_pad, 128), lambda p, b, i: (0, 0)),
            pl.BlockSpec((1, 128), lambda p, b, i: (0, 0)),
        ],
        out_shape=[
            jax.ShapeDtypeStruct((T, 128), jnp.int32),
            jax.ShapeDtypeStruct((nt_pad, 128), jnp.int32),
            jax.ShapeDtypeStruct((1, 128), jnp.int32),
        ],
        scratch_shapes=[pltpu.VMEM((1, 128), F32), pltpu.VMEM((1, 128), F32)],
        compiler_params=_params(3),
        name="route_rank",
    )(top_idx)
    pos_flat = pos[:, :TOP_K].T.reshape(-1)
    return pos_flat, te[:n_rows // tmx, 0], na[0, :1]


def _row_token_kernel(pos_ref, rt_ref, *, t):
    def zero(r, carry):
        rt_ref[r] = 0
        return carry

    lax.fori_loop(0, rt_ref.shape[0], zero, 0, unroll=8)
    for k in range(TOP_K):
        def body(tok, carry):
            rt_ref[pos_ref[k * t + tok]] = tok
            return carry

        lax.fori_loop(0, t, body, 0, unroll=8)


def _row_token(pos_flat, *, t, n_rows):
    return pl.pallas_call(
        functools.partial(_row_token_kernel, t=t),
        grid_spec=pltpu.PrefetchScalarGridSpec(
            num_scalar_prefetch=1, grid=(1,), in_specs=[],
            out_specs=pl.BlockSpec(memory_space=pltpu.SMEM)),
        out_shape=jax.ShapeDtypeStruct((n_rows,), jnp.int32),
        compiler_params=_params(1),
        name="route_row_token",
    )(pos_flat)


def _segments(fw, fh, lw, kw, vw, rank, d):
    names = ("fq", "fk", "fv", "ff", "lx", "lg", "gq", "gk", "gv", "ga", "gg", "gate0", "gate1", "gate2")
    widths = (fw, fw, fw, fh, lw, lw, kw, kw, vw, rank, vw, d, d, d)
    segs, off = {}, 0
    for n, w in zip(names, widths):
        segs[n] = (off, w)
        off += w
    return segs, off


def kernel(x, meta_tokens, emb_ln_g, emb_ln_b, w_in, b_in, conv_w, conv_b, lru_w_r, lru_b_r, lru_w_i, lru_b_i,
           lru_lambda, gla_w_alpha, gla_b_alpha, gla_norm_g, w_branch, w_out, b_out, ln1_g, ln1_b, w_router,
           b_router, w_up, b_up, w_down, b_down, ln2_g, ln2_b):
    B, S, D = x.shape
    n_meta = meta_tokens.shape[0]
    depth = w_in.shape[0]
    L = S + n_meta
    pad = (-L) % ROW_TILE
    lp = L + pad
    assert pad + n_meta == ROW_TILE and S % ROW_TILE == 0
    T = B * lp

    lw = conv_w.shape[2]
    rank, kw = gla_w_alpha.shape[1:]
    vw = gla_norm_g.shape[1]
    n_experts = w_router.shape[2]
    fexp = w_up.shape[3] // 2
    in_cols = w_in.shape[2]
    fh = (in_cols - 2 * lw - 2 * kw - 2 * vw - rank - 3 * D) // (3 * HEAD_DIM + 1)
    fw = fh * HEAD_DIM
    segs, total = _segments(fw, fh, lw, kw, vw, rank, D)
    assert total == in_cols and fh <= 8 and fh + rank <= 128
    alpha = (2.0 * depth) ** 0.25

    windows = (("fq", "fk", "fv"), ("lx", "lg", "gq", "gk", "gv"), ("gg", "gate0", "gate1", "gate2"))
    qscale = HEAD_DIM ** -0.5
    off, win_start, win_scale = {}, [], []
    for names in windows:
        start = segs[names[0]][0]
        for n in names:
            off[n] = segs[n][0] - start
            assert off[n] % segs[n][1] == 0 or n.startswith("gate")
        win_start.append(start)
        scales = {"fq": qscale * LOG2E, "gq": qscale}
        win_scale.append(jnp.concatenate(
            [jnp.full((segs[n][1],), scales.get(n, 1.0), F32) for n in names]).reshape(1, -1))

    def cols(a, n):
        o, w = segs[n]
        return a[..., o:o + w]

    tt = _pick(lp, (384, 256, 128))
    tmx = MOE_ROW_TILE if T >= 4096 else 128
    n_rows = (-(-(B * L * TOP_K + n_experts * (tmx - 1)) // tmx) + 1) * tmx

    head = jnp.concatenate([jnp.zeros((pad, D), F32), meta_tokens.astype(F32)], axis=0)
    hf, hb = _embed_ln(x, head, emb_ln_g.reshape(1, D), emb_ln_b.reshape(1, D), pad=pad, lp=lp)
    sel = (jnp.arange(2 * fexp)[:, None] == 2 * jnp.arange(fexp)[None, :]).astype(BF16)

    w_in_t = jnp.swapaxes(w_in, 1, 2)

    def rows(l, n):
        o, w = segs[n]
        return w_in_t[l, o:o + w]

    for l in range(depth):
        w_small = jnp.concatenate(
            [rows(l, "ff"), jnp.zeros((8 - fh, D), F32), rows(l, "ga"), jnp.zeros((128 - 8 - rank, D), F32)], axis=0)
        b_small = jnp.concatenate(
            [cols(b_in[l], "ff"), jnp.zeros((8 - fh,), F32), cols(b_in[l], "ga"),
             jnp.zeros((128 - 8 - rank,), F32)]).reshape(1, 128)
        wa_ext = jnp.zeros((128, kw), F32).at[8:8 + rank].set(gla_w_alpha[l])

        pa, pb, pc = [
            _in_proj(hb, w_in_t, l, b_in[l][s:s + sc.shape[1]].reshape(1, -1), sc, start=s, name=f"in_proj_{k}")
            for k, (s, sc) in enumerate(zip(win_start, win_scale))]
        crow, la = _prep(hb, w_small, b_small, wa_ext, gla_b_alpha[l].reshape(1, kw), B=B, lp=lp, pad=pad, tt=tt)
        o_fox = _fox(pa, crow, B=B, lp=lp, fw=fw, off_q=off["fq"], off_k=off["fk"], off_v=off["fv"], tq=FOX_Q_TILE,
                     tk=tt)
        o_lru = _lru(pb, conv_w[l], conv_b[l].reshape(1, lw), lru_w_r[l], lru_b_r[l].reshape(1, lw), lru_w_i[l],
                     lru_b_i[l].reshape(1, lw), lru_lambda[l].reshape(1, lw), B=B, lp=lp, lw=lw, off_x=off["lx"],
                     off_g=off["lg"], pad=pad, tt=tt)
        o_gla = _gla(pb, pc, la, gla_norm_g[l].reshape(1, vw), B=B, lp=lp, kw=kw, vw=vw, off_q=off["gq"],
                     off_k=off["gk"], off_v=off["gv"], off_g=off["gg"], pad=pad, tt=tt)
        merged = _merge(o_fox, o_lru, o_gla, w_branch, l, pc, off_gates=off["gate0"], d=D)
        mix = _matmul(merged, w_out, l, b_out[l].reshape(1, D), jnp.ones((1, D), F32), F32, "out_proj")

        wr_pad = jnp.zeros((D, 128), F32).at[:, :n_experts].set(w_router[l])
        br_pad = jnp.full((1, 128), -MASKED_KEY_BIAS, F32).at[0, :n_experts].set(b_router[l])
        hf, hx, top_idx, top_w = _ln_router(hf, mix, ln1_g[l].reshape(1, D), ln1_b[l].reshape(1, D), wr_pad, br_pad,
                                            alpha=alpha)
        pos, tile_expert, n_active = _route(top_idx, B=B, lp=lp, pad=pad, tmx=tmx, n_experts=n_experts, n_rows=n_rows)
        row_token = _row_token(pos, t=T, n_rows=n_rows)
        ys = _moe(tile_expert, n_active, row_token, hx, w_up, b_up[l].reshape(n_experts, 1, 2 * fexp), sel, w_down,
                  b_down[l].reshape(n_experts, 1, D), tmx=tmx, layer=l)
        final = l == depth - 1
        res = _combine(pos, ys, hf, top_w, ln2_g[l].reshape(1, D), ln2_b[l].reshape(1, D), alpha=alpha, lp=lp,
                       final=final)
        if final:
            return res.reshape(B, S, D)
        hf, hb = res
```

```python
import functools
import math

import jax
import jax.numpy as jnp
from jax import lax
from jax.experimental import pallas as pl
from jax.experimental.pallas import tpu as pltpu

F32 = jnp.float32
BF16 = jnp.bfloat16
U32 = jnp.uint32
HIGHEST = lax.Precision.HIGHEST

ROW_TILE = 128
HEAD_DIM = 128
GLA_DV = 256
GLA_CHUNK = 64
CONV_W = 4
CONV_HALO = 8
LN_EPS = 1e-5
RMS_EPS = 1e-6
LRU_C = 8.0
GLA_TAU = 16.0
SWIGLU_LIMIT = 7.0
SWIGLU_ALPHA = 1.702
TOP_K = 4
LOG2E = 1.4426950408889634
MASKED_KEY_BIAS = 1e30
MOE_ROW_TILE = 256
FOX_HEAD_GROUP = 4
FOX_Q_TILE = 384
GATHER_RING = 3
VMEM_LIMIT = 56 * 1024 * 1024


def _pick(n, candidates):
    for c in candidates:
        if n % c == 0:
            return c
    raise ValueError(f"no tile in {candidates} divides {n}")


def _params(n_axes, vmem=VMEM_LIMIT):
    return pltpu.CompilerParams(dimension_semantics=("arbitrary",) * n_axes, vmem_limit_bytes=vmem)


def _layer_norm(v, g, b):
    mu = jnp.mean(v, axis=-1, keepdims=True)
    d = v - mu
    var = jnp.mean(d * d, axis=-1, keepdims=True)
    return d * lax.rsqrt(var + LN_EPS) * g + b


def _log_sigmoid(x):
    return jnp.minimum(x, 0.0) - jnp.log1p(jnp.exp(-jnp.abs(x)))


def _sigmoid(x):
    return 0.5 * jnp.tanh(0.5 * x) + 0.5


def _gelu_tanh(x):
    return 0.5 * x * (1.0 + jnp.tanh(0.7978845608028654 * (x + 0.044715 * (x * x * x))))


def _embed_ln_kernel(x_ref, head_ref, g_ref, b_ref, hf_ref, hb_ref, *, pad):
    i = pl.program_id(1)

    @pl.when(i == 0)
    def _():
        y = _layer_norm(head_ref[...], g_ref[...], b_ref[...])
        rows = lax.broadcasted_iota(jnp.int32, (ROW_TILE, 1), 0)
        y = jnp.where(rows >= pad, y, 0.0)
        hf_ref[...] = y
        hb_ref[...] = y.astype(BF16)

    @pl.when(i > 0)
    def _():
        y = _layer_norm(x_ref[0], g_ref[...], b_ref[...])
        hf_ref[...] = y
        hb_ref[...] = y.astype(BF16)


def _embed_ln(x, head, g, b, *, pad, lp):
    B, S, D = x.shape
    nt = lp // ROW_TILE
    T = B * lp
    return pl.pallas_call(
        functools.partial(_embed_ln_kernel, pad=pad),
        grid=(B, nt),
        in_specs=[
            pl.BlockSpec((1, ROW_TILE, D), lambda b, i: (b, jnp.maximum(i - 1, 0), 0)),
            pl.BlockSpec((ROW_TILE, D), lambda b, i: (0, 0)),
            pl.BlockSpec((1, D), lambda b, i: (0, 0)),
            pl.BlockSpec((1, D), lambda b, i: (0, 0)),
        ],
        out_specs=[
            pl.BlockSpec((ROW_TILE, D), lambda b, i: (b * nt + i, 0)),
            pl.BlockSpec((ROW_TILE, D), lambda b, i: (b * nt + i, 0)),
        ],
        out_shape=[jax.ShapeDtypeStruct((T, D), F32), jax.ShapeDtypeStruct((T, D), BF16)],
        compiler_params=_params(2),
        name="embed_ln",
    )(x, head, g, b)


def _dot_nt(a, b):
    return lax.dot_general(a, b, (((1,), (1,)), ((), ())), preferred_element_type=F32)


def _cast_rows(src_ref, dst_ref, chunk):
    def body(c, carry):
        r0 = pl.multiple_of(c * chunk, chunk)
        dst_ref[pl.ds(r0, chunk), :] = src_ref[pl.ds(r0, chunk), :].astype(BF16)
        return carry

    lax.fori_loop(0, src_ref.shape[0] // chunk, body, 0)


def _mm_kernel(a_ref, w_ref, b_ref, s_ref, o_ref, *scratch):
    if scratch:
        (wbf_ref,) = scratch

        @pl.when(pl.program_id(1) == 0)
        def _():
            wbf_ref[...] = w_ref[...].astype(BF16)

        w = wbf_ref[...]
    else:
        w = w_ref[...]
    acc = jnp.dot(a_ref[...], w, preferred_element_type=F32)
    o_ref[...] = ((acc + b_ref[...]) * s_ref[...]).astype(o_ref.dtype)


def _matmul(a, w, layer, bias, scale, out_dtype, name):
    M, K = a.shape
    N = w.shape[2]
    tm = _pick(M, (1056, 1024, 768, 512, 384, 256, 128))
    tn = _pick(N, (512, 256, 128))
    scratch = [pltpu.VMEM((K, tn), BF16)] if w.dtype != BF16 else []
    return pl.pallas_call(
        _mm_kernel,
        grid=(N // tn, M // tm),
        in_specs=[
            pl.BlockSpec((tm, K), lambda j, i: (i, 0)),
            pl.BlockSpec((None, K, tn), lambda j, i: (layer, 0, j)),
            pl.BlockSpec((1, tn), lambda j, i: (0, j)),
            pl.BlockSpec((1, tn), lambda j, i: (0, j)),
        ],
        out_specs=pl.BlockSpec((tm, tn), lambda j, i: (i, j)),
        out_shape=jax.ShapeDtypeStruct((M, N), out_dtype),
        scratch_shapes=scratch,
        compiler_params=_params(2),
        name=name,
    )(a, w, bias, scale)


def _mm_nt_kernel(a_ref, wt_ref, b_ref, s_ref, o_ref, wbf_ref):
    @pl.when(pl.program_id(1) == 0)
    def _():
        _cast_rows(wt_ref.at[0], wbf_ref, 64)

    acc = _dot_nt(a_ref[...], wbf_ref[...])
    o_ref[...] = ((acc + b_ref[...]) * s_ref[...]).astype(o_ref.dtype)


def _in_proj(a, wt, layer, bias, scale, *, start, name):
    M, K = a.shape
    n = bias.shape[1]
    assert start % 8 == 0
    tm = _pick(M, (1056, 1024, 768, 512, 384, 256, 128))
    tn = _pick(n, (512, 256, 128))
    return pl.pallas_call(
        _mm_nt_kernel,
        grid=(n // tn, M // tm),
        in_specs=[
            pl.BlockSpec((tm, K), lambda j, i: (i, 0)),
            pl.BlockSpec((pl.Element(1), pl.Element(tn), pl.Element(K)),
                         lambda j, i: (layer, 8 * (start // 8 + j * (tn // 8)), 0)),
            pl.BlockSpec((1, tn), lambda j, i: (0, j)),
            pl.BlockSpec((1, tn), lambda j, i: (0, j)),
        ],
        out_specs=pl.BlockSpec((tm, tn), lambda j, i: (i, j)),
        out_shape=jax.ShapeDtypeStruct((M, n), BF16),
        scratch_shapes=[pltpu.VMEM((tn, K), BF16)],
        compiler_params=_params(2),
        name=name,
    )(a, wt, bias, scale)


def _prep_kernel(hb_ref, ws_ref, bs_ref, wa_ref, ba_ref, crow_ref, la_ref, carry_ref, *, pad, tt):
    i = pl.program_id(1)

    @pl.when(i == 0)
    def _():
        carry_ref[...] = jnp.zeros_like(carry_ref)

    z = _dot_nt(hb_ref[...], ws_ref[...].astype(BF16)) + bs_ref[...]
    pos = i * tt + lax.broadcasted_iota(jnp.int32, (tt, 1), 0)
    valid = pos >= pad
    la = _log_sigmoid(jnp.dot(z, wa_ref[...], preferred_element_type=F32, precision=HIGHEST) + ba_ref[...])
    la_ref[...] = jnp.where(valid, la * (1.0 / GLA_TAU), 0.0)
    lf = jnp.where(valid, _log_sigmoid(z), 0.0)

    r = lax.broadcasted_iota(jnp.int32, (ROW_TILE, ROW_TILE), 0)
    c = lax.broadcasted_iota(jnp.int32, (ROW_TILE, ROW_TILE), 1)
    tri = (r >= c).astype(F32)
    carry = carry_ref[...]
    for sb in range(tt // ROW_TILE):
        rows = slice(sb * ROW_TILE, (sb + 1) * ROW_TILE)
        cs = jnp.dot(tri, lf[rows], preferred_element_type=F32, precision=HIGHEST) + carry
        carry = cs[ROW_TILE - 1:ROW_TILE]
        posr = i * tt + sb * ROW_TILE + lax.broadcasted_iota(jnp.int32, (1, ROW_TILE), 1)
        crow_ref[0, :, rows] = jnp.where(posr >= pad, cs.T[0:8] * LOG2E, MASKED_KEY_BIAS)
    carry_ref[...] = carry


def _prep(hb, w_small, b_small, wa_ext, ba, *, B, lp, pad, tt):
    T, D = hb.shape
    KW = wa_ext.shape[1]
    nt = lp // tt
    return pl.pallas_call(
        functools.partial(_prep_kernel, pad=pad, tt=tt),
        grid=(B, nt),
        in_specs=[
            pl.BlockSpec((tt, D), lambda b, i: (b * nt + i, 0)),
            pl.BlockSpec((128, D), lambda b, i: (0, 0)),
            pl.BlockSpec((1, 128), lambda b, i: (0, 0)),
            pl.BlockSpec((128, KW), lambda b, i: (0, 0)),
            pl.BlockSpec((1, KW), lambda b, i: (0, 0)),
        ],
        out_specs=[
            pl.BlockSpec((1, 8, tt), lambda b, i: (b, 0, i)),
            pl.BlockSpec((tt, KW), lambda b, i: (b * nt + i, 0)),
        ],
        out_shape=[
            jax.ShapeDtypeStruct((B, 8, lp), F32),
            jax.ShapeDtypeStruct((T, KW), F32),
        ],
        scratch_shapes=[pltpu.VMEM((1, 128), F32)],
        compiler_params=_params(2),
        name="gate_prep",
    )(hb, w_small, b_small, wa_ext, ba)


def _fox_kernel(q_ref, k_ref, v_ref, crow_ref, o_ref, *, heads, tq, tk, group):
    qi = pl.program_id(1)
    n_full = (qi * tq) // tk
    kd = pl.multiple_of(n_full * tk, tk)
    rows = qi * tq + lax.broadcasted_iota(jnp.int32, (tq, tk), 0)
    cols = kd + lax.broadcasted_iota(jnp.int32, (tq, tk), 1)
    causal = cols <= rows
    ones = jnp.ones((tk, HEAD_DIM), BF16)

    def scores(h, k0):
        sl = slice(h * HEAD_DIM, (h + 1) * HEAD_DIM)
        cr = crow_ref[0, h:h + 1, pl.ds(k0, tk)]
        return _dot_nt(q_ref[:, sl], k_ref[pl.ds(k0, tk), sl]) - cr

    def weighted_values(h, p, k0):
        sl = slice(h * HEAD_DIM, (h + 1) * HEAD_DIM)
        va = jnp.concatenate([v_ref[pl.ds(k0, tk), sl], ones], axis=1)
        pv = jnp.dot(p, va, preferred_element_type=F32)
        return pv[:, :HEAD_DIM], pv[:, HEAD_DIM:HEAD_DIM + 1]

    for h0 in range(0, heads, group):
        hs = range(h0, min(h0 + group, heads))
        carry = []
        for h in hs:
            s = jnp.where(causal, scores(h, kd), -jnp.inf)
            m = jnp.max(s, axis=-1, keepdims=True)
            acc, l = weighted_values(h, jnp.exp2((s - m).astype(BF16)), kd)
            carry += [m, l, acc]

        def body(ki, carry):
            k0 = pl.multiple_of(ki * tk, tk)
            out = []
            for n, h in enumerate(hs):
                m, l, acc = carry[3 * n:3 * n + 3]
                s = scores(h, k0)
                m_new = jnp.maximum(m, jnp.max(s, axis=-1, keepdims=True))
                a = jnp.exp2(m - m_new)
                pv, p_sum = weighted_values(h, jnp.exp2((s - m_new).astype(BF16)), k0)
                out += [m_new, a * l + p_sum, a * acc + pv]
            return tuple(out)

        carry = lax.fori_loop(0, n_full, body, tuple(carry))
        for n, h in enumerate(hs):
            m, l, acc = carry[3 * n:3 * n + 3]
            o_ref[:, h * HEAD_DIM:(h + 1) * HEAD_DIM] = (acc / l).astype(o_ref.dtype)


def _fox(proj, crow, *, B, lp, fw, off_q, off_k, off_v, tq, tk):
    T = proj.shape[0]
    nq = lp // tq
    heads = fw // HEAD_DIM
    assert tk % tq == 0
    return pl.pallas_call(
        functools.partial(_fox_kernel, heads=heads, tq=tq, tk=tk, group=FOX_HEAD_GROUP),
        grid=(B, nq),
        in_specs=[
            pl.BlockSpec((tq, fw), lambda b, i: (b * nq + i, off_q // fw)),
            pl.BlockSpec((lp, fw), lambda b, i: (b, off_k // fw)),
            pl.BlockSpec((lp, fw), lambda b, i: (b, off_v // fw)),
            pl.BlockSpec((1, 8, lp), lambda b, i: (b, 0, 0)),
        ],
        out_specs=pl.BlockSpec((tq, fw), lambda b, i: (b * nq + i, 0)),
        out_shape=jax.ShapeDtypeStruct((T, fw), BF16),
        compiler_params=_params(2),
        name="fox_attention",
    )(proj, proj, proj, crow)


def _lru_kernel(lx_ref, lg_ref, cw_ref, cb_ref, wr_ref, br_ref, wi_ref, bi_ref, lam_ref, o_ref,
                ext_ref, a_ref, u_ref, hc_ref, *, pad, tt, nblk):
    i = pl.program_id(1)

    @pl.when(i == 0)
    def _():
        ext_ref[0:CONV_HALO, :] = jnp.zeros((CONV_HALO, ext_ref.shape[1]), F32)
        hc_ref[...] = jnp.zeros_like(hc_ref)

    @pl.when(i > 0)
    def _():
        ext_ref[0:CONV_HALO, :] = ext_ref[tt:tt + CONV_HALO, :]

    pos = i * tt + lax.broadcasted_iota(jnp.int32, (tt, 1), 0)
    valid = pos >= pad
    ext_ref[CONV_HALO:CONV_HALO + tt, :] = jnp.where(valid, lx_ref[...].astype(F32), 0.0)

    sp = jnp.maximum(-lam_ref[...], 0.0) + jnp.log1p(jnp.exp(-jnp.abs(lam_ref[...])))
    for n in range(nblk):
        sl = slice(n * HEAD_DIM, (n + 1) * HEAD_DIM)
        xc = cb_ref[:, sl]
        for j in range(CONV_W):
            start = CONV_HALO - (CONV_W - 1) + j
            xc = xc + cw_ref[j:j + 1, sl] * ext_ref[start:start + tt, sl]
        xb = xc.astype(BF16)
        r = _sigmoid(jnp.dot(xb, wr_ref[n].astype(BF16), preferred_element_type=F32) + br_ref[:, sl])
        g = _sigmoid(jnp.dot(xb, wi_ref[n].astype(BF16), preferred_element_type=F32) + bi_ref[:, sl])
        log_a = (-LRU_C) * r * sp[:, sl]
        a = jnp.exp(log_a)
        a_ref[:, sl] = a
        u = jnp.sqrt(-jnp.tanh(log_a) * (1.0 + a * a)) * (g * xc)
        u_ref[:, sl] = jnp.where(valid, u, 0.0)

    def body(gidx, h):
        r0 = pl.multiple_of(gidx * 8, 8)
        a8 = a_ref[pl.ds(r0, 8), :]
        u8 = u_ref[pl.ds(r0, 8), :]
        outs = []
        for r in range(8):
            h = a8[r:r + 1] * h + u8[r:r + 1]
            outs.append(h)
        u_ref[pl.ds(r0, 8), :] = jnp.concatenate(outs, axis=0)
        return h

    hc_ref[...] = lax.fori_loop(0, tt // 8, body, hc_ref[...])
    o_ref[...] = (u_ref[...] * _gelu_tanh(lg_ref[...].astype(F32))).astype(o_ref.dtype)


def _lru(proj, conv_w, conv_b, w_r, b_r, w_i, b_i, lam, *, B, lp, lw, off_x, off_g, pad, tt):
    T = proj.shape[0]
    nt = lp // tt
    nblk = lw // HEAD_DIM
    row = lambda b, i: (0, 0)
    return pl.pallas_call(
        functools.partial(_lru_kernel, pad=pad, tt=tt, nblk=nblk),
        grid=(B, nt),
        in_specs=[
            pl.BlockSpec((tt, lw), lambda b, i: (b * nt + i, off_x // lw)),
            pl.BlockSpec((tt, lw), lambda b, i: (b * nt + i, off_g // lw)),
            pl.BlockSpec((CONV_W, lw), row),
            pl.BlockSpec((1, lw), row),
            pl.BlockSpec((nblk, HEAD_DIM, HEAD_DIM), lambda b, i: (0, 0, 0)),
            pl.BlockSpec((1, lw), row),
            pl.BlockSpec((nblk, HEAD_DIM, HEAD_DIM), lambda b, i: (0, 0, 0)),
            pl.BlockSpec((1, lw), row),
            pl.BlockSpec((1, lw), row),
        ],
        out_specs=pl.BlockSpec((tt, lw), lambda b, i: (b * nt + i, 0)),
        out_shape=jax.ShapeDtypeStruct((T, lw), BF16),
        scratch_shapes=[
            pltpu.VMEM((tt + CONV_HALO, lw), F32),
            pltpu.VMEM((tt, lw), F32),
            pltpu.VMEM((tt, lw), F32),
            pltpu.VMEM((1, lw), F32),
        ],
        compiler_params=_params(2),
        name="conv_rglru",
    )(proj, proj, conv_w, conv_b, w_r, b_r, w_i, b_i, lam)


def _gla_kernel(q_ref, k_ref, v_ref, gg_ref, la_ref, ng_ref, o_ref, st_ref, *, pad, tt, heads):
    i = pl.program_id(1)

    @pl.when(i == 0)
    def _():
        st_ref[...] = jnp.zeros_like(st_ref)

    r = lax.broadcasted_iota(jnp.int32, (GLA_CHUNK, GLA_CHUNK), 0)
    c = lax.broadcasted_iota(jnp.int32, (GLA_CHUNK, GLA_CHUNK), 1)
    tri = (r >= c).astype(F32)
    for ci in range(tt // GLA_CHUNK):
        rows = slice(ci * GLA_CHUNK, (ci + 1) * GLA_CHUNK)
        pos = i * tt + ci * GLA_CHUNK + lax.broadcasted_iota(jnp.int32, (GLA_CHUNK, 1), 0)
        valid = pos >= pad
        for hd in range(heads):
            ks = slice(hd * HEAD_DIM, (hd + 1) * HEAD_DIM)
            vs = slice(hd * GLA_DV, (hd + 1) * GLA_DV)
            cs = jnp.dot(tri, la_ref[rows, ks], preferred_element_type=F32, precision=HIGHEST)
            cl = cs[GLA_CHUNK - 1:GLA_CHUNK]
            kdec = jnp.where(valid, k_ref[rows, ks].astype(F32) * jnp.exp(cl - cs), 0.0).astype(BF16)
            ut = lax.dot_general(v_ref[rows, vs], kdec, (((0,), (0,)), ((), ())), preferred_element_type=F32)
            st = st_ref[hd] * jnp.exp(cl) + ut
            st_ref[hd] = st
            o = _dot_nt(q_ref[rows, ks], st.astype(BF16))
            o = o * lax.rsqrt(jnp.mean(o * o, axis=-1, keepdims=True) + RMS_EPS) * ng_ref[:, vs]
            gg = gg_ref[rows, vs].astype(F32)
            o_ref[rows, vs] = (o * (gg * _sigmoid(gg))).astype(o_ref.dtype)


def _gla(proj, proj_g, la, norm_g, *, B, lp, kw, vw, off_q, off_k, off_v, off_g, pad, tt):
    T = proj.shape[0]
    nt = lp // tt
    heads = kw // HEAD_DIM
    return pl.pallas_call(
        functools.partial(_gla_kernel, pad=pad, tt=tt, heads=heads),
        grid=(B, nt),
        in_specs=[
            pl.BlockSpec((tt, kw), lambda b, i: (b * nt + i, off_q // kw)),
            pl.BlockSpec((tt, kw), lambda b, i: (b * nt + i, off_k // kw)),
            pl.BlockSpec((tt, vw), lambda b, i: (b * nt + i, off_v // vw)),
            pl.BlockSpec((tt, vw), lambda b, i: (b * nt + i, off_g // vw)),
            pl.BlockSpec((tt, kw), lambda b, i: (b * nt + i, 0)),
            pl.BlockSpec((1, vw), lambda b, i: (0, 0)),
        ],
        out_specs=pl.BlockSpec((tt, vw), lambda b, i: (b * nt + i, 0)),
        out_shape=jax.ShapeDtypeStruct((T, vw), BF16),
        scratch_shapes=[pltpu.VMEM((heads, GLA_DV, HEAD_DIM), F32)],
        compiler_params=_params(2),
        name="gla_chunked",
    )(proj, proj, proj, proj_g, la, norm_g)


def _merge_kernel(of_ref, ol_ref, og_ref, w_ref, g0_ref, g1_ref, g2_ref, o_ref, wbf_ref, *, fw, lw):
    @pl.when(pl.program_id(1) == 0)
    def _():
        wbf_ref[...] = w_ref[...].astype(BF16)

    y0 = jnp.dot(of_ref[...], wbf_ref[0:fw, :], preferred_element_type=F32)
    y1 = jnp.dot(ol_ref[...], wbf_ref[fw:fw + lw, :], preferred_element_type=F32)
    y2 = jnp.dot(og_ref[...], wbf_ref[fw + lw:, :], preferred_element_type=F32)
    out = (_sigmoid(g0_ref[...].astype(F32)) * y0 + _sigmoid(g1_ref[...].astype(F32)) * y1
           + _sigmoid(g2_ref[...].astype(F32)) * y2)
    o_ref[...] = out.astype(o_ref.dtype)


def _merge(o_fox, o_lru, o_gla, w_branch, layer, proj, *, off_gates, d):
    T, fw = o_fox.shape
    lw = o_lru.shape[1]
    vw = o_gla.shape[1]
    tm = _pick(T, (1056, 1024, 768, 512, 384, 256, 128))
    tn = _pick(math.gcd(d, off_gates), (512, 256, 128))
    gate_spec = lambda b: pl.BlockSpec((tm, tn), lambda j, i: (i, (off_gates + b * d) // tn + j))
    return pl.pallas_call(
        functools.partial(_merge_kernel, fw=fw, lw=lw),
        grid=(d // tn, T // tm),
        in_specs=[
            pl.BlockSpec((tm, fw), lambda j, i: (i, 0)),
            pl.BlockSpec((tm, lw), lambda j, i: (i, 0)),
            pl.BlockSpec((tm, vw), lambda j, i: (i, 0)),
            pl.BlockSpec((None, fw + lw + vw, tn), lambda j, i: (layer, 0, j)),
            gate_spec(0), gate_spec(1), gate_spec(2),
        ],
        out_specs=pl.BlockSpec((tm, tn), lambda j, i: (i, j)),
        out_shape=jax.ShapeDtypeStruct((T, d), BF16),
        scratch_shapes=[pltpu.VMEM((fw + lw + vw, tn), BF16)],
        compiler_params=_params(2),
        name="branch_merge",
    )(o_fox, o_lru, o_gla, w_branch, proj, proj, proj)


def _ln_router_kernel(h_ref, mix_ref, g_ref, b_ref, wr_ref, br_ref, hf_ref, hx_ref, idx_ref, wt_ref, *, alpha):
    y = _layer_norm(alpha * h_ref[...] + mix_ref[...], g_ref[...], b_ref[...])
    hf_ref[...] = y
    hx_ref[...] = _pack_halves(y)
    w = wr_ref[...]
    y_hi, w_hi = y.astype(BF16), w.astype(BF16)
    y_lo = (y - y_hi.astype(F32)).astype(BF16)
    w_lo = (w - w_hi.astype(F32)).astype(BF16)
    logits = (jnp.dot(y_hi, w_hi, preferred_element_type=F32)
              + (jnp.dot(y_lo, w_hi, preferred_element_type=F32) + jnp.dot(y_hi, w_lo, preferred_element_type=F32))
              + br_ref[...])
    lane = lax.broadcasted_iota(jnp.int32, logits.shape, 1).astype(F32)
    idx = jnp.zeros_like(logits)
    vals = []
    for k in range(TOP_K):
        mx = jnp.max(logits, axis=-1, keepdims=True)
        sel = jnp.min(jnp.where(logits == mx, lane, float(logits.shape[1])), axis=-1, keepdims=True)
        vals.append(mx)
        idx = jnp.where(lane == float(k), sel, idx)
        logits = jnp.where(lane == sel, -jnp.inf, logits)
    es = [jnp.exp(v - vals[0]) for v in vals]
    tot = es[0]
    for e in es[1:]:
        tot = tot + e
    wt = jnp.zeros_like(logits)
    for k in range(TOP_K):
        wt = jnp.where(lane == float(k), es[k] / tot, wt)
    idx_ref[...] = idx.astype(jnp.int32)
    wt_ref[...] = wt


def _ln_router(h, mix, g, b, wr_pad, br_pad, *, alpha):
    T, D = h.shape
    tm = _pick(T, (192, 128))
    blk = pl.BlockSpec((tm, D), lambda i: (i, 0))
    row = pl.BlockSpec((1, D), lambda i: (0, 0))
    small = pl.BlockSpec((tm, 128), lambda i: (i, 0))
    return pl.pallas_call(
        functools.partial(_ln_router_kernel, alpha=alpha),
        grid=(T // tm,),
        in_specs=[blk, blk, row, row, pl.BlockSpec((D, 128), lambda i: (0, 0)), pl.BlockSpec((1, 128), lambda i: (0, 0))],
        out_specs=[blk, pl.BlockSpec((tm, D // 2), lambda i: (i, 0)), small, small],
        out_shape=[
            jax.ShapeDtypeStruct((T, D), F32),
            jax.ShapeDtypeStruct((T, D // 2), U32),
            jax.ShapeDtypeStruct((T, 128), jnp.int32),
            jax.ShapeDtypeStruct((T, 128), F32),
        ],
        compiler_params=_params(1),
        name="ln_router",
    )(h, mix, g, b, wr_pad, br_pad)


def _pack_halves(y):
    half = y.shape[1] // 2
    lo = pltpu.bitcast(y[:, :half].astype(BF16).astype(F32), U32)
    hi = pltpu.bitcast(y[:, half:].astype(BF16).astype(F32), U32)
    return lax.shift_right_logical(lo, jnp.uint32(16)) | (hi & jnp.uint32(0xFFFF0000))


def _unpack_halves(w):
    lo = pltpu.bitcast(lax.shift_left(w, jnp.uint32(16)), F32)
    hi = pltpu.bitcast(w & jnp.uint32(0xFFFF0000), F32)
    return lo, hi


def _row_gather(src_hbm, idx_ref, base, buf_ref, sem, n, static=False):
    def start(r):
        pltpu.make_async_copy(src_hbm.at[pl.ds(idx_ref[base + r], 1)], buf_ref.at[pl.ds(r, 1)], sem).start()

    if static:
        for r in range(n):
            start(r)
    else:
        def body(r, carry):
            start(r)
            return carry

        lax.fori_loop(0, n, body, 0, unroll=8)


def _tile_wait(src_hbm, buf_ref, sem):
    pltpu.make_async_copy(src_hbm.at[pl.ds(0, buf_ref.shape[0])], buf_ref, sem).wait()


def _moe_kernel(te_ref, na_ref, first_ref, nxt_ref, tok_ref, h_hbm, wup_hbm, wdn_hbm, bup_ref, sel_ref, bdn_ref,
                y_ref, xbuf_ref, sem_ref, sup_ref, sdn_ref, wup_ref, wdn_ref, wsem_ref, *, tmx, layer):
    i = pl.program_id(0)
    na = na_ref[0]
    cur = lax.rem(i, GATHER_RING)
    ahead = lax.rem(i + GATHER_RING - 1, GATHER_RING)

    def weight_copies(e):
        return (pltpu.make_async_copy(wup_hbm.at[layer, e], sup_ref, wsem_ref.at[0]),
                pltpu.make_async_copy(wdn_hbm.at[layer, e], sdn_ref, wsem_ref.at[1]))

    def start_weights(e):
        for c in weight_copies(e):
            c.start(priority=1)

    for t in range(GATHER_RING - 1):
        @pl.when(jnp.logical_and(i == 0, na > t))
        def _():
            _row_gather(h_hbm, tok_ref, t * tmx, xbuf_ref.at[t], sem_ref.at[t], tmx)

    @pl.when(jnp.logical_and(i == 0, na > 0))
    def _():
        start_weights(te_ref[0])

    @pl.when(jnp.logical_and(i < na, first_ref[i] == 1))
    def _():
        up_copy, dn_copy = weight_copies(te_ref[i])
        up_copy.wait()
        _cast_rows(sup_ref, wup_ref, 256)
        dn_copy.wait()
        _cast_rows(sdn_ref, wdn_ref, 128)

        @pl.when(nxt_ref[i] >= 0)
        def _():
            start_weights(nxt_ref[i])

    @pl.when(i < na)
    def _():
        _tile_wait(h_hbm, xbuf_ref.at[cur], sem_ref.at[cur])
        lo, hi = _unpack_halves(xbuf_ref[cur])
        x = jnp.concatenate([lo.astype(BF16), hi.astype(BF16)], axis=1)
        h = jnp.dot(x, wup_ref[...], preferred_element_type=F32) + bup_ref[...]
        g = jnp.minimum(h, SWIGLU_LIMIT)
        u = jnp.clip(pltpu.roll(h, h.shape[1] - 1, axis=1), -SWIGLU_LIMIT, SWIGLU_LIMIT)
        act = ((u + 1.0) * g * _sigmoid(SWIGLU_ALPHA * g)).astype(BF16)
        act = jnp.dot(act, sel_ref[...], preferred_element_type=F32).astype(BF16)
        y = jnp.dot(act, wdn_ref[...], preferred_element_type=F32) + bdn_ref[...]
        y_ref[...] = _pack_halves(y)

    @pl.when(i + GATHER_RING - 1 < na)
    def _():
        _row_gather(h_hbm, tok_ref, (i + GATHER_RING - 1) * tmx, xbuf_ref.at[ahead], sem_ref.at[ahead], tmx,
                    static=True)

    @pl.when(i >= na)
    def _():
        y_ref[...] = jnp.zeros_like(y_ref)


def _moe(tile_expert, n_active, row_token, hx, w_up, b_up, sel, w_down, b_down, *, tmx, layer):
    T, D = hx.shape[0], 2 * hx.shape[1]
    _, E, _, F2 = w_up.shape
    F = F2 // 2
    P = row_token.shape[0]
    ntiles = P // tmx
    tiles = jnp.arange(ntiles, dtype=jnp.int32)
    first = jnp.where(tiles == 0, 1, (tile_expert != jnp.roll(tile_expert, 1)).astype(jnp.int32))
    j = jnp.searchsorted(tile_expert, tile_expert, side="right").astype(jnp.int32)
    nxt = jnp.where(j < ntiles, tile_expert[jnp.minimum(j, ntiles - 1)], -1).astype(jnp.int32)

    return pl.pallas_call(
        functools.partial(_moe_kernel, tmx=tmx, layer=layer),
        grid_spec=pltpu.PrefetchScalarGridSpec(
            num_scalar_prefetch=5,
            grid=(ntiles,),
            in_specs=[
                pl.BlockSpec(memory_space=pl.ANY),
                pl.BlockSpec(memory_space=pl.ANY),
                pl.BlockSpec(memory_space=pl.ANY),
                pl.BlockSpec((None, 1, F2), lambda i, te, *_: (te[i], 0, 0)),
                pl.BlockSpec((F2, F), lambda i, *_: (0, 0)),
                pl.BlockSpec((None, 1, D), lambda i, te, *_: (te[i], 0, 0)),
            ],
            out_specs=pl.BlockSpec((tmx, D // 2), lambda i, *_: (i, 0)),
            scratch_shapes=[
                pltpu.VMEM((GATHER_RING, tmx, D // 2), U32), pltpu.SemaphoreType.DMA((GATHER_RING,)),
                pltpu.VMEM((D, F2), F32), pltpu.VMEM((F, D), F32),
                pltpu.VMEM((D, F2), BF16), pltpu.VMEM((F, D), BF16), pltpu.SemaphoreType.DMA((2,)),
            ],
        ),
        out_shape=jax.ShapeDtypeStruct((P, D // 2), U32),
        compiler_params=_params(1),
        name="moe_experts",
    )(tile_expert, n_active, first, nxt, row_token, hx, w_up, w_down, b_up, sel, b_down)


def _combine_kernel(pos_ref, y_hbm, h_ref, w_ref, g_ref, b_ref, *rest, alpha, tc, nt, final):
    if final:
        out_ref, buf_ref, sem_ref = rest
    else:
        hf_ref, hb_ref, buf_ref, sem_ref = rest
    i = pl.program_id(0)
    cur = lax.rem(i, GATHER_RING)
    ahead = lax.rem(i + GATHER_RING - 1, GATHER_RING)

    def start(tile, slot, static):
        for k in range(TOP_K):
            _row_gather(y_hbm, pos_ref, (k * nt + tile) * tc, buf_ref.at[slot, k], sem_ref.at[slot], tc, static=static)

    @pl.when(i == 0)
    def _():
        for t in range(min(GATHER_RING - 1, nt)):
            start(t, t, False)

    for k in range(TOP_K):
        _tile_wait(y_hbm, buf_ref.at[cur, k], sem_ref.at[cur])
    lo = hi = None
    for k in range(TOP_K):
        l, h = _unpack_halves(buf_ref[cur, k])
        wk = w_ref[:, k:k + 1]
        lo = wk * l if lo is None else lo + wk * l
        hi = wk * h if hi is None else hi + wk * h
    ffn = jnp.concatenate([lo, hi], axis=1)
    y = _layer_norm(alpha * h_ref[...] + ffn, g_ref[...], b_ref[...])
    if final:
        out_ref[...] = y
    else:
        hf_ref[...] = y
        hb_ref[...] = y.astype(BF16)

    @pl.when(i + GATHER_RING - 1 < nt)
    def _():
        start(i + GATHER_RING - 1, ahead, True)


def _combine(pos, ys, hf, top_w, g, b, *, alpha, lp, final):
    T, D = hf.shape
    tc = ROW_TILE
    nt = T // tc
    ntb = lp // tc
    if final:
        out_specs = pl.BlockSpec((tc, D), lambda i, pos: ((i // ntb) * (ntb - 1) + jnp.maximum(i % ntb - 1, 0), 0))
        out_shape = jax.ShapeDtypeStruct(((T // lp) * (lp - tc), D), F32)
    else:
        out_specs = [pl.BlockSpec((tc, D), lambda i, pos: (i, 0)), pl.BlockSpec((tc, D), lambda i, pos: (i, 0))]
        out_shape = [jax.ShapeDtypeStruct((T, D), F32), jax.ShapeDtypeStruct((T, D), BF16)]
    grid_spec = pltpu.PrefetchScalarGridSpec(
        num_scalar_prefetch=1,
        grid=(nt,),
        in_specs=[
            pl.BlockSpec(memory_space=pl.ANY),
            pl.BlockSpec((tc, D), lambda i, pos: (i, 0)),
            pl.BlockSpec((tc, 128), lambda i, pos: (i, 0)),
            pl.BlockSpec((1, D), lambda i, pos: (0, 0)),
            pl.BlockSpec((1, D), lambda i, pos: (0, 0)),
        ],
        out_specs=out_specs,
        scratch_shapes=[pltpu.VMEM((GATHER_RING, TOP_K, tc, D // 2), U32), pltpu.SemaphoreType.DMA((GATHER_RING,))],
    )
    return pl.pallas_call(
        functools.partial(_combine_kernel, alpha=alpha, tc=tc, nt=nt, final=final),
        grid_spec=grid_spec,
        out_shape=out_shape,
        compiler_params=_params(1),
        name="moe_combine_ln",
    )(pos, ys, hf, top_w, g, b)


def _route_kernel(idx_ref, pos_ref, te_ref, na_ref, cnt_ref, pst_ref, *, pad, tr, tmx, n_experts, spare_row):
    p = pl.program_id(0)
    i = pl.program_id(2)
    first = jnp.logical_and(pl.program_id(1) == 0, i == 0)

    @pl.when(jnp.logical_and(p == 0, first))
    def _():
        cnt_ref[...] = jnp.zeros_like(cnt_ref)

    @pl.when(jnp.logical_and(p == 1, first))
    def _():
        cnt = cnt_ref[...]
        padded = jnp.floor((cnt + (tmx - 1.0)) * (1.0 / tmx)) * tmx
        r = lax.broadcasted_iota(jnp.int32, (128, 128), 0)
        c = lax.broadcasted_iota(jnp.int32, (128, 128), 1)
        pst = jnp.dot(padded, (r < c).astype(F32), preferred_element_type=F32, precision=HIGHEST)
        pst_ref[...] = pst
        cnt_ref[...] = jnp.zeros_like(cnt_ref)
        tile_end = (pst + padded) * (1.0 / tmx)
        nt_pad = te_ref.shape[0]
        t = lax.broadcasted_iota(jnp.int32, (nt_pad, 128), 0).astype(F32)
        lane = lax.broadcasted_iota(jnp.int32, (nt_pad, 128), 1)
        is_expert = lane < n_experts
        te = jnp.sum(jnp.where(jnp.logical_and(tile_end <= t, is_expert), 1.0, 0.0), axis=-1, keepdims=True)
        lane1 = lax.broadcasted_iota(jnp.int32, (1, 128), 1).astype(F32)
        e_last = jnp.max(jnp.where(cnt > 0.0, lane1, 0.0), axis=-1, keepdims=True)
        te_ref[...] = jnp.broadcast_to(jnp.minimum(te, e_last), (nt_pad, 128)).astype(jnp.int32)
        na_ref[...] = jnp.broadcast_to(jnp.max(tile_end, axis=-1, keepdims=True), (1, 128)).astype(jnp.int32)

    idx = idx_ref[...]
    lane = lax.broadcasted_iota(jnp.int32, (tr, 128), 1)
    valid = (i * tr + lax.broadcasted_iota(jnp.int32, (tr, 1), 0)) >= pad
    onehot = jnp.zeros((tr, 128), F32)
    for k in range(TOP_K):
        onehot = onehot + jnp.where(lane == idx[:, k:k + 1], 1.0, 0.0)
    onehot = jnp.where(valid, onehot, 0.0)
    r = lax.broadcasted_iota(jnp.int32, (tr, tr), 0)
    c = lax.broadcasted_iota(jnp.int32, (tr, tr), 1)
    rank = jnp.dot((r > c).astype(BF16), onehot.astype(BF16), preferred_element_type=F32) + cnt_ref[...]
    cnt_ref[...] = cnt_ref[...] + jnp.sum(onehot, axis=0, keepdims=True)

    @pl.when(p == 0)
    def _():
        pos_ref[...] = jnp.zeros_like(pos_ref)

    @pl.when(p == 1)
    def _():
        dest = rank + pst_ref[...]
        out = jnp.zeros((tr, 128), F32)
        for k in range(TOP_K):
            d = jnp.sum(jnp.where(lane == idx[:, k:k + 1], dest, 0.0), axis=-1, keepdims=True)
            out = jnp.where(lane == k, jnp.where(valid, d, float(spare_row)), out)
        pos_ref[...] = out.astype(jnp.int32)


def _route(top_idx, *, B, lp, pad, tmx, n_experts, n_rows):
    T = top_idx.shape[0]
    tr = _pick(lp, (384, 256, 128))
    nt = lp // tr
    nt_pad = -(-(n_rows // tmx) // 8) * 8
    pos, te, na = pl.pallas_call(
        functools.partial(_route_kernel, pad=pad, tr=tr, tmx=tmx, n_experts=n_experts, spare_row=n_rows - 1),
        grid=(2, B, nt),
        in_specs=[pl.BlockSpec((tr, 128), lambda p, b, i: (b * nt + i, 0))],
        out_specs=[
            pl.BlockSpec((tr, 128), lambda p, b, i: (p * (b * nt + i), 0)),
            pl.BlockSpec((nt_pad, 128), lambda p, b, i: (0, 0)),
            pl.BlockSpec((1, 128), lambda p, b, i: (0, 0)),
        ],
        out_shape=[
            jax.ShapeDtypeStruct((T, 128), jnp.int32),
            jax.ShapeDtypeStruct((nt_pad, 128), jnp.int32),
            jax.ShapeDtypeStruct((1, 128), jnp.int32),
        ],
        scratch_shapes=[pltpu.VMEM((1, 128), F32), pltpu.VMEM((1, 128), F32)],
        compiler_params=_params(3),
        name="route_rank",
    )(top_idx)
    pos_flat = pos[:, :TOP_K].T.reshape(-1)
    return pos_flat, te[:n_rows // tmx, 0], na[0, :1]


def _row_token_kernel(pos_ref, rt_ref, *, t):
    def zero(r, carry):
        rt_ref[r] = 0
        return carry

    lax.fori_loop(0, rt_ref.shape[0], zero, 0, unroll=8)
    for k in range(TOP_K):
        def body(tok, carry):
            rt_ref[pos_ref[k * t + tok]] = tok
            return carry

        lax.fori_loop(0, t, body, 0, unroll=8)


def _row_token(pos_flat, *, t, n_rows):
    return pl.pallas_call(
        functools.partial(_row_token_kernel, t=t),
        grid_spec=pltpu.PrefetchScalarGridSpec(
            num_scalar_prefetch=1, grid=(1,), in_specs=[],
            out_specs=pl.BlockSpec(memory_space=pltpu.SMEM)),
        out_shape=jax.ShapeDtypeStruct((n_rows,), jnp.int32),
        compiler_params=_params(1),
        name="route_row_token",
    )(pos_flat)


def _segments(fw, fh, lw, kw, vw, rank, d):
    names = ("fq", "fk", "fv", "ff", "lx", "lg", "gq", "gk", "gv", "ga", "gg", "gate0", "gate1", "gate2")
    widths = (fw, fw, fw, fh, lw, lw, kw, kw, vw, rank, vw, d, d, d)
    segs, off = {}, 0
    for n, w in zip(names, widths):
        segs[n] = (off, w)
        off += w
    return segs, off


def kernel(x, meta_tokens, emb_ln_g, emb_ln_b, w_in, b_in, conv_w, conv_b, lru_w_r, lru_b_r, lru_w_i, lru_b_i,
           lru_lambda, gla_w_alpha, gla_b_alpha, gla_norm_g, w_branch, w_out, b_out, ln1_g, ln1_b, w_router,
           b_router, w_up, b_up, w_down, b_down, ln2_g, ln2_b):
    B, S, D = x.shape
    n_meta = meta_tokens.shape[0]
    depth = w_in.shape[0]
    L = S + n_meta
    pad = (-L) % ROW_TILE
    lp = L + pad
    assert pad + n_meta == ROW_TILE and S % ROW_TILE == 0
    T = B * lp

    lw = conv_w.shape[2]
    rank, kw = gla_w_alpha.shape[1:]
    vw = gla_norm_g.shape[1]
    n_experts = w_router.shape[2]
    fexp = w_up.shape[3] // 2
    in_cols = w_in.shape[2]
    fh = (in_cols - 2 * lw - 2 * kw - 2 * vw - rank - 3 * D) // (3 * HEAD_DIM + 1)
    fw = fh * HEAD_DIM
    segs, total = _segments(fw, fh, lw, kw, vw, rank, D)
    assert total == in_cols and fh <= 8 and fh + rank <= 128
    alpha = (2.0 * depth) ** 0.25

    windows = (("fq", "fk", "fv"), ("lx", "lg", "gq", "gk", "gv"), ("gg", "gate0", "gate1", "gate2"))
    qscale = HEAD_DIM ** -0.5
    off, win_start, win_scale = {}, [], []
    for names in windows:
        start = segs[names[0]][0]
        for n in names:
            off[n] = segs[n][0] - start
            assert off[n] % segs[n][1] == 0 or n.startswith("gate")
        win_start.append(start)
        scales = {"fq": qscale * LOG2E, "gq": qscale}
        win_scale.append(jnp.concatenate(
            [jnp.full((segs[n][1],), scales.get(n, 1.0), F32) for n in names]).reshape(1, -1))

    def cols(a, n):
        o, w = segs[n]
        return a[..., o:o + w]

    tt = _pick(lp, (384, 256, 128))
    tmx = MOE_ROW_TILE if T >= 4096 else 128
    n_rows = (-(-(B * L * TOP_K + n_experts * (tmx - 1)) // tmx) + 1) * tmx

    head = jnp.concatenate([jnp.zeros((pad, D), F32), meta_tokens.astype(F32)], axis=0)
    hf, hb = _embed_ln(x, head, emb_ln_g.reshape(1, D), emb_ln_b.reshape(1, D), pad=pad, lp=lp)
    sel = (jnp.arange(2 * fexp)[:, None] == 2 * jnp.arange(fexp)[None, :]).astype(BF16)

    w_in_t = jnp.swapaxes(w_in, 1, 2)

    def rows(l, n):
        o, w = segs[n]
        return w_in_t[l, o:o + w]

    for l in range(depth):
        w_small = jnp.concatenate(
            [rows(l, "ff"), jnp.zeros((8 - fh, D), F32), rows(l, "ga"), jnp.zeros((128 - 8 - rank, D), F32)], axis=0)
        b_small = jnp.concatenate(
            [cols(b_in[l], "ff"), jnp.zeros((8 - fh,), F32), cols(b_in[l], "ga"),
             jnp.zeros((128 - 8 - rank,), F32)]).reshape(1, 128)
        wa_ext = jnp.zeros((128, kw), F32).at[8:8 + rank].set(gla_w_alpha[l])

        pa, pb, pc = [
            _in_proj(hb, w_in_t, l, b_in[l][s:s + sc.shape[1]].reshape(1, -1), sc, start=s, name=f"in_proj_{k}")
            for k, (s, sc) in enumerate(zip(win_start, win_scale))]
        crow, la = _prep(hb, w_small, b_small, wa_ext, gla_b_alpha[l].reshape(1, kw), B=B, lp=lp, pad=pad, tt=tt)
        o_fox = _fox(pa, crow, B=B, lp=lp, fw=fw, off_q=off["fq"], off_k=off["fk"], off_v=off["fv"], tq=FOX_Q_TILE,
                     tk=tt)
        o_lru = _lru(pb, conv_w[l], conv_b[l].reshape(1, lw), lru_w_r[l], lru_b_r[l].reshape(1, lw), lru_w_i[l],
                     lru_b_i[l].reshape(1, lw), lru_lambda[l].reshape(1, lw), B=B, lp=lp, lw=lw, off_x=off["lx"],
                     off_g=off["lg"], pad=pad, tt=tt)
        o_gla = _gla(pb, pc, la, gla_norm_g[l].reshape(1, vw), B=B, lp=lp, kw=kw, vw=vw, off_q=off["gq"],
                     off_k=off["gk"], off_v=off["gv"], off_g=off["gg"], pad=pad, tt=tt)
        merged = _merge(o_fox, o_lru, o_gla, w_branch, l, pc, off_gates=off["gate0"], d=D)
        mix = _matmul(merged, w_out, l, b_out[l].reshape(1, D), jnp.ones((1, D), F32), F32, "out_proj")

        wr_pad = jnp.zeros((D, 128), F32).at[:, :n_experts].set(w_router[l])
        br_pad = jnp.full((1, 128), -MASKED_KEY_BIAS, F32).at[0, :n_experts].set(b_router[l])
        hf, hx, top_idx, top_w = _ln_router(hf, mix, ln1_g[l].reshape(1, D), ln1_b[l].reshape(1, D), wr_pad, br_pad,
                                            alpha=alpha)
        pos, tile_expert, n_active = _route(top_idx, B=B, lp=lp, pad=pad, tmx=tmx, n_experts=n_experts, n_rows=n_rows)
        row_token = _row_token(pos, t=T, n_rows=n_rows)
        ys = _moe(tile_expert, n_active, row_token, hx, w_up, b_up[l].reshape(n_experts, 1, 2 * fexp), sel, w_down,
                  b_down[l].reshape(n_experts, 1, D), tmx=tmx, layer=l)
        final = l == depth - 1
        res = _combine(pos, ys, hf, top_w, ln2_g[l].reshape(1, D), ln2_b[l].reshape(1, D), alpha=alpha, lp=lp,
                       final=final)
        if final:
            return res.reshape(B, S, D)
        hf, hb = res
```

```python
import functools
import math

import jax
import jax.numpy as jnp
from jax import lax
from jax.experimental import pallas as pl
from jax.experimental.pallas import tpu as pltpu

F32 = jnp.float32
BF16 = jnp.bfloat16
U32 = jnp.uint32
HIGHEST = lax.Precision.HIGHEST

ROW_TILE = 128
HEAD_DIM = 128
GLA_DV = 256
GLA_CHUNK = 64
CONV_W = 4
CONV_HALO = 8
LN_EPS = 1e-5
RMS_EPS = 1e-6
LRU_C = 8.0
GLA_TAU = 16.0
SWIGLU_LIMIT = 7.0
SWIGLU_ALPHA = 1.702
TOP_K = 4
LOG2E = 1.4426950408889634
MASKED_KEY_BIAS = 1e30
MOE_ROW_TILE = 256
FOX_HEAD_GROUP = 4
FOX_Q_TILE = 384
GATHER_RING = 4
VMEM_LIMIT = 56 * 1024 * 1024


def _pick(n, candidates):
    for c in candidates:
        if n % c == 0:
            return c
    raise ValueError(f"no tile in {candidates} divides {n}")


def _params(n_axes, vmem=VMEM_LIMIT):
    return pltpu.CompilerParams(dimension_semantics=("arbitrary",) * n_axes, vmem_limit_bytes=vmem)


def _layer_norm(v, g, b):
    mu = jnp.mean(v, axis=-1, keepdims=True)
    d = v - mu
    var = jnp.mean(d * d, axis=-1, keepdims=True)
    return d * lax.rsqrt(var + LN_EPS) * g + b


def _log_sigmoid(x):
    return jnp.minimum(x, 0.0) - jnp.log1p(jnp.exp(-jnp.abs(x)))


def _sigmoid(x):
    return 0.5 * jnp.tanh(0.5 * x) + 0.5


def _gelu_tanh(x):
    return 0.5 * x * (1.0 + jnp.tanh(0.7978845608028654 * (x + 0.044715 * (x * x * x))))


def _embed_ln_kernel(x_ref, head_ref, g_ref, b_ref, hf_ref, hb_ref, *, pad):
    i = pl.program_id(1)

    @pl.when(i == 0)
    def _():
        y = _layer_norm(head_ref[...], g_ref[...], b_ref[...])
        rows = lax.broadcasted_iota(jnp.int32, (ROW_TILE, 1), 0)
        y = jnp.where(rows >= pad, y, 0.0)
        hf_ref[...] = y
        hb_ref[...] = y.astype(BF16)

    @pl.when(i > 0)
    def _():
        y = _layer_norm(x_ref[0], g_ref[...], b_ref[...])
        hf_ref[...] = y
        hb_ref[...] = y.astype(BF16)


def _embed_ln(x, head, g, b, *, pad, lp):
    B, S, D = x.shape
    nt = lp // ROW_TILE
    T = B * lp
    return pl.pallas_call(
        functools.partial(_embed_ln_kernel, pad=pad),
        grid=(B, nt),
        in_specs=[
            pl.BlockSpec((1, ROW_TILE, D), lambda b, i: (b, jnp.maximum(i - 1, 0), 0)),
            pl.BlockSpec((ROW_TILE, D), lambda b, i: (0, 0)),
            pl.BlockSpec((1, D), lambda b, i: (0, 0)),
            pl.BlockSpec((1, D), lambda b, i: (0, 0)),
        ],
        out_specs=[
            pl.BlockSpec((ROW_TILE, D), lambda b, i: (b * nt + i, 0)),
            pl.BlockSpec((ROW_TILE, D), lambda b, i: (b * nt + i, 0)),
        ],
        out_shape=[jax.ShapeDtypeStruct((T, D), F32), jax.ShapeDtypeStruct((T, D), BF16)],
        compiler_params=_params(2),
        name="embed_ln",
    )(x, head, g, b)


def _dot_nt(a, b):
    return lax.dot_general(a, b, (((1,), (1,)), ((), ())), preferred_element_type=F32)


def _cast_rows(src_ref, dst_ref, chunk):
    def body(c, carry):
        r0 = pl.multiple_of(c * chunk, chunk)
        dst_ref[pl.ds(r0, chunk), :] = src_ref[pl.ds(r0, chunk), :].astype(BF16)
        return carry

    lax.fori_loop(0, src_ref.shape[0] // chunk, body, 0)


def _mm_kernel(a_ref, w_ref, b_ref, s_ref, o_ref, *scratch):
    if scratch:
        (wbf_ref,) = scratch

        @pl.when(pl.program_id(1) == 0)
        def _():
            wbf_ref[...] = w_ref[...].astype(BF16)

        w = wbf_ref[...]
    else:
        w = w_ref[...]
    acc = jnp.dot(a_ref[...], w, preferred_element_type=F32)
    o_ref[...] = ((acc + b_ref[...]) * s_ref[...]).astype(o_ref.dtype)


def _matmul(a, w, layer, bias, scale, out_dtype, name):
    M, K = a.shape
    N = w.shape[2]
    tm = _pick(M, (1056, 1024, 768, 512, 384, 256, 128))
    tn = _pick(N, (512, 256, 128))
    scratch = [pltpu.VMEM((K, tn), BF16)] if w.dtype != BF16 else []
    return pl.pallas_call(
        _mm_kernel,
        grid=(N // tn, M // tm),
        in_specs=[
            pl.BlockSpec((tm, K), lambda j, i: (i, 0)),
            pl.BlockSpec((None, K, tn), lambda j, i: (layer, 0, j)),
            pl.BlockSpec((1, tn), lambda j, i: (0, j)),
            pl.BlockSpec((1, tn), lambda j, i: (0, j)),
        ],
        out_specs=pl.BlockSpec((tm, tn), lambda j, i: (i, j)),
        out_shape=jax.ShapeDtypeStruct((M, N), out_dtype),
        scratch_shapes=scratch,
        compiler_params=_params(2),
        name=name,
    )(a, w, bias, scale)


def _mm_nt_kernel(a_ref, wt_ref, b_ref, s_ref, o_ref, wbf_ref):
    @pl.when(pl.program_id(1) == 0)
    def _():
        _cast_rows(wt_ref.at[0], wbf_ref, 64)

    acc = _dot_nt(a_ref[...], wbf_ref[...])
    o_ref[...] = ((acc + b_ref[...]) * s_ref[...]).astype(o_ref.dtype)


def _in_proj(a, wt, layer, bias, scale, *, start, name):
    M, K = a.shape
    n = bias.shape[1]
    assert start % 8 == 0
    tm = _pick(M, (1056, 1024, 768, 512, 384, 256, 128))
    tn = _pick(n, (512, 256, 128))
    return pl.pallas_call(
        _mm_nt_kernel,
        grid=(n // tn, M // tm),
        in_specs=[
            pl.BlockSpec((tm, K), lambda j, i: (i, 0)),
            pl.BlockSpec((pl.Element(1), pl.Element(tn), pl.Element(K)),
                         lambda j, i: (layer, 8 * (start // 8 + j * (tn // 8)), 0)),
            pl.BlockSpec((1, tn), lambda j, i: (0, j)),
            pl.BlockSpec((1, tn), lambda j, i: (0, j)),
        ],
        out_specs=pl.BlockSpec((tm, tn), lambda j, i: (i, j)),
        out_shape=jax.ShapeDtypeStruct((M, n), BF16),
        scratch_shapes=[pltpu.VMEM((tn, K), BF16)],
        compiler_params=_params(2),
        name=name,
    )(a, wt, bias, scale)


def _prep_kernel(hb_ref, ws_ref, bs_ref, wa_ref, ba_ref, crow_ref, la_ref, carry_ref, *, pad, tt):
    i = pl.program_id(1)

    @pl.when(i == 0)
    def _():
        carry_ref[...] = jnp.zeros_like(carry_ref)

    z = _dot_nt(hb_ref[...], ws_ref[...].astype(BF16)) + bs_ref[...]
    pos = i * tt + lax.broadcasted_iota(jnp.int32, (tt, 1), 0)
    valid = pos >= pad
    la = _log_sigmoid(jnp.dot(z, wa_ref[...], preferred_element_type=F32, precision=HIGHEST) + ba_ref[...])
    la_ref[...] = jnp.where(valid, la * (1.0 / GLA_TAU), 0.0)
    lf = jnp.where(valid, _log_sigmoid(z), 0.0)

    r = lax.broadcasted_iota(jnp.int32, (ROW_TILE, ROW_TILE), 0)
    c = lax.broadcasted_iota(jnp.int32, (ROW_TILE, ROW_TILE), 1)
    tri = (r >= c).astype(F32)
    carry = carry_ref[...]
    for sb in range(tt // ROW_TILE):
        rows = slice(sb * ROW_TILE, (sb + 1) * ROW_TILE)
        cs = jnp.dot(tri, lf[rows], preferred_element_type=F32, precision=HIGHEST) + carry
        carry = cs[ROW_TILE - 1:ROW_TILE]
        posr = i * tt + sb * ROW_TILE + lax.broadcasted_iota(jnp.int32, (1, ROW_TILE), 1)
        crow_ref[0, :, rows] = jnp.where(posr >= pad, cs.T[0:8] * LOG2E, MASKED_KEY_BIAS)
    carry_ref[...] = carry


def _prep(hb, w_small, b_small, wa_ext, ba, *, B, lp, pad, tt):
    T, D = hb.shape
    KW = wa_ext.shape[1]
    nt = lp // tt
    return pl.pallas_call(
        functools.partial(_prep_kernel, pad=pad, tt=tt),
        grid=(B, nt),
        in_specs=[
            pl.BlockSpec((tt, D), lambda b, i: (b * nt + i, 0)),
            pl.BlockSpec((128, D), lambda b, i: (0, 0)),
            pl.BlockSpec((1, 128), lambda b, i: (0, 0)),
            pl.BlockSpec((128, KW), lambda b, i: (0, 0)),
            pl.BlockSpec((1, KW), lambda b, i: (0, 0)),
        ],
        out_specs=[
            pl.BlockSpec((1, 8, tt), lambda b, i: (b, 0, i)),
            pl.BlockSpec((tt, KW), lambda b, i: (b * nt + i, 0)),
        ],
        out_shape=[
            jax.ShapeDtypeStruct((B, 8, lp), F32),
            jax.ShapeDtypeStruct((T, KW), F32),
        ],
        scratch_shapes=[pltpu.VMEM((1, 128), F32)],
        compiler_params=_params(2),
        name="gate_prep",
    )(hb, w_small, b_small, wa_ext, ba)


def _fox_kernel(q_ref, k_ref, v_ref, crow_ref, o_ref, *, heads, tq, tk, group):
    qi = pl.program_id(1)
    n_full = (qi * tq) // tk
    kd = pl.multiple_of(n_full * tk, tk)
    rows = qi * tq + lax.broadcasted_iota(jnp.int32, (tq, tk), 0)
    cols = kd + lax.broadcasted_iota(jnp.int32, (tq, tk), 1)
    causal = cols <= rows
    ones = jnp.ones((tk, HEAD_DIM), BF16)

    def scores(h, k0):
        sl = slice(h * HEAD_DIM, (h + 1) * HEAD_DIM)
        cr = crow_ref[0, h:h + 1, pl.ds(k0, tk)]
        return _dot_nt(q_ref[:, sl], k_ref[pl.ds(k0, tk), sl]) - cr

    def weighted_values(h, p, k0):
        sl = slice(h * HEAD_DIM, (h + 1) * HEAD_DIM)
        va = jnp.concatenate([v_ref[pl.ds(k0, tk), sl], ones], axis=1)
        pv = jnp.dot(p, va, preferred_element_type=F32)
        return pv[:, :HEAD_DIM], pv[:, HEAD_DIM:HEAD_DIM + 1]

    for h0 in range(0, heads, group):
        hs = range(h0, min(h0 + group, heads))
        carry = []
        for h in hs:
            s = jnp.where(causal, scores(h, kd), -jnp.inf)
            m = jnp.max(s, axis=-1, keepdims=True)
            acc, l = weighted_values(h, jnp.exp2((s - m).astype(BF16)), kd)
            carry += [m, l, acc]

        def body(ki, carry):
            k0 = pl.multiple_of(ki * tk, tk)
            out = []
            for n, h in enumerate(hs):
                m, l, acc = carry[3 * n:3 * n + 3]
                s = scores(h, k0)
                m_new = jnp.maximum(m, jnp.max(s, axis=-1, keepdims=True))
                a = jnp.exp2(m - m_new)
                pv, p_sum = weighted_values(h, jnp.exp2((s - m_new).astype(BF16)), k0)
                out += [m_new, a * l + p_sum, a * acc + pv]
            return tuple(out)

        carry = lax.fori_loop(0, n_full, body, tuple(carry))
        for n, h in enumerate(hs):
            m, l, acc = carry[3 * n:3 * n + 3]
            o_ref[:, h * HEAD_DIM:(h + 1) * HEAD_DIM] = (acc / l).astype(o_ref.dtype)


def _fox(proj, crow, *, B, lp, fw, off_q, off_k, off_v, tq, tk):
    T = proj.shape[0]
    nq = lp // tq
    heads = fw // HEAD_DIM
    assert tk % tq == 0
    return pl.pallas_call(
        functools.partial(_fox_kernel, heads=heads, tq=tq, tk=tk, group=FOX_HEAD_GROUP),
        grid=(B, nq),
        in_specs=[
            pl.BlockSpec((tq, fw), lambda b, i: (b * nq + i, off_q // fw)),
            pl.BlockSpec((lp, fw), lambda b, i: (b, off_k // fw)),
            pl.BlockSpec((lp, fw), lambda b, i: (b, off_v // fw)),
            pl.BlockSpec((1, 8, lp), lambda b, i: (b, 0, 0)),
        ],
        out_specs=pl.BlockSpec((tq, fw), lambda b, i: (b * nq + i, 0)),
        out_shape=jax.ShapeDtypeStruct((T, fw), BF16),
        compiler_params=_params(2),
        name="fox_attention",
    )(proj, proj, proj, crow)


def _lru_kernel(lx_ref, lg_ref, cw_ref, cb_ref, wr_ref, br_ref, wi_ref, bi_ref, lam_ref, o_ref,
                ext_ref, a_ref, u_ref, hc_ref, *, pad, tt, nblk):
    i = pl.program_id(1)

    @pl.when(i == 0)
    def _():
        ext_ref[0:CONV_HALO, :] = jnp.zeros((CONV_HALO, ext_ref.shape[1]), F32)
        hc_ref[...] = jnp.zeros_like(hc_ref)

    @pl.when(i > 0)
    def _():
        ext_ref[0:CONV_HALO, :] = ext_ref[tt:tt + CONV_HALO, :]

    pos = i * tt + lax.broadcasted_iota(jnp.int32, (tt, 1), 0)
    valid = pos >= pad
    ext_ref[CONV_HALO:CONV_HALO + tt, :] = jnp.where(valid, lx_ref[...].astype(F32), 0.0)

    sp = jnp.maximum(-lam_ref[...], 0.0) + jnp.log1p(jnp.exp(-jnp.abs(lam_ref[...])))
    for n in range(nblk):
        sl = slice(n * HEAD_DIM, (n + 1) * HEAD_DIM)
        xc = cb_ref[:, sl]
        for j in range(CONV_W):
            start = CONV_HALO - (CONV_W - 1) + j
            xc = xc + cw_ref[j:j + 1, sl] * ext_ref[start:start + tt, sl]
        xb = xc.astype(BF16)
        r = _sigmoid(jnp.dot(xb, wr_ref[n].astype(BF16), preferred_element_type=F32) + br_ref[:, sl])
        g = _sigmoid(jnp.dot(xb, wi_ref[n].astype(BF16), preferred_element_type=F32) + bi_ref[:, sl])
        log_a = (-LRU_C) * r * sp[:, sl]
        a = jnp.exp(log_a)
        a_ref[:, sl] = a
        u = jnp.sqrt(-jnp.tanh(log_a) * (1.0 + a * a)) * (g * xc)
        u_ref[:, sl] = jnp.where(valid, u, 0.0)

    def body(gidx, h):
        r0 = pl.multiple_of(gidx * 8, 8)
        a8 = a_ref[pl.ds(r0, 8), :]
        u8 = u_ref[pl.ds(r0, 8), :]
        outs = []
        for r in range(8):
            h = a8[r:r + 1] * h + u8[r:r + 1]
            outs.append(h)
        u_ref[pl.ds(r0, 8), :] = jnp.concatenate(outs, axis=0)
        return h

    hc_ref[...] = lax.fori_loop(0, tt // 8, body, hc_ref[...])
    o_ref[...] = (u_ref[...] * _gelu_tanh(lg_ref[...].astype(F32))).astype(o_ref.dtype)


def _lru(proj, conv_w, conv_b, w_r, b_r, w_i, b_i, lam, *, B, lp, lw, off_x, off_g, pad, tt):
    T = proj.shape[0]
    nt = lp // tt
    nblk = lw // HEAD_DIM
    row = lambda b, i: (0, 0)
    return pl.pallas_call(
        functools.partial(_lru_kernel, pad=pad, tt=tt, nblk=nblk),
        grid=(B, nt),
        in_specs=[
            pl.BlockSpec((tt, lw), lambda b, i: (b * nt + i, off_x // lw)),
            pl.BlockSpec((tt, lw), lambda b, i: (b * nt + i, off_g // lw)),
            pl.BlockSpec((CONV_W, lw), row),
            pl.BlockSpec((1, lw), row),
            pl.BlockSpec((nblk, HEAD_DIM, HEAD_DIM), lambda b, i: (0, 0, 0)),
            pl.BlockSpec((1, lw), row),
            pl.BlockSpec((nblk, HEAD_DIM, HEAD_DIM), lambda b, i: (0, 0, 0)),
            pl.BlockSpec((1, lw), row),
            pl.BlockSpec((1, lw), row),
        ],
        out_specs=pl.BlockSpec((tt, lw), lambda b, i: (b * nt + i, 0)),
        out_shape=jax.ShapeDtypeStruct((T, lw), BF16),
        scratch_shapes=[
            pltpu.VMEM((tt + CONV_HALO, lw), F32),
            pltpu.VMEM((tt, lw), F32),
            pltpu.VMEM((tt, lw), F32),
            pltpu.VMEM((1, lw), F32),
        ],
        compiler_params=_params(2),
        name="conv_rglru",
    )(proj, proj, conv_w, conv_b, w_r, b_r, w_i, b_i, lam)


def _gla_kernel(q_ref, k_ref, v_ref, gg_ref, la_ref, ng_ref, o_ref, st_ref, *, pad, tt, heads):
    i = pl.program_id(1)

    @pl.when(i == 0)
    def _():
        st_ref[...] = jnp.zeros_like(st_ref)

    r = lax.broadcasted_iota(jnp.int32, (GLA_CHUNK, GLA_CHUNK), 0)
    c = lax.broadcasted_iota(jnp.int32, (GLA_CHUNK, GLA_CHUNK), 1)
    tri = (r >= c).astype(F32)
    for ci in range(tt // GLA_CHUNK):
        rows = slice(ci * GLA_CHUNK, (ci + 1) * GLA_CHUNK)
        pos = i * tt + ci * GLA_CHUNK + lax.broadcasted_iota(jnp.int32, (GLA_CHUNK, 1), 0)
        valid = pos >= pad
        for hd in range(heads):
            ks = slice(hd * HEAD_DIM, (hd + 1) * HEAD_DIM)
            vs = slice(hd * GLA_DV, (hd + 1) * GLA_DV)
            cs = jnp.dot(tri, la_ref[rows, ks], preferred_element_type=F32, precision=HIGHEST)
            cl = cs[GLA_CHUNK - 1:GLA_CHUNK]
            kdec = jnp.where(valid, k_ref[rows, ks].astype(F32) * jnp.exp(cl - cs), 0.0).astype(BF16)
            ut = lax.dot_general(v_ref[rows, vs], kdec, (((0,), (0,)), ((), ())), preferred_element_type=F32)
            st = st_ref[hd] * jnp.exp(cl) + ut
            st_ref[hd] = st
            o = _dot_nt(q_ref[rows, ks], st.astype(BF16))
            o = o * lax.rsqrt(jnp.mean(o * o, axis=-1, keepdims=True) + RMS_EPS) * ng_ref[:, vs]
            gg = gg_ref[rows, vs].astype(F32)
            o_ref[rows, vs] = (o * (gg * _sigmoid(gg))).astype(o_ref.dtype)


def _gla(proj, proj_g, la, norm_g, *, B, lp, kw, vw, off_q, off_k, off_v, off_g, pad, tt):
    T = proj.shape[0]
    nt = lp // tt
    heads = kw // HEAD_DIM
    return pl.pallas_call(
        functools.partial(_gla_kernel, pad=pad, tt=tt, heads=heads),
        grid=(B, nt),
        in_specs=[
            pl.BlockSpec((tt, kw), lambda b, i: (b * nt + i, off_q // kw)),
            pl.BlockSpec((tt, kw), lambda b, i: (b * nt + i, off_k // kw)),
            pl.BlockSpec((tt, vw), lambda b, i: (b * nt + i, off_v // vw)),
            pl.BlockSpec((tt, vw), lambda b, i: (b * nt + i, off_g // vw)),
            pl.BlockSpec((tt, kw), lambda b, i: (b * nt + i, 0)),
            pl.BlockSpec((1, vw), lambda b, i: (0, 0)),
        ],
        out_specs=pl.BlockSpec((tt, vw), lambda b, i: (b * nt + i, 0)),
        out_shape=jax.ShapeDtypeStruct((T, vw), BF16),
        scratch_shapes=[pltpu.VMEM((heads, GLA_DV, HEAD_DIM), F32)],
        compiler_params=_params(2),
        name="gla_chunked",
    )(proj, proj, proj, proj_g, la, norm_g)


def _merge_kernel(of_ref, ol_ref, og_ref, w_ref, g0_ref, g1_ref, g2_ref, o_ref, wbf_ref, *, fw, lw):
    @pl.when(pl.program_id(1) == 0)
    def _():
        wbf_ref[...] = w_ref[...].astype(BF16)

    y0 = jnp.dot(of_ref[...], wbf_ref[0:fw, :], preferred_element_type=F32)
    y1 = jnp.dot(ol_ref[...], wbf_ref[fw:fw + lw, :], preferred_element_type=F32)
    y2 = jnp.dot(og_ref[...], wbf_ref[fw + lw:, :], preferred_element_type=F32)
    out = (_sigmoid(g0_ref[...].astype(F32)) * y0 + _sigmoid(g1_ref[...].astype(F32)) * y1
           + _sigmoid(g2_ref[...].astype(F32)) * y2)
    o_ref[...] = out.astype(o_ref.dtype)


def _merge(o_fox, o_lru, o_gla, w_branch, layer, proj, *, off_gates, d):
    T, fw = o_fox.shape
    lw = o_lru.shape[1]
    vw = o_gla.shape[1]
    tm = _pick(T, (1056, 1024, 768, 512, 384, 256, 128))
    tn = _pick(math.gcd(d, off_gates), (512, 256, 128))
    gate_spec = lambda b: pl.BlockSpec((tm, tn), lambda j, i: (i, (off_gates + b * d) // tn + j))
    return pl.pallas_call(
        functools.partial(_merge_kernel, fw=fw, lw=lw),
        grid=(d // tn, T // tm),
        in_specs=[
            pl.BlockSpec((tm, fw), lambda j, i: (i, 0)),
            pl.BlockSpec((tm, lw), lambda j, i: (i, 0)),
            pl.BlockSpec((tm, vw), lambda j, i: (i, 0)),
            pl.BlockSpec((None, fw + lw + vw, tn), lambda j, i: (layer, 0, j)),
            gate_spec(0), gate_spec(1), gate_spec(2),
        ],
        out_specs=pl.BlockSpec((tm, tn), lambda j, i: (i, j)),
        out_shape=jax.ShapeDtypeStruct((T, d), BF16),
        scratch_shapes=[pltpu.VMEM((fw + lw + vw, tn), BF16)],
        compiler_params=_params(2),
        name="branch_merge",
    )(o_fox, o_lru, o_gla, w_branch, proj, proj, proj)


def _ln_router_kernel(h_ref, mix_ref, g_ref, b_ref, wr_ref, br_ref, hf_ref, hx_ref, idx_ref, wt_ref, *, alpha):
    y = _layer_norm(alpha * h_ref[...] + mix_ref[...], g_ref[...], b_ref[...])
    hf_ref[...] = y
    hx_ref[...] = _pack_halves(y)
    w = wr_ref[...]
    y_hi, w_hi = y.astype(BF16), w.astype(BF16)
    y_lo = (y - y_hi.astype(F32)).astype(BF16)
    w_lo = (w - w_hi.astype(F32)).astype(BF16)
    logits = (jnp.dot(y_hi, w_hi, preferred_element_type=F32)
              + (jnp.dot(y_lo, w_hi, preferred_element_type=F32) + jnp.dot(y_hi, w_lo, preferred_element_type=F32))
              + br_ref[...])
    lane = lax.broadcasted_iota(jnp.int32, logits.shape, 1).astype(F32)
    idx = jnp.zeros_like(logits)
    vals = []
    for k in range(TOP_K):
        mx = jnp.max(logits, axis=-1, keepdims=True)
        sel = jnp.min(jnp.where(logits == mx, lane, float(logits.shape[1])), axis=-1, keepdims=True)
        vals.append(mx)
        idx = jnp.where(lane == float(k), sel, idx)
        logits = jnp.where(lane == sel, -jnp.inf, logits)
    es = [jnp.exp(v - vals[0]) for v in vals]
    tot = es[0]
    for e in es[1:]:
        tot = tot + e
    wt = jnp.zeros_like(logits)
    for k in range(TOP_K):
        wt = jnp.where(lane == float(k), es[k] / tot, wt)
    idx_ref[...] = idx.astype(jnp.int32)
    wt_ref[...] = wt


def _ln_router(h, mix, g, b, wr_pad, br_pad, *, alpha):
    T, D = h.shape
    tm = _pick(T, (192, 128))
    blk = pl.BlockSpec((tm, D), lambda i: (i, 0))
    row = pl.BlockSpec((1, D), lambda i: (0, 0))
    small = pl.BlockSpec((tm, 128), lambda i: (i, 0))
    return pl.pallas_call(
        functools.partial(_ln_router_kernel, alpha=alpha),
        grid=(T // tm,),
        in_specs=[blk, blk, row, row, pl.BlockSpec((D, 128), lambda i: (0, 0)), pl.BlockSpec((1, 128), lambda i: (0, 0))],
        out_specs=[blk, pl.BlockSpec((tm, D // 2), lambda i: (i, 0)), small, small],
        out_shape=[
            jax.ShapeDtypeStruct((T, D), F32),
            jax.ShapeDtypeStruct((T, D // 2), U32),
            jax.ShapeDtypeStruct((T, 128), jnp.int32),
            jax.ShapeDtypeStruct((T, 128), F32),
        ],
        compiler_params=_params(1),
        name="ln_router",
    )(h, mix, g, b, wr_pad, br_pad)


def _pack_halves(y):
    half = y.shape[1] // 2
    lo = pltpu.bitcast(y[:, :half].astype(BF16).astype(F32), U32)
    hi = pltpu.bitcast(y[:, half:].astype(BF16).astype(F32), U32)
    return lax.shift_right_logical(lo, jnp.uint32(16)) | (hi & jnp.uint32(0xFFFF0000))


def _unpack_halves(w):
    lo = pltpu.bitcast(lax.shift_left(w, jnp.uint32(16)), F32)
    hi = pltpu.bitcast(w & jnp.uint32(0xFFFF0000), F32)
    return lo, hi


def _row_gather(src_hbm, idx_ref, base, buf_ref, sem, n, static=False):
    def start(r):
        pltpu.make_async_copy(src_hbm.at[pl.ds(idx_ref[base + r], 1)], buf_ref.at[pl.ds(r, 1)], sem).start()

    if static:
        for r in range(n):
            start(r)
    else:
        def body(r, carry):
            start(r)
            return carry

        lax.fori_loop(0, n, body, 0, unroll=8)


def _tile_wait(src_hbm, buf_ref, sem):
    pltpu.make_async_copy(src_hbm.at[pl.ds(0, buf_ref.shape[0])], buf_ref, sem).wait()


def _moe_kernel(te_ref, na_ref, first_ref, nxt_ref, tok_ref, h_hbm, wup_hbm, wdn_hbm, bup_ref, sel_ref, bdn_ref,
                y_ref, xbuf_ref, sem_ref, sup_ref, sdn_ref, wup_ref, wdn_ref, wsem_ref, *, tmx, layer):
    i = pl.program_id(0)
    na = na_ref[0]
    cur = lax.rem(i, GATHER_RING)
    ahead = lax.rem(i + GATHER_RING - 1, GATHER_RING)

    def weight_copies(e):
        return (pltpu.make_async_copy(wup_hbm.at[layer, e], sup_ref, wsem_ref.at[0]),
                pltpu.make_async_copy(wdn_hbm.at[layer, e], sdn_ref, wsem_ref.at[1]))

    def start_weights(e):
        up_copy, dn_copy = weight_copies(e)
        up_copy.start(priority=1)
        dn_copy.start()

    for t in range(GATHER_RING - 1):
        @pl.when(jnp.logical_and(i == 0, na > t))
        def _():
            _row_gather(h_hbm, tok_ref, t * tmx, xbuf_ref.at[t], sem_ref.at[t], tmx)

    @pl.when(jnp.logical_and(i == 0, na > 0))
    def _():
        start_weights(te_ref[0])

    @pl.when(jnp.logical_and(i < na, first_ref[i] == 1))
    def _():
        up_copy, dn_copy = weight_copies(te_ref[i])
        up_copy.wait()
        _cast_rows(sup_ref, wup_ref, 256)
        dn_copy.wait()
        _cast_rows(sdn_ref, wdn_ref, 128)

        @pl.when(nxt_ref[i] >= 0)
        def _():
            start_weights(nxt_ref[i])

    @pl.when(i < na)
    def _():
        _tile_wait(h_hbm, xbuf_ref.at[cur], sem_ref.at[cur])
        lo, hi = _unpack_halves(xbuf_ref[cur])
        x = jnp.concatenate([lo.astype(BF16), hi.astype(BF16)], axis=1)
        h = jnp.dot(x, wup_ref[...], preferred_element_type=F32) + bup_ref[...]
        g = jnp.minimum(h, SWIGLU_LIMIT)
        u = jnp.clip(pltpu.roll(h, h.shape[1] - 1, axis=1), -SWIGLU_LIMIT, SWIGLU_LIMIT)
        act = ((u + 1.0) * g * _sigmoid(SWIGLU_ALPHA * g)).astype(BF16)
        act = jnp.dot(act, sel_ref[...], preferred_element_type=F32).astype(BF16)
        y = jnp.dot(act, wdn_ref[...], preferred_element_type=F32) + bdn_ref[...]
        y_ref[...] = _pack_halves(y)

    @pl.when(i + GATHER_RING - 1 < na)
    def _():
        _row_gather(h_hbm, tok_ref, (i + GATHER_RING - 1) * tmx, xbuf_ref.at[ahead], sem_ref.at[ahead], tmx,
                    static=True)

    @pl.when(i >= na)
    def _():
        y_ref[...] = jnp.zeros_like(y_ref)


def _moe(tile_expert, n_active, row_token, hx, w_up, b_up, sel, w_down, b_down, *, tmx, layer):
    T, D = hx.shape[0], 2 * hx.shape[1]
    _, E, _, F2 = w_up.shape
    F = F2 // 2
    P = row_token.shape[0]
    ntiles = P // tmx
    tiles = jnp.arange(ntiles, dtype=jnp.int32)
    first = jnp.where(tiles == 0, 1, (tile_expert != jnp.roll(tile_expert, 1)).astype(jnp.int32))
    j = jnp.searchsorted(tile_expert, tile_expert, side="right").astype(jnp.int32)
    nxt = jnp.where(j < ntiles, tile_expert[jnp.minimum(j, ntiles - 1)], -1).astype(jnp.int32)

    return pl.pallas_call(
        functools.partial(_moe_kernel, tmx=tmx, layer=layer),
        grid_spec=pltpu.PrefetchScalarGridSpec(
            num_scalar_prefetch=5,
            grid=(ntiles,),
            in_specs=[
                pl.BlockSpec(memory_space=pl.ANY),
                pl.BlockSpec(memory_space=pl.ANY),
                pl.BlockSpec(memory_space=pl.ANY),
                pl.BlockSpec((None, 1, F2), lambda i, te, *_: (te[i], 0, 0)),
                pl.BlockSpec((F2, F), lambda i, *_: (0, 0)),
                pl.BlockSpec((None, 1, D), lambda i, te, *_: (te[i], 0, 0)),
            ],
            out_specs=pl.BlockSpec((tmx, D // 2), lambda i, *_: (i, 0)),
            scratch_shapes=[
                pltpu.VMEM((GATHER_RING, tmx, D // 2), U32), pltpu.SemaphoreType.DMA((GATHER_RING,)),
                pltpu.VMEM((D, F2), F32), pltpu.VMEM((F, D), F32),
                pltpu.VMEM((D, F2), BF16), pltpu.VMEM((F, D), BF16), pltpu.SemaphoreType.DMA((2,)),
            ],
        ),
        out_shape=jax.ShapeDtypeStruct((P, D // 2), U32),
        compiler_params=_params(1),
        name="moe_experts",
    )(tile_expert, n_active, first, nxt, row_token, hx, w_up, w_down, b_up, sel, b_down)


def _combine_kernel(pos_ref, y_hbm, h_ref, w_ref, g_ref, b_ref, *rest, alpha, tc, nt, final):
    if final:
        out_ref, buf_ref, sem_ref = rest
    else:
        hf_ref, hb_ref, buf_ref, sem_ref = rest
    i = pl.program_id(0)
    cur = lax.rem(i, GATHER_RING)
    ahead = lax.rem(i + GATHER_RING - 1, GATHER_RING)

    def start(tile, slot, static):
        for k in range(TOP_K):
            _row_gather(y_hbm, pos_ref, (k * nt + tile) * tc, buf_ref.at[slot, k], sem_ref.at[slot], tc, static=static)

    @pl.when(i == 0)
    def _():
        for t in range(min(GATHER_RING - 1, nt)):
            start(t, t, False)

    for k in range(TOP_K):
        _tile_wait(y_hbm, buf_ref.at[cur, k], sem_ref.at[cur])
    lo = hi = None
    for k in range(TOP_K):
        l, h = _unpack_halves(buf_ref[cur, k])
        wk = w_ref[:, k:k + 1]
        lo = wk * l if lo is None else lo + wk * l
        hi = wk * h if hi is None else hi + wk * h
    ffn = jnp.concatenate([lo, hi], axis=1)
    y = _layer_norm(alpha * h_ref[...] + ffn, g_ref[...], b_ref[...])
    if final:
        out_ref[...] = y
    else:
        hf_ref[...] = y
        hb_ref[...] = y.astype(BF16)

    @pl.when(i + GATHER_RING - 1 < nt)
    def _():
        start(i + GATHER_RING - 1, ahead, True)


def _combine(pos, ys, hf, top_w, g, b, *, alpha, lp, final):
    T, D = hf.shape
    tc = ROW_TILE
    nt = T // tc
    ntb = lp // tc
    if final:
        out_specs = pl.BlockSpec((tc, D), lambda i, pos: ((i // ntb) * (ntb - 1) + jnp.maximum(i % ntb - 1, 0), 0))
        out_shape = jax.ShapeDtypeStruct(((T // lp) * (lp - tc), D), F32)
    else:
        out_specs = [pl.BlockSpec((tc, D), lambda i, pos: (i, 0)), pl.BlockSpec((tc, D), lambda i, pos: (i, 0))]
        out_shape = [jax.ShapeDtypeStruct((T, D), F32), jax.ShapeDtypeStruct((T, D), BF16)]
    grid_spec = pltpu.PrefetchScalarGridSpec(
        num_scalar_prefetch=1,
        grid=(nt,),
        in_specs=[
            pl.BlockSpec(memory_space=pl.ANY),
            pl.BlockSpec((tc, D), lambda i, pos: (i, 0)),
            pl.BlockSpec((tc, 128), lambda i, pos: (i, 0)),
            pl.BlockSpec((1, D), lambda i, pos: (0, 0)),
            pl.BlockSpec((1, D), lambda i, pos: (0, 0)),
        ],
        out_specs=out_specs,
        scratch_shapes=[pltpu.VMEM((GATHER_RING, TOP_K, tc, D // 2), U32), pltpu.SemaphoreType.DMA((GATHER_RING,))],
    )
    return pl.pallas_call(
        functools.partial(_combine_kernel, alpha=alpha, tc=tc, nt=nt, final=final),
        grid_spec=grid_spec,
        out_shape=out_shape,
        compiler_params=_params(1),
        name="moe_combine_ln",
    )(pos, ys, hf, top_w, g, b)


def _route_kernel(idx_ref, pos_ref, te_ref, na_ref, cnt_ref, pst_ref, *, pad, tr, tmx, n_experts, spare_row):
    p = pl.program_id(0)
    i = pl.program_id(2)
    first = jnp.logical_and(pl.program_id(1) == 0, i == 0)

    @pl.when(jnp.logical_and(p == 0, first))
    def _():
        cnt_ref[...] = jnp.zeros_like(cnt_ref)

    @pl.when(jnp.logical_and(p == 1, first))
    def _():
        cnt = cnt_ref[...]
        padded = jnp.floor((cnt + (tmx - 1.0)) * (1.0 / tmx)) * tmx
        r = lax.broadcasted_iota(jnp.int32, (128, 128), 0)
        c = lax.broadcasted_iota(jnp.int32, (128, 128), 1)
        pst = jnp.dot(padded, (r < c).astype(F32), preferred_element_type=F32, precision=HIGHEST)
        pst_ref[...] = pst
        cnt_ref[...] = jnp.zeros_like(cnt_ref)
        tile_end = (pst + padded) * (1.0 / tmx)
        nt_pad = te_ref.shape[0]
        t = lax.broadcasted_iota(jnp.int32, (nt_pad, 128), 0).astype(F32)
        lane = lax.broadcasted_iota(jnp.int32, (nt_pad, 128), 1)
        is_expert = lane < n_experts
        te = jnp.sum(jnp.where(jnp.logical_and(tile_end <= t, is_expert), 1.0, 0.0), axis=-1, keepdims=True)
        lane1 = lax.broadcasted_iota(jnp.int32, (1, 128), 1).astype(F32)
        e_last = jnp.max(jnp.where(cnt > 0.0, lane1, 0.0), axis=-1, keepdims=True)
        te_ref[...] = jnp.broadcast_to(jnp.minimum(te, e_last), (nt_pad, 128)).astype(jnp.int32)
        na_ref[...] = jnp.broadcast_to(jnp.max(tile_end, axis=-1, keepdims=True), (1, 128)).astype(jnp.int32)

    idx = idx_ref[...]
    lane = lax.broadcasted_iota(jnp.int32, (tr, 128), 1)
    valid = (i * tr + lax.broadcasted_iota(jnp.int32, (tr, 1), 0)) >= pad
    onehot = jnp.zeros((tr, 128), F32)
    for k in range(TOP_K):
        onehot = onehot + jnp.where(lane == idx[:, k:k + 1], 1.0, 0.0)
    onehot = jnp.where(valid, onehot, 0.0)
    r = lax.broadcasted_iota(jnp.int32, (tr, tr), 0)
    c = lax.broadcasted_iota(jnp.int32, (tr, tr), 1)
    rank = jnp.dot((r > c).astype(BF16), onehot.astype(BF16), preferred_element_type=F32) + cnt_ref[...]
    cnt_ref[...] = cnt_ref[...] + jnp.sum(onehot, axis=0, keepdims=True)

    @pl.when(p == 0)
    def _():
        pos_ref[...] = jnp.zeros_like(pos_ref)

    @pl.when(p == 1)
    def _():
        dest = rank + pst_ref[...]
        out = jnp.zeros((tr, 128), F32)
        for k in range(TOP_K):
            d = jnp.sum(jnp.where(lane == idx[:, k:k + 1], dest, 0.0), axis=-1, keepdims=True)
            out = jnp.where(lane == k, jnp.where(valid, d, float(spare_row)), out)
        pos_ref[...] = out.astype(jnp.int32)


def _route(top_idx, *, B, lp, pad, tmx, n_experts, n_rows):
    T = top_idx.shape[0]
    tr = _pick(lp, (384, 256, 128))
    nt = lp // tr
    nt_pad = -(-(n_rows // tmx) // 8) * 8
    pos, te, na = pl.pallas_call(
        functools.partial(_route_kernel, pad=pad, tr=tr, tmx=tmx, n_experts=n_experts, spare_row=n_rows - 1),
        grid=(2, B, nt),
        in_specs=[pl.BlockSpec((tr, 128), lambda p, b, i: (b * nt + i, 0))],
        out_specs=[
            pl.BlockSpec((tr, 128), lambda p, b, i: (p * (b * nt + i), 0)),
            pl.BlockSpec((nt_pad, 128), lambda p, b, i: (0, 0)),
            pl.BlockSpec((1, 128), lambda p, b, i: (0, 0)),
        ],
        out_shape=[
            jax.ShapeDtypeStruct((T, 128), jnp.int32),
            jax.ShapeDtypeStruct((nt_pad, 128), jnp.int32),
            jax.ShapeDtypeStruct((1, 128), jnp.int32),
        ],
        scratch_shapes=[pltpu.VMEM((1, 128), F32), pltpu.VMEM((1, 128), F32)],
        compiler_params=_params(3),
        name="route_rank",
    )(top_idx)
    pos_flat = pos[:, :TOP_K].T.reshape(-1)
    return pos_flat, te[:n_rows // tmx, 0], na[0, :1]


def _row_token_kernel(pos_ref, rt_ref, *, t):
    def zero(r, carry):
        rt_ref[r] = 0
        return carry

    lax.fori_loop(0, rt_ref.shape[0], zero, 0, unroll=8)
    for k in range(TOP_K):
        def body(tok, carry):
            rt_ref[pos_ref[k * t + tok]] = tok
            return carry

        lax.fori_loop(0, t, body, 0, unroll=8)


def _row_token(pos_flat, *, t, n_rows):
    return pl.pallas_call(
        functools.partial(_row_token_kernel, t=t),
        grid_spec=pltpu.PrefetchScalarGridSpec(
            num_scalar_prefetch=1, grid=(1,), in_specs=[],
            out_specs=pl.BlockSpec(memory_space=pltpu.SMEM)),
        out_shape=jax.ShapeDtypeStruct((n_rows,), jnp.int32),
        compiler_params=_params(1),
        name="route_row_token",
    )(pos_flat)


def _segments(fw, fh, lw, kw, vw, rank, d):
    names = ("fq", "fk", "fv", "ff", "lx", "lg", "gq", "gk", "gv", "ga", "gg", "gate0", "gate1", "gate2")
    widths = (fw, fw, fw, fh, lw, lw, kw, kw, vw, rank, vw, d, d, d)
    segs, off = {}, 0
    for n, w in zip(names, widths):
        segs[n] = (off, w)
        off += w
    return segs, off


def kernel(x, meta_tokens, emb_ln_g, emb_ln_b, w_in, b_in, conv_w, conv_b, lru_w_r, lru_b_r, lru_w_i, lru_b_i,
           lru_lambda, gla_w_alpha, gla_b_alpha, gla_norm_g, w_branch, w_out, b_out, ln1_g, ln1_b, w_router,
           b_router, w_up, b_up, w_down, b_down, ln2_g, ln2_b):
    B, S, D = x.shape
    n_meta = meta_tokens.shape[0]
    depth = w_in.shape[0]
    L = S + n_meta
    pad = (-L) % ROW_TILE
    lp = L + pad
    assert pad + n_meta == ROW_TILE and S % ROW_TILE == 0
    T = B * lp

    lw = conv_w.shape[2]
    rank, kw = gla_w_alpha.shape[1:]
    vw = gla_norm_g.shape[1]
    n_experts = w_router.shape[2]
    fexp = w_up.shape[3] // 2
    in_cols = w_in.shape[2]
    fh = (in_cols - 2 * lw - 2 * kw - 2 * vw - rank - 3 * D) // (3 * HEAD_DIM + 1)
    fw = fh * HEAD_DIM
    segs, total = _segments(fw, fh, lw, kw, vw, rank, D)
    assert total == in_cols and fh <= 8 and fh + rank <= 128
    alpha = (2.0 * depth) ** 0.25

    windows = (("fq", "fk", "fv"), ("lx", "lg", "gq", "gk", "gv"), ("gg", "gate0", "gate1", "gate2"))
    qscale = HEAD_DIM ** -0.5
    off, win_start, win_scale = {}, [], []
    for names in windows:
        start = segs[names[0]][0]
        for n in names:
            off[n] = segs[n][0] - start
            assert off[n] % segs[n][1] == 0 or n.startswith("gate")
        win_start.append(start)
        scales = {"fq": qscale * LOG2E, "gq": qscale}
        win_scale.append(jnp.concatenate(
            [jnp.full((segs[n][1],), scales.get(n, 1.0), F32) for n in names]).reshape(1, -1))

    def cols(a, n):
        o, w = segs[n]
        return a[..., o:o + w]

    tt = _pick(lp, (384, 256, 128))
    tmx = MOE_ROW_TILE if T >= 4096 else 128
    n_rows = (-(-(B * L * TOP_K + n_experts * (tmx - 1)) // tmx) + 1) * tmx

    head = jnp.concatenate([jnp.zeros((pad, D), F32), meta_tokens.astype(F32)], axis=0)
    hf, hb = _embed_ln(x, head, emb_ln_g.reshape(1, D), emb_ln_b.reshape(1, D), pad=pad, lp=lp)
    sel = (jnp.arange(2 * fexp)[:, None] == 2 * jnp.arange(fexp)[None, :]).astype(BF16)

    w_in_t = jnp.swapaxes(w_in, 1, 2)

    def rows(l, n):
        o, w = segs[n]
        return w_in_t[l, o:o + w]

    for l in range(depth):
        w_small = jnp.concatenate(
            [rows(l, "ff"), jnp.zeros((8 - fh, D), F32), rows(l, "ga"), jnp.zeros((128 - 8 - rank, D), F32)], axis=0)
        b_small = jnp.concatenate(
            [cols(b_in[l], "ff"), jnp.zeros((8 - fh,), F32), cols(b_in[l], "ga"),
             jnp.zeros((128 - 8 - rank,), F32)]).reshape(1, 128)
        wa_ext = jnp.zeros((128, kw), F32).at[8:8 + rank].set(gla_w_alpha[l])

        pa, pb, pc = [
            _in_proj(hb, w_in_t, l, b_in[l][s:s + sc.shape[1]].reshape(1, -1), sc, start=s, name=f"in_proj_{k}")
            for k, (s, sc) in enumerate(zip(win_start, win_scale))]
        crow, la = _prep(hb, w_small, b_small, wa_ext, gla_b_alpha[l].reshape(1, kw), B=B, lp=lp, pad=pad, tt=tt)
        o_fox = _fox(pa, crow, B=B, lp=lp, fw=fw, off_q=off["fq"], off_k=off["fk"], off_v=off["fv"], tq=FOX_Q_TILE,
                     tk=tt)
        o_lru = _lru(pb, conv_w[l], conv_b[l].reshape(1, lw), lru_w_r[l], lru_b_r[l].reshape(1, lw), lru_w_i[l],
                     lru_b_i[l].reshape(1, lw), lru_lambda[l].reshape(1, lw), B=B, lp=lp, lw=lw, off_x=off["lx"],
                     off_g=off["lg"], pad=pad, tt=tt)
        o_gla = _gla(pb, pc, la, gla_norm_g[l].reshape(1, vw), B=B, lp=lp, kw=kw, vw=vw, off_q=off["gq"],
                     off_k=off["gk"], off_v=off["gv"], off_g=off["gg"], pad=pad, tt=tt)
        merged = _merge(o_fox, o_lru, o_gla, w_branch, l, pc, off_gates=off["gate0"], d=D)
        mix = _matmul(merged, w_out, l, b_out[l].reshape(1, D), jnp.ones((1, D), F32), F32, "out_proj")

        wr_pad = jnp.zeros((D, 128), F32).at[:, :n_experts].set(w_router[l])
        br_pad = jnp.full((1, 128), -MASKED_KEY_BIAS, F32).at[0, :n_experts].set(b_router[l])
        hf, hx, top_idx, top_w = _ln_router(hf, mix, ln1_g[l].reshape(1, D), ln1_b[l].reshape(1, D), wr_pad, br_pad,
                                            alpha=alpha)
        pos, tile_expert, n_active = _route(top_idx, B=B, lp=lp, pad=pad, tmx=tmx, n_experts=n_experts, n_rows=n_rows)
        row_token = _row_token(pos, t=T, n_rows=n_rows)
        ys = _moe(tile_expert, n_active, row_token, hx, w_up, b_up[l].reshape(n_experts, 1, 2 * fexp), sel, w_down,
                  b_down[l].reshape(n_experts, 1, D), tmx=tmx, layer=l)
        final = l == depth - 1
        res = _combine(pos, ys, hf, top_w, ln2_g[l].reshape(1, D), ln2_b[l].reshape(1, D), alpha=alpha, lp=lp,
                       final=final)
        if final:
            return res.reshape(B, S, D)
        hf, hb = res
```

```python
import functools
import math

import jax
import jax.numpy as jnp
from jax import lax
from jax.experimental import pallas as pl
from jax.experimental.pallas import tpu as pltpu

F32 = jnp.float32
BF16 = jnp.bfloat16
U32 = jnp.uint32
HIGHEST = lax.Precision.HIGHEST

ROW_TILE = 128
HEAD_DIM = 128
GLA_DV = 256
GLA_CHUNK = 64
CONV_W = 4
CONV_HALO = 8
LN_EPS = 1e-5
RMS_EPS = 1e-6
LRU_C = 8.0
GLA_TAU = 16.0
SWIGLU_LIMIT = 7.0
SWIGLU_ALPHA = 1.702
TOP_K = 4
LOG2E = 1.4426950408889634
MASKED_KEY_BIAS = 1e30
MOE_ROW_TILE = 256
FOX_HEAD_GROUP = 4
FOX_Q_TILE = 384
GATHER_RING = 4
VMEM_LIMIT = 56 * 1024 * 1024


def _pick(n, candidates):
    for c in candidates:
        if n % c == 0:
            return c
    raise ValueError(f"no tile in {candidates} divides {n}")


def _params(n_axes, vmem=VMEM_LIMIT):
    return pltpu.CompilerParams(dimension_semantics=("arbitrary",) * n_axes, vmem_limit_bytes=vmem)


def _layer_norm(v, g, b):
    mu = jnp.mean(v, axis=-1, keepdims=True)
    d = v - mu
    var = jnp.mean(d * d, axis=-1, keepdims=True)
    return d * lax.rsqrt(var + LN_EPS) * g + b


def _log_sigmoid(x):
    return jnp.minimum(x, 0.0) - jnp.log1p(jnp.exp(-jnp.abs(x)))


def _sigmoid(x):
    return 0.5 * jnp.tanh(0.5 * x) + 0.5


def _gelu_tanh(x):
    return 0.5 * x * (1.0 + jnp.tanh(0.7978845608028654 * (x + 0.044715 * (x * x * x))))


def _embed_ln_kernel(x_ref, head_ref, g_ref, b_ref, hf_ref, hb_ref, *, pad):
    i = pl.program_id(1)

    @pl.when(i == 0)
    def _():
        y = _layer_norm(head_ref[...], g_ref[...], b_ref[...])
        rows = lax.broadcasted_iota(jnp.int32, (ROW_TILE, 1), 0)
        y = jnp.where(rows >= pad, y, 0.0)
        hf_ref[...] = y
        hb_ref[...] = y.astype(BF16)

    @pl.when(i > 0)
    def _():
        y = _layer_norm(x_ref[0], g_ref[...], b_ref[...])
        hf_ref[...] = y
        hb_ref[...] = y.astype(BF16)


def _embed_ln(x, head, g, b, *, pad, lp):
    B, S, D = x.shape
    nt = lp // ROW_TILE
    T = B * lp
    return pl.pallas_call(
        functools.partial(_embed_ln_kernel, pad=pad),
        grid=(B, nt),
        in_specs=[
            pl.BlockSpec((1, ROW_TILE, D), lambda b, i: (b, jnp.maximum(i - 1, 0), 0)),
            pl.BlockSpec((ROW_TILE, D), lambda b, i: (0, 0)),
            pl.BlockSpec((1, D), lambda b, i: (0, 0)),
            pl.BlockSpec((1, D), lambda b, i: (0, 0)),
        ],
        out_specs=[
            pl.BlockSpec((ROW_TILE, D), lambda b, i: (b * nt + i, 0)),
            pl.BlockSpec((ROW_TILE, D), lambda b, i: (b * nt + i, 0)),
        ],
        out_shape=[jax.ShapeDtypeStruct((T, D), F32), jax.ShapeDtypeStruct((T, D), BF16)],
        compiler_params=_params(2),
        name="embed_ln",
    )(x, head, g, b)


def _dot_nt(a, b):
    return lax.dot_general(a, b, (((1,), (1,)), ((), ())), preferred_element_type=F32)


def _cast_rows(src_ref, dst_ref, chunk):
    def body(c, carry):
        r0 = pl.multiple_of(c * chunk, chunk)
        dst_ref[pl.ds(r0, chunk), :] = src_ref[pl.ds(r0, chunk), :].astype(BF16)
        return carry

    lax.fori_loop(0, src_ref.shape[0] // chunk, body, 0)


def _mm_kernel(a_ref, w_ref, b_ref, s_ref, o_ref, *scratch):
    if scratch:
        (wbf_ref,) = scratch

        @pl.when(pl.program_id(1) == 0)
        def _():
            wbf_ref[...] = w_ref[...].astype(BF16)

        w = wbf_ref[...]
    else:
        w = w_ref[...]
    acc = jnp.dot(a_ref[...], w, preferred_element_type=F32)
    o_ref[...] = ((acc + b_ref[...]) * s_ref[...]).astype(o_ref.dtype)


def _matmul(a, w, layer, bias, scale, out_dtype, name):
    M, K = a.shape
    N = w.shape[2]
    tm = _pick(M, (1056, 1024, 768, 512, 384, 256, 128))
    tn = _pick(N, (512, 256, 128))
    scratch = [pltpu.VMEM((K, tn), BF16)] if w.dtype != BF16 else []
    return pl.pallas_call(
        _mm_kernel,
        grid=(N // tn, M // tm),
        in_specs=[
            pl.BlockSpec((tm, K), lambda j, i: (i, 0)),
            pl.BlockSpec((None, K, tn), lambda j, i: (layer, 0, j)),
            pl.BlockSpec((1, tn), lambda j, i: (0, j)),
            pl.BlockSpec((1, tn), lambda j, i: (0, j)),
        ],
        out_specs=pl.BlockSpec((tm, tn), lambda j, i: (i, j)),
        out_shape=jax.ShapeDtypeStruct((M, N), out_dtype),
        scratch_shapes=scratch,
        compiler_params=_params(2),
        name=name,
    )(a, w, bias, scale)


def _mm_nt_kernel(a_ref, wt_ref, b_ref, s_ref, o_ref, wbf_ref):
    @pl.when(pl.program_id(1) == 0)
    def _():
        _cast_rows(wt_ref.at[0], wbf_ref, 64)

    acc = _dot_nt(a_ref[...], wbf_ref[...])
    o_ref[...] = ((acc + b_ref[...]) * s_ref[...]).astype(o_ref.dtype)


def _in_proj(a, wt, layer, bias, scale, *, start, name):
    M, K = a.shape
    n = bias.shape[1]
    assert start % 8 == 0
    tm = _pick(M, (1056, 1024, 768, 512, 384, 256, 128))
    tn = _pick(n, (512, 256, 128))
    return pl.pallas_call(
        _mm_nt_kernel,
        grid=(n // tn, M // tm),
        in_specs=[
            pl.BlockSpec((tm, K), lambda j, i: (i, 0)),
            pl.BlockSpec((pl.Element(1), pl.Element(tn), pl.Element(K)),
                         lambda j, i: (layer, 8 * (start // 8 + j * (tn // 8)), 0)),
            pl.BlockSpec((1, tn), lambda j, i: (0, j)),
            pl.BlockSpec((1, tn), lambda j, i: (0, j)),
        ],
        out_specs=pl.BlockSpec((tm, tn), lambda j, i: (i, j)),
        out_shape=jax.ShapeDtypeStruct((M, n), BF16),
        scratch_shapes=[pltpu.VMEM((tn, K), BF16)],
        compiler_params=_params(2),
        name=name,
    )(a, wt, bias, scale)


def _prep_kernel(hb_ref, ws_ref, bs_ref, wa_ref, ba_ref, crow_ref, la_ref, carry_ref, *, pad, tt):
    i = pl.program_id(1)

    @pl.when(i == 0)
    def _():
        carry_ref[...] = jnp.zeros_like(carry_ref)

    z = _dot_nt(hb_ref[...], ws_ref[...].astype(BF16)) + bs_ref[...]
    pos = i * tt + lax.broadcasted_iota(jnp.int32, (tt, 1), 0)
    valid = pos >= pad
    la = _log_sigmoid(jnp.dot(z, wa_ref[...], preferred_element_type=F32, precision=HIGHEST) + ba_ref[...])
    la_ref[...] = jnp.where(valid, la * (1.0 / GLA_TAU), 0.0)
    lf = jnp.where(valid, _log_sigmoid(z), 0.0)

    r = lax.broadcasted_iota(jnp.int32, (ROW_TILE, ROW_TILE), 0)
    c = lax.broadcasted_iota(jnp.int32, (ROW_TILE, ROW_TILE), 1)
    tri = (r >= c).astype(F32)
    carry = carry_ref[...]
    for sb in range(tt // ROW_TILE):
        rows = slice(sb * ROW_TILE, (sb + 1) * ROW_TILE)
        cs = jnp.dot(tri, lf[rows], preferred_element_type=F32, precision=HIGHEST) + carry
        carry = cs[ROW_TILE - 1:ROW_TILE]
        posr = i * tt + sb * ROW_TILE + lax.broadcasted_iota(jnp.int32, (1, ROW_TILE), 1)
        crow_ref[0, :, rows] = jnp.where(posr >= pad, cs.T[0:8] * LOG2E, MASKED_KEY_BIAS)
    carry_ref[...] = carry


def _prep(hb, w_small, b_small, wa_ext, ba, *, B, lp, pad, tt):
    T, D = hb.shape
    KW = wa_ext.shape[1]
    nt = lp // tt
    return pl.pallas_call(
        functools.partial(_prep_kernel, pad=pad, tt=tt),
        grid=(B, nt),
        in_specs=[
            pl.BlockSpec((tt, D), lambda b, i: (b * nt + i, 0)),
            pl.BlockSpec((128, D), lambda b, i: (0, 0)),
            pl.BlockSpec((1, 128), lambda b, i: (0, 0)),
            pl.BlockSpec((128, KW), lambda b, i: (0, 0)),
            pl.BlockSpec((1, KW), lambda b, i: (0, 0)),
        ],
        out_specs=[
            pl.BlockSpec((1, 8, tt), lambda b, i: (b, 0, i)),
            pl.BlockSpec((tt, KW), lambda b, i: (b * nt + i, 0)),
        ],
        out_shape=[
            jax.ShapeDtypeStruct((B, 8, lp), F32),
            jax.ShapeDtypeStruct((T, KW), F32),
        ],
        scratch_shapes=[pltpu.VMEM((1, 128), F32)],
        compiler_params=_params(2),
        name="gate_prep",
    )(hb, w_small, b_small, wa_ext, ba)


def _fox_kernel(q_ref, k_ref, v_ref, crow_ref, o_ref, *, heads, tq, tk, group):
    qi = pl.program_id(1)
    n_full = (qi * tq) // tk
    kd = pl.multiple_of(n_full * tk, tk)
    rows = qi * tq + lax.broadcasted_iota(jnp.int32, (tq, tk), 0)
    cols = kd + lax.broadcasted_iota(jnp.int32, (tq, tk), 1)
    causal = cols <= rows
    ones = jnp.ones((tk, HEAD_DIM), BF16)

    def scores(h, k0):
        sl = slice(h * HEAD_DIM, (h + 1) * HEAD_DIM)
        cr = crow_ref[0, h:h + 1, pl.ds(k0, tk)]
        return _dot_nt(q_ref[:, sl], k_ref[pl.ds(k0, tk), sl]) - cr

    def weighted_values(h, p, k0):
        sl = slice(h * HEAD_DIM, (h + 1) * HEAD_DIM)
        va = jnp.concatenate([v_ref[pl.ds(k0, tk), sl], ones], axis=1)
        pv = jnp.dot(p, va, preferred_element_type=F32)
        return pv[:, :HEAD_DIM], pv[:, HEAD_DIM:HEAD_DIM + 1]

    for h0 in range(0, heads, group):
        hs = range(h0, min(h0 + group, heads))
        carry = []
        for h in hs:
            s = jnp.where(causal, scores(h, kd), -jnp.inf)
            m = jnp.max(s, axis=-1, keepdims=True)
            acc, l = weighted_values(h, jnp.exp2((s - m).astype(BF16)), kd)
            carry += [m, l, acc]

        def body(ki, carry):
            k0 = pl.multiple_of(ki * tk, tk)
            out = []
            for n, h in enumerate(hs):
                m, l, acc = carry[3 * n:3 * n + 3]
                s = scores(h, k0)
                m_new = jnp.maximum(m, jnp.max(s, axis=-1, keepdims=True))
                a = jnp.exp2(m - m_new)
                pv, p_sum = weighted_values(h, jnp.exp2((s - m_new).astype(BF16)), k0)
                out += [m_new, a * l + p_sum, a * acc + pv]
            return tuple(out)

        carry = lax.fori_loop(0, n_full, body, tuple(carry))
        for n, h in enumerate(hs):
            m, l, acc = carry[3 * n:3 * n + 3]
            o_ref[:, h * HEAD_DIM:(h + 1) * HEAD_DIM] = (acc / l).astype(o_ref.dtype)


def _fox(proj, crow, *, B, lp, fw, off_q, off_k, off_v, tq, tk):
    T = proj.shape[0]
    nq = lp // tq
    heads = fw // HEAD_DIM
    assert tk % tq == 0
    return pl.pallas_call(
        functools.partial(_fox_kernel, heads=heads, tq=tq, tk=tk, group=FOX_HEAD_GROUP),
        grid=(B, nq),
        in_specs=[
            pl.BlockSpec((tq, fw), lambda b, i: (b * nq + i, off_q // fw)),
            pl.BlockSpec((lp, fw), lambda b, i: (b, off_k // fw)),
            pl.BlockSpec((lp, fw), lambda b, i: (b, off_v // fw)),
            pl.BlockSpec((1, 8, lp), lambda b, i: (b, 0, 0)),
        ],
        out_specs=pl.BlockSpec((tq, fw), lambda b, i: (b * nq + i, 0)),
        out_shape=jax.ShapeDtypeStruct((T, fw), BF16),
        compiler_params=_params(2),
        name="fox_attention",
    )(proj, proj, proj, crow)


def _lru_kernel(lx_ref, lg_ref, cw_ref, cb_ref, wr_ref, br_ref, wi_ref, bi_ref, lam_ref, o_ref,
                ext_ref, a_ref, u_ref, hc_ref, *, pad, tt, nblk):
    i = pl.program_id(1)

    @pl.when(i == 0)
    def _():
        ext_ref[0:CONV_HALO, :] = jnp.zeros((CONV_HALO, ext_ref.shape[1]), F32)
        hc_ref[...] = jnp.zeros_like(hc_ref)

    @pl.when(i > 0)
    def _():
        ext_ref[0:CONV_HALO, :] = ext_ref[tt:tt + CONV_HALO, :]

    pos = i * tt + lax.broadcasted_iota(jnp.int32, (tt, 1), 0)
    valid = pos >= pad
    ext_ref[CONV_HALO:CONV_HALO + tt, :] = jnp.where(valid, lx_ref[...].astype(F32), 0.0)

    sp = jnp.maximum(-lam_ref[...], 0.0) + jnp.log1p(jnp.exp(-jnp.abs(lam_ref[...])))
    for n in range(nblk):
        sl = slice(n * HEAD_DIM, (n + 1) * HEAD_DIM)
        xe = ext_ref[:, sl]
        acc = cw_ref[0:1, sl] * xe
        for j in range(1, CONV_W):
            acc = cw_ref[j:j + 1, sl] * xe + pltpu.roll(acc, 1, axis=0)
        xc = acc[CONV_HALO:, :] + cb_ref[:, sl]
        xb = xc.astype(BF16)
        r = _sigmoid(jnp.dot(xb, wr_ref[n].astype(BF16), preferred_element_type=F32) + br_ref[:, sl])
        g = _sigmoid(jnp.dot(xb, wi_ref[n].astype(BF16), preferred_element_type=F32) + bi_ref[:, sl])
        log_a = (-LRU_C) * r * sp[:, sl]
        a = jnp.exp(log_a)
        a_ref[:, sl] = a
        u = jnp.sqrt(-jnp.tanh(log_a) * (1.0 + a * a)) * (g * xc)
        u_ref[:, sl] = jnp.where(valid, u, 0.0)

    def body(gidx, h):
        r0 = pl.multiple_of(gidx * 8, 8)
        a8 = a_ref[pl.ds(r0, 8), :]
        u8 = u_ref[pl.ds(r0, 8), :]
        outs = []
        for r in range(8):
            h = a8[r:r + 1] * h + u8[r:r + 1]
            outs.append(h)
        u_ref[pl.ds(r0, 8), :] = jnp.concatenate(outs, axis=0)
        return h

    hc_ref[...] = lax.fori_loop(0, tt // 8, body, hc_ref[...])
    o_ref[...] = (u_ref[...] * _gelu_tanh(lg_ref[...].astype(F32))).astype(o_ref.dtype)


def _lru(proj, conv_w, conv_b, w_r, b_r, w_i, b_i, lam, *, B, lp, lw, off_x, off_g, pad, tt):
    T = proj.shape[0]
    nt = lp // tt
    nblk = lw // HEAD_DIM
    row = lambda b, i: (0, 0)
    return pl.pallas_call(
        functools.partial(_lru_kernel, pad=pad, tt=tt, nblk=nblk),
        grid=(B, nt),
        in_specs=[
            pl.BlockSpec((tt, lw), lambda b, i: (b * nt + i, off_x // lw)),
            pl.BlockSpec((tt, lw), lambda b, i: (b * nt + i, off_g // lw)),
            pl.BlockSpec((CONV_W, lw), row),
            pl.BlockSpec((1, lw), row),
            pl.BlockSpec((nblk, HEAD_DIM, HEAD_DIM), lambda b, i: (0, 0, 0)),
            pl.BlockSpec((1, lw), row),
            pl.BlockSpec((nblk, HEAD_DIM, HEAD_DIM), lambda b, i: (0, 0, 0)),
            pl.BlockSpec((1, lw), row),
            pl.BlockSpec((1, lw), row),
        ],
        out_specs=pl.BlockSpec((tt, lw), lambda b, i: (b * nt + i, 0)),
        out_shape=jax.ShapeDtypeStruct((T, lw), BF16),
        scratch_shapes=[
            pltpu.VMEM((tt + CONV_HALO, lw), F32),
            pltpu.VMEM((tt, lw), F32),
            pltpu.VMEM((tt, lw), F32),
            pltpu.VMEM((1, lw), F32),
        ],
        compiler_params=_params(2),
        name="conv_rglru",
    )(proj, proj, conv_w, conv_b, w_r, b_r, w_i, b_i, lam)


def _gla_kernel(q_ref, k_ref, v_ref, gg_ref, la_ref, ng_ref, o_ref, st_ref, *, pad, tt, heads):
    i = pl.program_id(1)

    @pl.when(i == 0)
    def _():
        st_ref[...] = jnp.zeros_like(st_ref)

    r = lax.broadcasted_iota(jnp.int32, (GLA_CHUNK, GLA_CHUNK), 0)
    c = lax.broadcasted_iota(jnp.int32, (GLA_CHUNK, GLA_CHUNK), 1)
    tri = (r >= c).astype(F32)
    for ci in range(tt // GLA_CHUNK):
        rows = slice(ci * GLA_CHUNK, (ci + 1) * GLA_CHUNK)
        pos = i * tt + ci * GLA_CHUNK + lax.broadcasted_iota(jnp.int32, (GLA_CHUNK, 1), 0)
        valid = pos >= pad
        for hd in range(heads):
            ks = slice(hd * HEAD_DIM, (hd + 1) * HEAD_DIM)
            vs = slice(hd * GLA_DV, (hd + 1) * GLA_DV)
            cs = jnp.dot(tri, la_ref[rows, ks], preferred_element_type=F32, precision=HIGHEST)
            cl = cs[GLA_CHUNK - 1:GLA_CHUNK]
            kdec = jnp.where(valid, k_ref[rows, ks].astype(F32) * jnp.exp(cl - cs), 0.0).astype(BF16)
            ut = lax.dot_general(v_ref[rows, vs], kdec, (((0,), (0,)), ((), ())), preferred_element_type=F32)
            st = st_ref[hd] * jnp.exp(cl) + ut
            st_ref[hd] = st
            o = _dot_nt(q_ref[rows, ks], st.astype(BF16))
            o = o * lax.rsqrt(jnp.mean(o * o, axis=-1, keepdims=True) + RMS_EPS) * ng_ref[:, vs]
            gg = gg_ref[rows, vs].astype(F32)
            o_ref[rows, vs] = (o * (gg * _sigmoid(gg))).astype(o_ref.dtype)


def _gla(proj, proj_g, la, norm_g, *, B, lp, kw, vw, off_q, off_k, off_v, off_g, pad, tt):
    T = proj.shape[0]
    nt = lp // tt
    heads = kw // HEAD_DIM
    return pl.pallas_call(
        functools.partial(_gla_kernel, pad=pad, tt=tt, heads=heads),
        grid=(B, nt),
        in_specs=[
            pl.BlockSpec((tt, kw), lambda b, i: (b * nt + i, off_q // kw)),
            pl.BlockSpec((tt, kw), lambda b, i: (b * nt + i, off_k // kw)),
            pl.BlockSpec((tt, vw), lambda b, i: (b * nt + i, off_v // vw)),
            pl.BlockSpec((tt, vw), lambda b, i: (b * nt + i, off_g // vw)),
            pl.BlockSpec((tt, kw), lambda b, i: (b * nt + i, 0)),
            pl.BlockSpec((1, vw), lambda b, i: (0, 0)),
        ],
        out_specs=pl.BlockSpec((tt, vw), lambda b, i: (b * nt + i, 0)),
        out_shape=jax.ShapeDtypeStruct((T, vw), BF16),
        scratch_shapes=[pltpu.VMEM((heads, GLA_DV, HEAD_DIM), F32)],
        compiler_params=_params(2),
        name="gla_chunked",
    )(proj, proj, proj, proj_g, la, norm_g)


def _merge_kernel(of_ref, ol_ref, og_ref, w_ref, g0_ref, g1_ref, g2_ref, o_ref, wbf_ref, *, fw, lw):
    @pl.when(pl.program_id(1) == 0)
    def _():
        wbf_ref[...] = w_ref[...].astype(BF16)

    y0 = jnp.dot(of_ref[...], wbf_ref[0:fw, :], preferred_element_type=F32)
    y1 = jnp.dot(ol_ref[...], wbf_ref[fw:fw + lw, :], preferred_element_type=F32)
    y2 = jnp.dot(og_ref[...], wbf_ref[fw + lw:, :], preferred_element_type=F32)
    out = (_sigmoid(g0_ref[...].astype(F32)) * y0 + _sigmoid(g1_ref[...].astype(F32)) * y1
           + _sigmoid(g2_ref[...].astype(F32)) * y2)
    o_ref[...] = out.astype(o_ref.dtype)


def _merge(o_fox, o_lru, o_gla, w_branch, layer, proj, *, off_gates, d):
    T, fw = o_fox.shape
    lw = o_lru.shape[1]
    vw = o_gla.shape[1]
    tm = _pick(T, (1056, 1024, 768, 512, 384, 256, 128))
    tn = _pick(math.gcd(d, off_gates), (512, 256, 128))
    gate_spec = lambda b: pl.BlockSpec((tm, tn), lambda j, i: (i, (off_gates + b * d) // tn + j))
    return pl.pallas_call(
        functools.partial(_merge_kernel, fw=fw, lw=lw),
        grid=(d // tn, T // tm),
        in_specs=[
            pl.BlockSpec((tm, fw), lambda j, i: (i, 0)),
            pl.BlockSpec((tm, lw), lambda j, i: (i, 0)),
            pl.BlockSpec((tm, vw), lambda j, i: (i, 0)),
            pl.BlockSpec((None, fw + lw + vw, tn), lambda j, i: (layer, 0, j)),
            gate_spec(0), gate_spec(1), gate_spec(2),
        ],
        out_specs=pl.BlockSpec((tm, tn), lambda j, i: (i, j)),
        out_shape=jax.ShapeDtypeStruct((T, d), BF16),
        scratch_shapes=[pltpu.VMEM((fw + lw + vw, tn), BF16)],
        compiler_params=_params(2),
        name="branch_merge",
    )(o_fox, o_lru, o_gla, w_branch, proj, proj, proj)


def _ln_router_kernel(h_ref, mix_ref, g_ref, b_ref, wr_ref, br_ref, hf_ref, hx_ref, idx_ref, wt_ref, *, alpha):
    y = _layer_norm(alpha * h_ref[...] + mix_ref[...], g_ref[...], b_ref[...])
    hf_ref[...] = y
    hx_ref[...] = _pack_halves(y)
    w = wr_ref[...]
    y_hi, w_hi = y.astype(BF16), w.astype(BF16)
    y_lo = (y - y_hi.astype(F32)).astype(BF16)
    w_lo = (w - w_hi.astype(F32)).astype(BF16)
    logits = (jnp.dot(y_hi, w_hi, preferred_element_type=F32)
              + (jnp.dot(y_lo, w_hi, preferred_element_type=F32) + jnp.dot(y_hi, w_lo, preferred_element_type=F32))
              + br_ref[...])
    lane = lax.broadcasted_iota(jnp.int32, logits.shape, 1).astype(F32)
    idx = jnp.zeros_like(logits)
    vals = []
    for k in range(TOP_K):
        mx = jnp.max(logits, axis=-1, keepdims=True)
        sel = jnp.min(jnp.where(logits == mx, lane, float(logits.shape[1])), axis=-1, keepdims=True)
        vals.append(mx)
        idx = jnp.where(lane == float(k), sel, idx)
        logits = jnp.where(lane == sel, -jnp.inf, logits)
    es = [jnp.exp(v - vals[0]) for v in vals]
    tot = es[0]
    for e in es[1:]:
        tot = tot + e
    wt = jnp.zeros_like(logits)
    for k in range(TOP_K):
        wt = jnp.where(lane == float(k), es[k] / tot, wt)
    idx_ref[...] = idx.astype(jnp.int32)
    wt_ref[...] = wt


def _ln_router(h, mix, g, b, wr_pad, br_pad, *, alpha):
    T, D = h.shape
    tm = _pick(T, (192, 128))
    blk = pl.BlockSpec((tm, D), lambda i: (i, 0))
    row = pl.BlockSpec((1, D), lambda i: (0, 0))
    small = pl.BlockSpec((tm, 128), lambda i: (i, 0))
    return pl.pallas_call(
        functools.partial(_ln_router_kernel, alpha=alpha),
        grid=(T // tm,),
        in_specs=[blk, blk, row, row, pl.BlockSpec((D, 128), lambda i: (0, 0)), pl.BlockSpec((1, 128), lambda i: (0, 0))],
        out_specs=[blk, pl.BlockSpec((tm, D // 2), lambda i: (i, 0)), small, small],
        out_shape=[
            jax.ShapeDtypeStruct((T, D), F32),
            jax.ShapeDtypeStruct((T, D // 2), U32),
            jax.ShapeDtypeStruct((T, 128), jnp.int32),
            jax.ShapeDtypeStruct((T, 128), F32),
        ],
        compiler_params=_params(1),
        name="ln_router",
    )(h, mix, g, b, wr_pad, br_pad)


def _pack_halves(y):
    half = y.shape[1] // 2
    lo = pltpu.bitcast(y[:, :half].astype(BF16).astype(F32), U32)
    hi = pltpu.bitcast(y[:, half:].astype(BF16).astype(F32), U32)
    return lax.shift_right_logical(lo, jnp.uint32(16)) | (hi & jnp.uint32(0xFFFF0000))


def _unpack_halves(w):
    lo = pltpu.bitcast(lax.shift_left(w, jnp.uint32(16)), F32)
    hi = pltpu.bitcast(w & jnp.uint32(0xFFFF0000), F32)
    return lo, hi


def _row_gather(src_hbm, idx_ref, base, buf_ref, sem, n, static=False):
    def start(r):
        pltpu.make_async_copy(src_hbm.at[pl.ds(idx_ref[base + r], 1)], buf_ref.at[pl.ds(r, 1)], sem).start()

    if static:
        for r in range(n):
            start(r)
    else:
        def body(r, carry):
            start(r)
            return carry

        lax.fori_loop(0, n, body, 0, unroll=8)


def _tile_wait(src_hbm, buf_ref, sem):
    pltpu.make_async_copy(src_hbm.at[pl.ds(0, buf_ref.shape[0])], buf_ref, sem).wait()


def _moe_kernel(te_ref, na_ref, first_ref, nxt_ref, tok_ref, h_hbm, wup_hbm, wdn_hbm, bup_ref, sel_ref, bdn_ref,
                y_ref, xbuf_ref, sem_ref, sup_ref, sdn_ref, wup_ref, wdn_ref, wsem_ref, *, tmx, layer):
    i = pl.program_id(0)
    na = na_ref[0]
    cur = lax.rem(i, GATHER_RING)
    ahead = lax.rem(i + GATHER_RING - 1, GATHER_RING)

    def weight_copies(e):
        return (pltpu.make_async_copy(wup_hbm.at[layer, e], sup_ref, wsem_ref.at[0]),
                pltpu.make_async_copy(wdn_hbm.at[layer, e], sdn_ref, wsem_ref.at[1]))

    def start_weights(e):
        up_copy, dn_copy = weight_copies(e)
        up_copy.start(priority=1)
        dn_copy.start()

    for t in range(GATHER_RING - 1):
        @pl.when(jnp.logical_and(i == 0, na > t))
        def _():
            _row_gather(h_hbm, tok_ref, t * tmx, xbuf_ref.at[t], sem_ref.at[t], tmx)

    @pl.when(jnp.logical_and(i == 0, na > 0))
    def _():
        start_weights(te_ref[0])

    @pl.when(jnp.logical_and(i < na, first_ref[i] == 1))
    def _():
        up_copy, dn_copy = weight_copies(te_ref[i])
        up_copy.wait()
        _cast_rows(sup_ref, wup_ref, 256)
        dn_copy.wait()
        _cast_rows(sdn_ref, wdn_ref, 128)

        @pl.when(nxt_ref[i] >= 0)
        def _():
            start_weights(nxt_ref[i])

    @pl.when(i < na)
    def _():
        _tile_wait(h_hbm, xbuf_ref.at[cur], sem_ref.at[cur])
        lo, hi = _unpack_halves(xbuf_ref[cur])
        x = jnp.concatenate([lo.astype(BF16), hi.astype(BF16)], axis=1)
        h = jnp.dot(x, wup_ref[...], preferred_element_type=F32) + bup_ref[...]
        g = jnp.minimum(h, SWIGLU_LIMIT)
        u = jnp.clip(pltpu.roll(h, h.shape[1] - 1, axis=1), -SWIGLU_LIMIT, SWIGLU_LIMIT)
        act = ((u + 1.0) * g * _sigmoid(SWIGLU_ALPHA * g)).astype(BF16)
        act = jnp.dot(act, sel_ref[...], preferred_element_type=F32).astype(BF16)
        y = jnp.dot(act, wdn_ref[...], preferred_element_type=F32) + bdn_ref[...]
        y_ref[...] = _pack_halves(y)

    @pl.when(i + GATHER_RING - 1 < na)
    def _():
        _row_gather(h_hbm, tok_ref, (i + GATHER_RING - 1) * tmx, xbuf_ref.at[ahead], sem_ref.at[ahead], tmx,
                    static=True)

    @pl.when(i >= na)
    def _():
        y_ref[...] = jnp.zeros_like(y_ref)


def _moe(tile_expert, n_active, row_token, hx, w_up, b_up, sel, w_down, b_down, *, tmx, layer):
    T, D = hx.shape[0], 2 * hx.shape[1]
    _, E, _, F2 = w_up.shape
    F = F2 // 2
    P = row_token.shape[0]
    ntiles = P // tmx
    tiles = jnp.arange(ntiles, dtype=jnp.int32)
    first = jnp.where(tiles == 0, 1, (tile_expert != jnp.roll(tile_expert, 1)).astype(jnp.int32))
    j = jnp.sum((tile_expert[None, :] <= tile_expert[:, None]).astype(jnp.int32), axis=1)
    nxt = jnp.sum(jnp.where(tiles[None, :] == j[:, None], tile_expert[None, :] + 1, 0), axis=1) - 1

    return pl.pallas_call(
        functools.partial(_moe_kernel, tmx=tmx, layer=layer),
        grid_spec=pltpu.PrefetchScalarGridSpec(
            num_scalar_prefetch=5,
            grid=(ntiles,),
            in_specs=[
                pl.BlockSpec(memory_space=pl.ANY),
                pl.BlockSpec(memory_space=pl.ANY),
                pl.BlockSpec(memory_space=pl.ANY),
                pl.BlockSpec((None, 1, F2), lambda i, te, *_: (te[i], 0, 0)),
                pl.BlockSpec((F2, F), lambda i, *_: (0, 0)),
                pl.BlockSpec((None, 1, D), lambda i, te, *_: (te[i], 0, 0)),
            ],
            out_specs=pl.BlockSpec((tmx, D // 2), lambda i, *_: (i, 0)),
            scratch_shapes=[
                pltpu.VMEM((GATHER_RING, tmx, D // 2), U32), pltpu.SemaphoreType.DMA((GATHER_RING,)),
                pltpu.VMEM((D, F2), F32), pltpu.VMEM((F, D), F32),
                pltpu.VMEM((D, F2), BF16), pltpu.VMEM((F, D), BF16), pltpu.SemaphoreType.DMA((2,)),
            ],
        ),
        out_shape=jax.ShapeDtypeStruct((P, D // 2), U32),
        compiler_params=_params(1),
        name="moe_experts",
    )(tile_expert, n_active, first, nxt, row_token, hx, w_up, w_down, b_up, sel, b_down)


def _combine_kernel(pos_ref, y_hbm, h_ref, w_ref, g_ref, b_ref, *rest, alpha, tc, nt, final):
    if final:
        out_ref, buf_ref, sem_ref = rest
    else:
        hf_ref, hb_ref, buf_ref, sem_ref = rest
    i = pl.program_id(0)
    cur = lax.rem(i, GATHER_RING)
    ahead = lax.rem(i + GATHER_RING - 1, GATHER_RING)

    def start(tile, slot, static):
        for k in range(TOP_K):
            _row_gather(y_hbm, pos_ref, (k * nt + tile) * tc, buf_ref.at[slot, k], sem_ref.at[slot], tc, static=static)

    @pl.when(i == 0)
    def _():
        for t in range(min(GATHER_RING - 1, nt)):
            start(t, t, False)

    for k in range(TOP_K):
        _tile_wait(y_hbm, buf_ref.at[cur, k], sem_ref.at[cur])
    lo = hi = None
    for k in range(TOP_K):
        l, h = _unpack_halves(buf_ref[cur, k])
        wk = w_ref[:, k:k + 1]
        lo = wk * l if lo is None else lo + wk * l
        hi = wk * h if hi is None else hi + wk * h
    ffn = jnp.concatenate([lo, hi], axis=1)
    y = _layer_norm(alpha * h_ref[...] + ffn, g_ref[...], b_ref[...])
    if final:
        out_ref[...] = y
    else:
        hf_ref[...] = y
        hb_ref[...] = y.astype(BF16)

    @pl.when(i + GATHER_RING - 1 < nt)
    def _():
        start(i + GATHER_RING - 1, ahead, True)


def _combine(pos, ys, hf, top_w, g, b, *, alpha, lp, final):
    T, D = hf.shape
    tc = ROW_TILE
    nt = T // tc
    ntb = lp // tc
    if final:
        out_specs = pl.BlockSpec((tc, D), lambda i, pos: ((i // ntb) * (ntb - 1) + jnp.maximum(i % ntb - 1, 0), 0))
        out_shape = jax.ShapeDtypeStruct(((T // lp) * (lp - tc), D), F32)
    else:
        out_specs = [pl.BlockSpec((tc, D), lambda i, pos: (i, 0)), pl.BlockSpec((tc, D), lambda i, pos: (i, 0))]
        out_shape = [jax.ShapeDtypeStruct((T, D), F32), jax.ShapeDtypeStruct((T, D), BF16)]
    grid_spec = pltpu.PrefetchScalarGridSpec(
        num_scalar_prefetch=1,
        grid=(nt,),
        in_specs=[
            pl.BlockSpec(memory_space=pl.ANY),
            pl.BlockSpec((tc, D), lambda i, pos: (i, 0)),
            pl.BlockSpec((tc, 128), lambda i, pos: (i, 0)),
            pl.BlockSpec((1, D), lambda i, pos: (0, 0)),
            pl.BlockSpec((1, D), lambda i, pos: (0, 0)),
        ],
        out_specs=out_specs,
        scratch_shapes=[pltpu.VMEM((GATHER_RING, TOP_K, tc, D // 2), U32), pltpu.SemaphoreType.DMA((GATHER_RING,))],
    )
    return pl.pallas_call(
        functools.partial(_combine_kernel, alpha=alpha, tc=tc, nt=nt, final=final),
        grid_spec=grid_spec,
        out_shape=out_shape,
        compiler_params=_params(1),
        name="moe_combine_ln",
    )(pos, ys, hf, top_w, g, b)


def _route_kernel(idx_ref, pos_ref, te_ref, na_ref, cnt_ref, pst_ref, *, pad, tr, tmx, n_experts, spare_row):
    p = pl.program_id(0)
    i = pl.program_id(2)
    first = jnp.logical_and(pl.program_id(1) == 0, i == 0)

    @pl.when(jnp.logical_and(p == 0, first))
    def _():
        cnt_ref[...] = jnp.zeros_like(cnt_ref)

    @pl.when(jnp.logical_and(p == 1, first))
    def _():
        cnt = cnt_ref[...]
        padded = jnp.floor((cnt + (tmx - 1.0)) * (1.0 / tmx)) * tmx
        r = lax.broadcasted_iota(jnp.int32, (128, 128), 0)
        c = lax.broadcasted_iota(jnp.int32, (128, 128), 1)
        pst = jnp.dot(padded, (r < c).astype(F32), preferred_element_type=F32, precision=HIGHEST)
        pst_ref[...] = pst
        cnt_ref[...] = jnp.zeros_like(cnt_ref)
        tile_end = (pst + padded) * (1.0 / tmx)
        nt_pad = te_ref.shape[0]
        t = lax.broadcasted_iota(jnp.int32, (nt_pad, 128), 0).astype(F32)
        lane = lax.broadcasted_iota(jnp.int32, (nt_pad, 128), 1)
        is_expert = lane < n_experts
        te = jnp.sum(jnp.where(jnp.logical_and(tile_end <= t, is_expert), 1.0, 0.0), axis=-1, keepdims=True)
        lane1 = lax.broadcasted_iota(jnp.int32, (1, 128), 1).astype(F32)
        e_last = jnp.max(jnp.where(cnt > 0.0, lane1, 0.0), axis=-1, keepdims=True)
        te_ref[...] = jnp.broadcast_to(jnp.minimum(te, e_last), (nt_pad, 128)).astype(jnp.int32)
        na_ref[...] = jnp.broadcast_to(jnp.max(tile_end, axis=-1, keepdims=True), (1, 128)).astype(jnp.int32)

    idx = idx_ref[...]
    lane = lax.broadcasted_iota(jnp.int32, (tr, 128), 1)
    valid = (i * tr + lax.broadcasted_iota(jnp.int32, (tr, 1), 0)) >= pad
    onehot = jnp.zeros((tr, 128), F32)
    for k in range(TOP_K):
        onehot = onehot + jnp.where(lane == idx[:, k:k + 1], 1.0, 0.0)
    onehot = jnp.where(valid, onehot, 0.0)
    r = lax.broadcasted_iota(jnp.int32, (tr, tr), 0)
    c = lax.broadcasted_iota(jnp.int32, (tr, tr), 1)
    rank = jnp.dot((r > c).astype(BF16), onehot.astype(BF16), preferred_element_type=F32) + cnt_ref[...]
    cnt_ref[...] = cnt_ref[...] + jnp.sum(onehot, axis=0, keepdims=True)

    @pl.when(p == 0)
    def _():
        pos_ref[...] = jnp.zeros_like(pos_ref)

    @pl.when(p == 1)
    def _():
        dest = rank + pst_ref[...]
        out = jnp.zeros((tr, 128), F32)
        for k in range(TOP_K):
            d = jnp.sum(jnp.where(lane == idx[:, k:k + 1], dest, 0.0), axis=-1, keepdims=True)
            out = jnp.where(lane == k, jnp.where(valid, d, float(spare_row)), out)
        pos_ref[...] = out.astype(jnp.int32)


def _route(top_idx, *, B, lp, pad, tmx, n_experts, n_rows):
    T = top_idx.shape[0]
    tr = _pick(lp, (384, 256, 128))
    nt = lp // tr
    nt_pad = -(-(n_rows // tmx) // 8) * 8
    pos, te, na = pl.pallas_call(
        functools.partial(_route_kernel, pad=pad, tr=tr, tmx=tmx, n_experts=n_experts, spare_row=n_rows - 1),
        grid=(2, B, nt),
        in_specs=[pl.BlockSpec((tr, 128), lambda p, b, i: (b * nt + i, 0))],
        out_specs=[
            pl.BlockSpec((tr, 128), lambda p, b, i: (p * (b * nt + i), 0)),
            pl.BlockSpec((nt_pad, 128), lambda p, b, i: (0, 0)),
            pl.BlockSpec((1, 128), lambda p, b, i: (0, 0)),
        ],
        out_shape=[
            jax.ShapeDtypeStruct((T, 128), jnp.int32),
            jax.ShapeDtypeStruct((nt_pad, 128), jnp.int32),
            jax.ShapeDtypeStruct((1, 128), jnp.int32),
        ],
        scratch_shapes=[pltpu.VMEM((1, 128), F32), pltpu.VMEM((1, 128), F32)],
        compiler_params=_params(3),
        name="route_rank",
    )(top_idx)
    pos_flat = pos[:, :TOP_K].T.reshape(-1)
    return pos_flat, te[:n_rows // tmx, 0], na[0, :1]


def _row_token_kernel(pos_ref, rt_ref, *, t):
    def zero(r, carry):
        rt_ref[r] = 0
        return carry

    lax.fori_loop(0, rt_ref.shape[0], zero, 0, unroll=8)
    for k in range(TOP_K):
        def body(tok, carry):
            rt_ref[pos_ref[k * t + tok]] = tok
            return carry

        lax.fori_loop(0, t, body, 0, unroll=8)


def _row_token(pos_flat, *, t, n_rows):
    return pl.pallas_call(
        functools.partial(_row_token_kernel, t=t),
        grid_spec=pltpu.PrefetchScalarGridSpec(
            num_scalar_prefetch=1, grid=(1,), in_specs=[],
            out_specs=pl.BlockSpec(memory_space=pltpu.SMEM)),
        out_shape=jax.ShapeDtypeStruct((n_rows,), jnp.int32),
        compiler_params=_params(1),
        name="route_row_token",
    )(pos_flat)


def _segments(fw, fh, lw, kw, vw, rank, d):
    names = ("fq", "fk", "fv", "ff", "lx", "lg", "gq", "gk", "gv", "ga", "gg", "gate0", "gate1", "gate2")
    widths = (fw, fw, fw, fh, lw, lw, kw, kw, vw, rank, vw, d, d, d)
    segs, off = {}, 0
    for n, w in zip(names, widths):
        segs[n] = (off, w)
        off += w
    return segs, off


def kernel(x, meta_tokens, emb_ln_g, emb_ln_b, w_in, b_in, conv_w, conv_b, lru_w_r, lru_b_r, lru_w_i, lru_b_i,
           lru_lambda, gla_w_alpha, gla_b_alpha, gla_norm_g, w_branch, w_out, b_out, ln1_g, ln1_b, w_router,
           b_router, w_up, b_up, w_down, b_down, ln2_g, ln2_b):
    B, S, D = x.shape
    n_meta = meta_tokens.shape[0]
    depth = w_in.shape[0]
    L = S + n_meta
    pad = (-L) % ROW_TILE
    lp = L + pad
    assert pad + n_meta == ROW_TILE and S % ROW_TILE == 0
    T = B * lp

    lw = conv_w.shape[2]
    rank, kw = gla_w_alpha.shape[1:]
    vw = gla_norm_g.shape[1]
    n_experts = w_router.shape[2]
    fexp = w_up.shape[3] // 2
    in_cols = w_in.shape[2]
    fh = (in_cols - 2 * lw - 2 * kw - 2 * vw - rank - 3 * D) // (3 * HEAD_DIM + 1)
    fw = fh * HEAD_DIM
    segs, total = _segments(fw, fh, lw, kw, vw, rank, D)
    assert total == in_cols and fh <= 8 and fh + rank <= 128
    alpha = (2.0 * depth) ** 0.25

    windows = (("fq", "fk", "fv"), ("lx", "lg", "gq", "gk", "gv"), ("gg", "gate0", "gate1", "gate2"))
    qscale = HEAD_DIM ** -0.5
    off, win_start, win_scale = {}, [], []
    for names in windows:
        start = segs[names[0]][0]
        for n in names:
            off[n] = segs[n][0] - start
            assert off[n] % segs[n][1] == 0 or n.startswith("gate")
        win_start.append(start)
        scales = {"fq": qscale * LOG2E, "gq": qscale}
        win_scale.append(jnp.concatenate(
            [jnp.full((segs[n][1],), scales.get(n, 1.0), F32) for n in names]).reshape(1, -1))

    def cols(a, n):
        o, w = segs[n]
        return a[..., o:o + w]

    tt = _pick(lp, (384, 256, 128))
    tmx = MOE_ROW_TILE if T >= 4096 else 128
    n_rows = (-(-(B * L * TOP_K + n_experts * (tmx - 1)) // tmx) + 1) * tmx

    head = jnp.concatenate([jnp.zeros((pad, D), F32), meta_tokens.astype(F32)], axis=0)
    hf, hb = _embed_ln(x, head, emb_ln_g.reshape(1, D), emb_ln_b.reshape(1, D), pad=pad, lp=lp)
    sel = (jnp.arange(2 * fexp)[:, None] == 2 * jnp.arange(fexp)[None, :]).astype(BF16)

    w_in_t = jnp.swapaxes(w_in, 1, 2)

    def rows(l, n):
        o, w = segs[n]
        return w_in_t[l, o:o + w]

    for l in range(depth):
        w_small = jnp.concatenate(
            [rows(l, "ff"), jnp.zeros((8 - fh, D), F32), rows(l, "ga"), jnp.zeros((128 - 8 - rank, D), F32)], axis=0)
        b_small = jnp.concatenate(
            [cols(b_in[l], "ff"), jnp.zeros((8 - fh,), F32), cols(b_in[l], "ga"),
             jnp.zeros((128 - 8 - rank,), F32)]).reshape(1, 128)
        wa_ext = jnp.zeros((128, kw), F32).at[8:8 + rank].set(gla_w_alpha[l])

        pa, pb, pc = [
            _in_proj(hb, w_in_t, l, b_in[l][s:s + sc.shape[1]].reshape(1, -1), sc, start=s, name=f"in_proj_{k}")
            for k, (s, sc) in enumerate(zip(win_start, win_scale))]
        crow, la = _prep(hb, w_small, b_small, wa_ext, gla_b_alpha[l].reshape(1, kw), B=B, lp=lp, pad=pad, tt=tt)
        o_fox = _fox(pa, crow, B=B, lp=lp, fw=fw, off_q=off["fq"], off_k=off["fk"], off_v=off["fv"], tq=FOX_Q_TILE,
                     tk=tt)
        o_lru = _lru(pb, conv_w[l], conv_b[l].reshape(1, lw), lru_w_r[l], lru_b_r[l].reshape(1, lw), lru_w_i[l],
                     lru_b_i[l].reshape(1, lw), lru_lambda[l].reshape(1, lw), B=B, lp=lp, lw=lw, off_x=off["lx"],
                     off_g=off["lg"], pad=pad, tt=tt)
        o_gla = _gla(pb, pc, la, gla_norm_g[l].reshape(1, vw), B=B, lp=lp, kw=kw, vw=vw, off_q=off["gq"],
                     off_k=off["gk"], off_v=off["gv"], off_g=off["gg"], pad=pad, tt=tt)
        merged = _merge(o_fox, o_lru, o_gla, w_branch, l, pc, off_gates=off["gate0"], d=D)
        mix = _matmul(merged, w_out, l, b_out[l].reshape(1, D), jnp.ones((1, D), F32), F32, "out_proj")

        wr_pad = jnp.zeros((D, 128), F32).at[:, :n_experts].set(w_router[l])
        br_pad = jnp.full((1, 128), -MASKED_KEY_BIAS, F32).at[0, :n_experts].set(b_router[l])
        hf, hx, top_idx, top_w = _ln_router(hf, mix, ln1_g[l].reshape(1, D), ln1_b[l].reshape(1, D), wr_pad, br_pad,
                                            alpha=alpha)
        pos, tile_expert, n_active = _route(top_idx, B=B, lp=lp, pad=pad, tmx=tmx, n_experts=n_experts, n_rows=n_rows)
        row_token = _row_token(pos, t=T, n_rows=n_rows)
        ys = _moe(tile_expert, n_active, row_token, hx, w_up, b_up[l].reshape(n_experts, 1, 2 * fexp), sel, w_down,
                  b_down[l].reshape(n_experts, 1, D), tmx=tmx, layer=l)
        final = l == depth - 1
        res = _combine(pos, ys, hf, top_w, ln2_g[l].reshape(1, D), ln2_b[l].reshape(1, D), alpha=alpha, lp=lp,
                       final=final)
        if final:
            return res.reshape(B, S, D)
        hf, hb = res
```

```python
import functools
import math

import jax
import jax.numpy as jnp
from jax import lax
from jax.experimental import pallas as pl
from jax.experimental.pallas import tpu as pltpu

F32 = jnp.float32
BF16 = jnp.bfloat16
U32 = jnp.uint32
HIGHEST = lax.Precision.HIGHEST

ROW_TILE = 128
HEAD_DIM = 128
GLA_DV = 256
GLA_CHUNK = 64
CONV_W = 4
CONV_HALO = 8
LN_EPS = 1e-5
RMS_EPS = 1e-6
LRU_C = 8.0
GLA_TAU = 16.0
SWIGLU_LIMIT = 7.0
SWIGLU_ALPHA = 1.702
TOP_K = 4
LOG2E = 1.4426950408889634
MASKED_KEY_BIAS = 1e30
MOE_ROW_TILE = 256
FOX_HEAD_GROUP = 4
FOX_Q_TILE = 384
GATHER_RING = 4
VMEM_LIMIT = 56 * 1024 * 1024


def _pick(n, candidates):
    for c in candidates:
        if n % c == 0:
            return c
    raise ValueError(f"no tile in {candidates} divides {n}")


def _params(n_axes, vmem=VMEM_LIMIT):
    return pltpu.CompilerParams(dimension_semantics=("arbitrary",) * n_axes, vmem_limit_bytes=vmem)


def _layer_norm(v, g, b):
    mu = jnp.mean(v, axis=-1, keepdims=True)
    d = v - mu
    var = jnp.mean(d * d, axis=-1, keepdims=True)
    return d * lax.rsqrt(var + LN_EPS) * g + b


def _log_sigmoid(x):
    return jnp.minimum(x, 0.0) - jnp.log1p(jnp.exp(-jnp.abs(x)))


def _sigmoid(x):
    return 0.5 * jnp.tanh(0.5 * x) + 0.5


def _gelu_tanh(x):
    return 0.5 * x * (1.0 + jnp.tanh(0.7978845608028654 * (x + 0.044715 * (x * x * x))))


def _embed_ln_kernel(x_ref, head_ref, g_ref, b_ref, hf_ref, hb_ref, *, pad):
    i = pl.program_id(1)

    @pl.when(i == 0)
    def _():
        y = _layer_norm(head_ref[...], g_ref[...], b_ref[...])
        rows = lax.broadcasted_iota(jnp.int32, (ROW_TILE, 1), 0)
        y = jnp.where(rows >= pad, y, 0.0)
        hf_ref[...] = y
        hb_ref[...] = y.astype(BF16)

    @pl.when(i > 0)
    def _():
        y = _layer_norm(x_ref[0], g_ref[...], b_ref[...])
        hf_ref[...] = y
        hb_ref[...] = y.astype(BF16)


def _embed_ln(x, head, g, b, *, pad, lp):
    B, S, D = x.shape
    nt = lp // ROW_TILE
    T = B * lp
    return pl.pallas_call(
        functools.partial(_embed_ln_kernel, pad=pad),
        grid=(B, nt),
        in_specs=[
            pl.BlockSpec((1, ROW_TILE, D), lambda b, i: (b, jnp.maximum(i - 1, 0), 0)),
            pl.BlockSpec((ROW_TILE, D), lambda b, i: (0, 0)),
            pl.BlockSpec((1, D), lambda b, i: (0, 0)),
            pl.BlockSpec((1, D), lambda b, i: (0, 0)),
        ],
        out_specs=[
            pl.BlockSpec((ROW_TILE, D), lambda b, i: (b * nt + i, 0)),
            pl.BlockSpec((ROW_TILE, D), lambda b, i: (b * nt + i, 0)),
        ],
        out_shape=[jax.ShapeDtypeStruct((T, D), F32), jax.ShapeDtypeStruct((T, D), BF16)],
        compiler_params=_params(2),
        name="embed_ln",
    )(x, head, g, b)


def _dot_nt(a, b):
    return lax.dot_general(a, b, (((1,), (1,)), ((), ())), preferred_element_type=F32)


def _cast_rows(src_ref, dst_ref, chunk):
    def body(c, carry):
        r0 = pl.multiple_of(c * chunk, chunk)
        dst_ref[pl.ds(r0, chunk), :] = src_ref[pl.ds(r0, chunk), :].astype(BF16)
        return carry

    lax.fori_loop(0, src_ref.shape[0] // chunk, body, 0)


def _mm_kernel(a_ref, w_ref, b_ref, s_ref, o_ref, *scratch):
    if scratch:
        (wbf_ref,) = scratch

        @pl.when(pl.program_id(1) == 0)
        def _():
            wbf_ref[...] = w_ref[...].astype(BF16)

        w = wbf_ref[...]
    else:
        w = w_ref[...]
    acc = jnp.dot(a_ref[...], w, preferred_element_type=F32)
    o_ref[...] = ((acc + b_ref[...]) * s_ref[...]).astype(o_ref.dtype)


def _matmul(a, w, layer, bias, scale, out_dtype, name):
    M, K = a.shape
    N = w.shape[2]
    tm = _pick(M, (1056, 1024, 768, 512, 384, 256, 128))
    tn = _pick(N, (512, 256, 128))
    scratch = [pltpu.VMEM((K, tn), BF16)] if w.dtype != BF16 else []
    return pl.pallas_call(
        _mm_kernel,
        grid=(N // tn, M // tm),
        in_specs=[
            pl.BlockSpec((tm, K), lambda j, i: (i, 0)),
            pl.BlockSpec((None, K, tn), lambda j, i: (layer, 0, j)),
            pl.BlockSpec((1, tn), lambda j, i: (0, j)),
            pl.BlockSpec((1, tn), lambda j, i: (0, j)),
        ],
        out_specs=pl.BlockSpec((tm, tn), lambda j, i: (i, j)),
        out_shape=jax.ShapeDtypeStruct((M, N), out_dtype),
        scratch_shapes=scratch,
        compiler_params=_params(2),
        name=name,
    )(a, w, bias, scale)


def _mm_nt_kernel(a_ref, wt_ref, b_ref, s_ref, o_ref, wbf_ref):
    @pl.when(pl.program_id(1) == 0)
    def _():
        _cast_rows(wt_ref.at[0], wbf_ref, 64)

    acc = _dot_nt(a_ref[...], wbf_ref[...])
    o_ref[...] = ((acc + b_ref[...]) * s_ref[...]).astype(o_ref.dtype)


def _in_proj(a, wt, layer, bias, scale, *, start, name):
    M, K = a.shape
    n = bias.shape[1]
    assert start % 8 == 0
    tm = _pick(M, (1056, 1024, 768, 512, 384, 256, 128))
    tn = _pick(n, (512, 256, 128))
    return pl.pallas_call(
        _mm_nt_kernel,
        grid=(n // tn, M // tm),
        in_specs=[
            pl.BlockSpec((tm, K), lambda j, i: (i, 0)),
            pl.BlockSpec((pl.Element(1), pl.Element(tn), pl.Element(K)),
                         lambda j, i: (layer, 8 * (start // 8 + j * (tn // 8)), 0)),
            pl.BlockSpec((1, tn), lambda j, i: (0, j)),
            pl.BlockSpec((1, tn), lambda j, i: (0, j)),
        ],
        out_specs=pl.BlockSpec((tm, tn), lambda j, i: (i, j)),
        out_shape=jax.ShapeDtypeStruct((M, n), BF16),
        scratch_shapes=[pltpu.VMEM((tn, K), BF16)],
        compiler_params=_params(2),
        name=name,
    )(a, wt, bias, scale)


def _prep_kernel(hb_ref, ws_ref, bs_ref, wa_ref, ba_ref, crow_ref, la_ref, carry_ref, *, pad, tt):
    i = pl.program_id(1)

    @pl.when(i == 0)
    def _():
        carry_ref[...] = jnp.zeros_like(carry_ref)

    z = _dot_nt(hb_ref[...], ws_ref[...].astype(BF16)) + bs_ref[...]
    pos = i * tt + lax.broadcasted_iota(jnp.int32, (tt, 1), 0)
    valid = pos >= pad
    la = _log_sigmoid(jnp.dot(z, wa_ref[...], preferred_element_type=F32, precision=HIGHEST) + ba_ref[...])
    la_ref[...] = jnp.where(valid, la * (1.0 / GLA_TAU), 0.0)
    lf = jnp.where(valid, _log_sigmoid(z), 0.0)

    r = lax.broadcasted_iota(jnp.int32, (ROW_TILE, ROW_TILE), 0)
    c = lax.broadcasted_iota(jnp.int32, (ROW_TILE, ROW_TILE), 1)
    tri = (r >= c).astype(F32)
    carry = carry_ref[...]
    for sb in range(tt // ROW_TILE):
        rows = slice(sb * ROW_TILE, (sb + 1) * ROW_TILE)
        cs = jnp.dot(tri, lf[rows], preferred_element_type=F32, precision=HIGHEST) + carry
        carry = cs[ROW_TILE - 1:ROW_TILE]
        posr = i * tt + sb * ROW_TILE + lax.broadcasted_iota(jnp.int32, (1, ROW_TILE), 1)
        crow_ref[0, :, rows] = jnp.where(posr >= pad, cs.T[0:8] * LOG2E, MASKED_KEY_BIAS)
    carry_ref[...] = carry


def _prep(hb, w_small, b_small, wa_ext, ba, *, B, lp, pad, tt):
    T, D = hb.shape
    KW = wa_ext.shape[1]
    nt = lp // tt
    return pl.pallas_call(
        functools.partial(_prep_kernel, pad=pad, tt=tt),
        grid=(B, nt),
        in_specs=[
            pl.BlockSpec((tt, D), lambda b, i: (b * nt + i, 0)),
            pl.BlockSpec((128, D), lambda b, i: (0, 0)),
            pl.BlockSpec((1, 128), lambda b, i: (0, 0)),
            pl.BlockSpec((128, KW), lambda b, i: (0, 0)),
            pl.BlockSpec((1, KW), lambda b, i: (0, 0)),
        ],
        out_specs=[
            pl.BlockSpec((1, 8, tt), lambda b, i: (b, 0, i)),
            pl.BlockSpec((tt, KW), lambda b, i: (b * nt + i, 0)),
        ],
        out_shape=[
            jax.ShapeDtypeStruct((B, 8, lp), F32),
            jax.ShapeDtypeStruct((T, KW), F32),
        ],
        scratch_shapes=[pltpu.VMEM((1, 128), F32)],
        compiler_params=_params(2),
        name="gate_prep",
    )(hb, w_small, b_small, wa_ext, ba)


def _fox_kernel(q_ref, k_ref, v_ref, crow_ref, o_ref, *, heads, tq, tk, group):
    qi = pl.program_id(1)
    n_full = (qi * tq) // tk
    kd = pl.multiple_of(n_full * tk, tk)
    rows = qi * tq + lax.broadcasted_iota(jnp.int32, (tq, tk), 0)
    cols = kd + lax.broadcasted_iota(jnp.int32, (tq, tk), 1)
    causal = cols <= rows
    ones = jnp.ones((tk, HEAD_DIM), BF16)

    def scores(h, k0):
        sl = slice(h * HEAD_DIM, (h + 1) * HEAD_DIM)
        cr = crow_ref[0, h:h + 1, pl.ds(k0, tk)]
        return _dot_nt(q_ref[:, sl], k_ref[pl.ds(k0, tk), sl]) - cr

    def weighted_values(h, p, k0):
        sl = slice(h * HEAD_DIM, (h + 1) * HEAD_DIM)
        va = jnp.concatenate([v_ref[pl.ds(k0, tk), sl], ones], axis=1)
        pv = jnp.dot(p, va, preferred_element_type=F32)
        return pv[:, :HEAD_DIM], pv[:, HEAD_DIM:HEAD_DIM + 1]

    for h0 in range(0, heads, group):
        hs = range(h0, min(h0 + group, heads))
        carry = []
        for h in hs:
            s = jnp.where(causal, scores(h, kd), -jnp.inf)
            m = jnp.max(s, axis=-1, keepdims=True)
            acc, l = weighted_values(h, jnp.exp2((s - m).astype(BF16)), kd)
            carry += [m, l, acc]

        def body(ki, carry):
            k0 = pl.multiple_of(ki * tk, tk)
            out = []
            for n, h in enumerate(hs):
                m, l, acc = carry[3 * n:3 * n + 3]
                s = scores(h, k0)
                m_new = jnp.maximum(m, jnp.max(s, axis=-1, keepdims=True))
                a = jnp.exp2(m - m_new)
                pv, p_sum = weighted_values(h, jnp.exp2((s - m_new).astype(BF16)), k0)
                out += [m_new, a * l + p_sum, a * acc + pv]
            return tuple(out)

        carry = lax.fori_loop(0, n_full, body, tuple(carry))
        for n, h in enumerate(hs):
            m, l, acc = carry[3 * n:3 * n + 3]
            o_ref[:, h * HEAD_DIM:(h + 1) * HEAD_DIM] = (acc / l).astype(o_ref.dtype)


def _fox(proj, crow, *, B, lp, fw, off_q, off_k, off_v, tq, tk):
    T = proj.shape[0]
    nq = lp // tq
    heads = fw // HEAD_DIM
    assert tk % tq == 0
    return pl.pallas_call(
        functools.partial(_fox_kernel, heads=heads, tq=tq, tk=tk, group=FOX_HEAD_GROUP),
        grid=(B, nq),
        in_specs=[
            pl.BlockSpec((tq, fw), lambda b, i: (b * nq + i, off_q // fw)),
            pl.BlockSpec((lp, fw), lambda b, i: (b, off_k // fw)),
            pl.BlockSpec((lp, fw), lambda b, i: (b, off_v // fw)),
            pl.BlockSpec((1, 8, lp), lambda b, i: (b, 0, 0)),
        ],
        out_specs=pl.BlockSpec((tq, fw), lambda b, i: (b * nq + i, 0)),
        out_shape=jax.ShapeDtypeStruct((T, fw), BF16),
        compiler_params=_params(2),
        name="fox_attention",
    )(proj, proj, proj, crow)


def _lru_kernel(lx_ref, lg_ref, cw_ref, cb_ref, wr_ref, br_ref, wi_ref, bi_ref, lam_ref, o_ref,
                ext_ref, a_ref, u_ref, hc_ref, *, pad, tt, nblk):
    i = pl.program_id(1)

    @pl.when(i == 0)
    def _():
        ext_ref[0:CONV_HALO, :] = jnp.zeros((CONV_HALO, ext_ref.shape[1]), F32)
        hc_ref[...] = jnp.zeros_like(hc_ref)

    @pl.when(i > 0)
    def _():
        ext_ref[0:CONV_HALO, :] = ext_ref[tt:tt + CONV_HALO, :]

    pos = i * tt + lax.broadcasted_iota(jnp.int32, (tt, 1), 0)
    valid = pos >= pad
    ext_ref[CONV_HALO:CONV_HALO + tt, :] = jnp.where(valid, lx_ref[...].astype(F32), 0.0)

    sp = jnp.maximum(-lam_ref[...], 0.0) + jnp.log1p(jnp.exp(-jnp.abs(lam_ref[...])))
    for n in range(nblk):
        sl = slice(n * HEAD_DIM, (n + 1) * HEAD_DIM)
        xe = ext_ref[:, sl]
        acc = cw_ref[0:1, sl] * xe
        for j in range(1, CONV_W):
            acc = cw_ref[j:j + 1, sl] * xe + pltpu.roll(acc, 1, axis=0)
        xc = acc[CONV_HALO:, :] + cb_ref[:, sl]
        xb = xc.astype(BF16)
        r = _sigmoid(jnp.dot(xb, wr_ref[n].astype(BF16), preferred_element_type=F32) + br_ref[:, sl])
        g = _sigmoid(jnp.dot(xb, wi_ref[n].astype(BF16), preferred_element_type=F32) + bi_ref[:, sl])
        log_a = (-LRU_C) * r * sp[:, sl]
        a = jnp.exp(log_a)
        a_ref[:, sl] = a
        u = jnp.sqrt(-jnp.tanh(log_a) * (1.0 + a * a)) * (g * xc)
        u_ref[:, sl] = jnp.where(valid, u, 0.0)

    def body(gidx, h):
        r0 = pl.multiple_of(gidx * 8, 8)
        a8 = a_ref[pl.ds(r0, 8), :]
        u8 = u_ref[pl.ds(r0, 8), :]
        outs = []
        for r in range(8):
            h = a8[r:r + 1] * h + u8[r:r + 1]
            outs.append(h)
        u_ref[pl.ds(r0, 8), :] = jnp.concatenate(outs, axis=0)
        return h

    hc_ref[...] = lax.fori_loop(0, tt // 8, body, hc_ref[...])
    o_ref[...] = (u_ref[...] * _gelu_tanh(lg_ref[...].astype(F32))).astype(o_ref.dtype)


def _lru(proj, conv_w, conv_b, w_r, b_r, w_i, b_i, lam, *, B, lp, lw, off_x, off_g, pad, tt):
    T = proj.shape[0]
    nt = lp // tt
    nblk = lw // HEAD_DIM
    row = lambda b, i: (0, 0)
    return pl.pallas_call(
        functools.partial(_lru_kernel, pad=pad, tt=tt, nblk=nblk),
        grid=(B, nt),
        in_specs=[
            pl.BlockSpec((tt, lw), lambda b, i: (b * nt + i, off_x // lw)),
            pl.BlockSpec((tt, lw), lambda b, i: (b * nt + i, off_g // lw)),
            pl.BlockSpec((CONV_W, lw), row),
            pl.BlockSpec((1, lw), row),
            pl.BlockSpec((nblk, HEAD_DIM, HEAD_DIM), lambda b, i: (0, 0, 0)),
            pl.BlockSpec((1, lw), row),
            pl.BlockSpec((nblk, HEAD_DIM, HEAD_DIM), lambda b, i: (0, 0, 0)),
            pl.BlockSpec((1, lw), row),
            pl.BlockSpec((1, lw), row),
        ],
        out_specs=pl.BlockSpec((tt, lw), lambda b, i: (b * nt + i, 0)),
        out_shape=jax.ShapeDtypeStruct((T, lw), BF16),
        scratch_shapes=[
            pltpu.VMEM((tt + CONV_HALO, lw), F32),
            pltpu.VMEM((tt, lw), F32),
            pltpu.VMEM((tt, lw), F32),
            pltpu.VMEM((1, lw), F32),
        ],
        compiler_params=_params(2),
        name="conv_rglru",
    )(proj, proj, conv_w, conv_b, w_r, b_r, w_i, b_i, lam)


def _gla_kernel(q_ref, k_ref, v_ref, gg_ref, la_ref, ng_ref, o_ref, st_ref, *, pad, tt, heads):
    i = pl.program_id(1)

    @pl.when(i == 0)
    def _():
        st_ref[...] = jnp.zeros_like(st_ref)

    r = lax.broadcasted_iota(jnp.int32, (GLA_CHUNK, GLA_CHUNK), 0)
    c = lax.broadcasted_iota(jnp.int32, (GLA_CHUNK, GLA_CHUNK), 1)
    tri = (r >= c).astype(F32)
    for ci in range(tt // GLA_CHUNK):
        rows = slice(ci * GLA_CHUNK, (ci + 1) * GLA_CHUNK)
        pos = i * tt + ci * GLA_CHUNK + lax.broadcasted_iota(jnp.int32, (GLA_CHUNK, 1), 0)
        valid = pos >= pad
        for hd in range(heads):
            ks = slice(hd * HEAD_DIM, (hd + 1) * HEAD_DIM)
            vs = slice(hd * GLA_DV, (hd + 1) * GLA_DV)
            cs = jnp.dot(tri, la_ref[rows, ks], preferred_element_type=F32, precision=HIGHEST)
            cl = cs[GLA_CHUNK - 1:GLA_CHUNK]
            kdec = jnp.where(valid, k_ref[rows, ks].astype(F32) * jnp.exp(cl - cs), 0.0).astype(BF16)
            ut = lax.dot_general(v_ref[rows, vs], kdec, (((0,), (0,)), ((), ())), preferred_element_type=F32)
            st = st_ref[hd] * jnp.exp(cl) + ut
            st_ref[hd] = st
            o = _dot_nt(q_ref[rows, ks], st.astype(BF16))
            o = o * lax.rsqrt(jnp.mean(o * o, axis=-1, keepdims=True) + RMS_EPS) * ng_ref[:, vs]
            gg = gg_ref[rows, vs].astype(F32)
            o_ref[rows, vs] = (o * (gg * _sigmoid(gg))).astype(o_ref.dtype)


def _gla(proj, proj_g, la, norm_g, *, B, lp, kw, vw, off_q, off_k, off_v, off_g, pad, tt):
    T = proj.shape[0]
    nt = lp // tt
    heads = kw // HEAD_DIM
    return pl.pallas_call(
        functools.partial(_gla_kernel, pad=pad, tt=tt, heads=heads),
        grid=(B, nt),
        in_specs=[
            pl.BlockSpec((tt, kw), lambda b, i: (b * nt + i, off_q // kw)),
            pl.BlockSpec((tt, kw), lambda b, i: (b * nt + i, off_k // kw)),
            pl.BlockSpec((tt, vw), lambda b, i: (b * nt + i, off_v // vw)),
            pl.BlockSpec((tt, vw), lambda b, i: (b * nt + i, off_g // vw)),
            pl.BlockSpec((tt, kw), lambda b, i: (b * nt + i, 0)),
            pl.BlockSpec((1, vw), lambda b, i: (0, 0)),
        ],
        out_specs=pl.BlockSpec((tt, vw), lambda b, i: (b * nt + i, 0)),
        out_shape=jax.ShapeDtypeStruct((T, vw), BF16),
        scratch_shapes=[pltpu.VMEM((heads, GLA_DV, HEAD_DIM), F32)],
        compiler_params=_params(2),
        name="gla_chunked",
    )(proj, proj, proj, proj_g, la, norm_g)


def _merge_kernel(of_ref, ol_ref, og_ref, w_ref, g0_ref, g1_ref, g2_ref, o_ref, wbf_ref, *, fw, lw):
    @pl.when(pl.program_id(1) == 0)
    def _():
        wbf_ref[...] = w_ref[...].astype(BF16)

    y0 = jnp.dot(of_ref[...], wbf_ref[0:fw, :], preferred_element_type=F32)
    y1 = jnp.dot(ol_ref[...], wbf_ref[fw:fw + lw, :], preferred_element_type=F32)
    y2 = jnp.dot(og_ref[...], wbf_ref[fw + lw:, :], preferred_element_type=F32)
    out = (_sigmoid(g0_ref[...].astype(F32)) * y0 + _sigmoid(g1_ref[...].astype(F32)) * y1
           + _sigmoid(g2_ref[...].astype(F32)) * y2)
    o_ref[...] = out.astype(o_ref.dtype)


def _merge(o_fox, o_lru, o_gla, w_branch, layer, proj, *, off_gates, d):
    T, fw = o_fox.shape
    lw = o_lru.shape[1]
    vw = o_gla.shape[1]
    tm = _pick(T, (1056, 1024, 768, 512, 384, 256, 128))
    tn = _pick(math.gcd(d, off_gates), (512, 256, 128))
    gate_spec = lambda b: pl.BlockSpec((tm, tn), lambda j, i: (i, (off_gates + b * d) // tn + j))
    return pl.pallas_call(
        functools.partial(_merge_kernel, fw=fw, lw=lw),
        grid=(d // tn, T // tm),
        in_specs=[
            pl.BlockSpec((tm, fw), lambda j, i: (i, 0)),
            pl.BlockSpec((tm, lw), lambda j, i: (i, 0)),
            pl.BlockSpec((tm, vw), lambda j, i: (i, 0)),
            pl.BlockSpec((None, fw + lw + vw, tn), lambda j, i: (layer, 0, j)),
            gate_spec(0), gate_spec(1), gate_spec(2),
        ],
        out_specs=pl.BlockSpec((tm, tn), lambda j, i: (i, j)),
        out_shape=jax.ShapeDtypeStruct((T, d), BF16),
        scratch_shapes=[pltpu.VMEM((fw + lw + vw, tn), BF16)],
        compiler_params=_params(2),
        name="branch_merge",
    )(o_fox, o_lru, o_gla, w_branch, proj, proj, proj)


def _ln_router_kernel(h_ref, mix_ref, g_ref, b_ref, wr_ref, br_ref, hf_ref, hx_ref, idx_ref, wt_ref, *, alpha):
    y = _layer_norm(alpha * h_ref[...] + mix_ref[...], g_ref[...], b_ref[...])
    hf_ref[...] = y
    hx_ref[...] = _pack_halves(y)
    w = wr_ref[...]
    y_hi, w_hi = y.astype(BF16), w.astype(BF16)
    y_lo = (y - y_hi.astype(F32)).astype(BF16)
    w_lo = (w - w_hi.astype(F32)).astype(BF16)
    logits = (jnp.dot(y_hi, w_hi, preferred_element_type=F32)
              + (jnp.dot(y_lo, w_hi, preferred_element_type=F32) + jnp.dot(y_hi, w_lo, preferred_element_type=F32))
              + br_ref[...])
    lane = lax.broadcasted_iota(jnp.int32, logits.shape, 1).astype(F32)
    idx = jnp.zeros_like(logits)
    vals = []
    for k in range(TOP_K):
        mx = jnp.max(logits, axis=-1, keepdims=True)
        sel = jnp.min(jnp.where(logits == mx, lane, float(logits.shape[1])), axis=-1, keepdims=True)
        vals.append(mx)
        idx = jnp.where(lane == float(k), sel, idx)
        logits = jnp.where(lane == sel, -jnp.inf, logits)
    es = [jnp.exp(v - vals[0]) for v in vals]
    tot = es[0]
    for e in es[1:]:
        tot = tot + e
    wt = jnp.zeros_like(logits)
    for k in range(TOP_K):
        wt = jnp.where(lane == float(k), es[k] / tot, wt)
    idx_ref[...] = idx.astype(jnp.int32)
    wt_ref[...] = wt


def _ln_router(h, mix, g, b, wr_pad, br_pad, *, alpha):
    T, D = h.shape
    tm = _pick(T, (192, 128))
    blk = pl.BlockSpec((tm, D), lambda i: (i, 0))
    row = pl.BlockSpec((1, D), lambda i: (0, 0))
    small = pl.BlockSpec((tm, 128), lambda i: (i, 0))
    return pl.pallas_call(
        functools.partial(_ln_router_kernel, alpha=alpha),
        grid=(T // tm,),
        in_specs=[blk, blk, row, row, pl.BlockSpec((D, 128), lambda i: (0, 0)), pl.BlockSpec((1, 128), lambda i: (0, 0))],
        out_specs=[blk, pl.BlockSpec((tm, D // 2), lambda i: (i, 0)), small, small],
        out_shape=[
            jax.ShapeDtypeStruct((T, D), F32),
            jax.ShapeDtypeStruct((T, D // 2), U32),
            jax.ShapeDtypeStruct((T, 128), jnp.int32),
            jax.ShapeDtypeStruct((T, 128), F32),
        ],
        compiler_params=_params(1),
        name="ln_router",
    )(h, mix, g, b, wr_pad, br_pad)


def _pack_halves(y):
    half = y.shape[1] // 2
    lo = pltpu.bitcast(y[:, :half].astype(BF16).astype(F32), U32)
    hi = pltpu.bitcast(y[:, half:].astype(BF16).astype(F32), U32)
    return lax.shift_right_logical(lo, jnp.uint32(16)) | (hi & jnp.uint32(0xFFFF0000))


def _unpack_halves(w):
    lo = pltpu.bitcast(lax.shift_left(w, jnp.uint32(16)), F32)
    hi = pltpu.bitcast(w & jnp.uint32(0xFFFF0000), F32)
    return lo, hi


def _row_gather(src_hbm, idx_ref, base, buf_ref, sem, n, static=False):
    def start(r):
        pltpu.make_async_copy(src_hbm.at[pl.ds(idx_ref[base + r], 1)], buf_ref.at[pl.ds(r, 1)], sem).start()

    if static:
        for r in range(n):
            start(r)
    else:
        def body(r, carry):
            start(r)
            return carry

        lax.fori_loop(0, n, body, 0, unroll=8)


def _tile_wait(src_hbm, buf_ref, sem):
    pltpu.make_async_copy(src_hbm.at[pl.ds(0, buf_ref.shape[0])], buf_ref, sem).wait()


def _moe_kernel(te_ref, na_ref, first_ref, nxt_ref, tok_ref, h_hbm, wup_hbm, wdn_hbm, bup_ref, sel_ref, bdn_ref,
                y_ref, xbuf_ref, sem_ref, sup_ref, sdn_ref, wup_ref, wdn_ref, wsem_ref, *, tmx, layer):
    i = pl.program_id(0)
    na = na_ref[0]
    cur = lax.rem(i, GATHER_RING)
    ahead = lax.rem(i + GATHER_RING - 1, GATHER_RING)

    def weight_copies(e):
        return (pltpu.make_async_copy(wup_hbm.at[layer, e], sup_ref, wsem_ref.at[0]),
                pltpu.make_async_copy(wdn_hbm.at[layer, e], sdn_ref, wsem_ref.at[1]))

    def start_weights(e):
        up_copy, dn_copy = weight_copies(e)
        up_copy.start(priority=1)
        dn_copy.start()

    for t in range(GATHER_RING - 1):
        @pl.when(jnp.logical_and(i == 0, na > t))
        def _():
            _row_gather(h_hbm, tok_ref, t * tmx, xbuf_ref.at[t], sem_ref.at[t], tmx)

    @pl.when(jnp.logical_and(i == 0, na > 0))
    def _():
        start_weights(te_ref[0])

    @pl.when(jnp.logical_and(i < na, first_ref[i] == 1))
    def _():
        up_copy, dn_copy = weight_copies(te_ref[i])
        up_copy.wait()
        _cast_rows(sup_ref, wup_ref, 256)
        dn_copy.wait()
        _cast_rows(sdn_ref, wdn_ref, 128)

        @pl.when(nxt_ref[i] >= 0)
        def _():
            start_weights(nxt_ref[i])

    @pl.when(i < na)
    def _():
        _tile_wait(h_hbm, xbuf_ref.at[cur], sem_ref.at[cur])
        lo, hi = _unpack_halves(xbuf_ref[cur])
        x = jnp.concatenate([lo.astype(BF16), hi.astype(BF16)], axis=1)
        h = jnp.dot(x, wup_ref[...], preferred_element_type=F32) + bup_ref[...]
        g = jnp.minimum(h, SWIGLU_LIMIT)
        u = jnp.clip(pltpu.roll(h, h.shape[1] - 1, axis=1), -SWIGLU_LIMIT, SWIGLU_LIMIT)
        act = ((u + 1.0) * g * _sigmoid(SWIGLU_ALPHA * g)).astype(BF16)
        act = jnp.dot(act, sel_ref[...], preferred_element_type=F32).astype(BF16)
        y = jnp.dot(act, wdn_ref[...], preferred_element_type=F32) + bdn_ref[...]
        y_ref[...] = _pack_halves(y)

    @pl.when(i + GATHER_RING - 1 < na)
    def _():
        _row_gather(h_hbm, tok_ref, (i + GATHER_RING - 1) * tmx, xbuf_ref.at[ahead], sem_ref.at[ahead], tmx,
                    static=True)

    @pl.when(i >= na)
    def _():
        y_ref[...] = jnp.zeros_like(y_ref)


def _moe(tile_expert, n_active, row_token, hx, w_up, b_up, sel, w_down, b_down, *, tmx, layer):
    T, D = hx.shape[0], 2 * hx.shape[1]
    _, E, _, F2 = w_up.shape
    F = F2 // 2
    P = row_token.shape[0]
    ntiles = P // tmx
    tiles = jnp.arange(ntiles, dtype=jnp.int32)
    first = jnp.where(tiles == 0, 1, (tile_expert != jnp.roll(tile_expert, 1)).astype(jnp.int32))
    j = jnp.sum((tile_expert[None, :] <= tile_expert[:, None]).astype(jnp.int32), axis=1)
    nxt = jnp.sum(jnp.where(tiles[None, :] == j[:, None], tile_expert[None, :] + 1, 0), axis=1) - 1

    return pl.pallas_call(
        functools.partial(_moe_kernel, tmx=tmx, layer=layer),
        grid_spec=pltpu.PrefetchScalarGridSpec(
            num_scalar_prefetch=5,
            grid=(ntiles,),
            in_specs=[
                pl.BlockSpec(memory_space=pl.ANY),
                pl.BlockSpec(memory_space=pl.ANY),
                pl.BlockSpec(memory_space=pl.ANY),
                pl.BlockSpec((None, 1, F2), lambda i, te, *_: (te[i], 0, 0)),
                pl.BlockSpec((F2, F), lambda i, *_: (0, 0)),
                pl.BlockSpec((None, 1, D), lambda i, te, *_: (te[i], 0, 0)),
            ],
            out_specs=pl.BlockSpec((tmx, D // 2), lambda i, *_: (i, 0)),
            scratch_shapes=[
                pltpu.VMEM((GATHER_RING, tmx, D // 2), U32), pltpu.SemaphoreType.DMA((GATHER_RING,)),
                pltpu.VMEM((D, F2), F32), pltpu.VMEM((F, D), F32),
                pltpu.VMEM((D, F2), BF16), pltpu.VMEM((F, D), BF16), pltpu.SemaphoreType.DMA((2,)),
            ],
        ),
        out_shape=jax.ShapeDtypeStruct((P, D // 2), U32),
        compiler_params=_params(1),
        name="moe_experts",
    )(tile_expert, n_active, first, nxt, row_token, hx, w_up, w_down, b_up, sel, b_down)


def _combine_kernel(pos_ref, y_hbm, h_ref, w_ref, g_ref, b_ref, *rest, alpha, tc, nt, final):
    if final:
        out_ref, buf_ref, sem_ref = rest
    else:
        hf_ref, hb_ref, buf_ref, sem_ref = rest
    i = pl.program_id(0)
    cur = lax.rem(i, GATHER_RING)
    ahead = lax.rem(i + GATHER_RING - 1, GATHER_RING)

    def start(tile, slot, static):
        for k in range(TOP_K):
            _row_gather(y_hbm, pos_ref, (k * nt + tile) * tc, buf_ref.at[slot, k], sem_ref.at[slot], tc, static=static)

    @pl.when(i == 0)
    def _():
        for t in range(min(GATHER_RING - 1, nt)):
            start(t, t, False)

    for k in range(TOP_K):
        _tile_wait(y_hbm, buf_ref.at[cur, k], sem_ref.at[cur])
    lo = hi = None
    for k in range(TOP_K):
        l, h = _unpack_halves(buf_ref[cur, k])
        wk = w_ref[:, k:k + 1]
        lo = wk * l if lo is None else lo + wk * l
        hi = wk * h if hi is None else hi + wk * h
    ffn = jnp.concatenate([lo, hi], axis=1)
    y = _layer_norm(alpha * h_ref[...] + ffn, g_ref[...], b_ref[...])
    if final:
        out_ref[...] = y
    else:
        hf_ref[...] = y
        hb_ref[...] = y.astype(BF16)

    @pl.when(i + GATHER_RING - 1 < nt)
    def _():
        start(i + GATHER_RING - 1, ahead, True)


def _combine(pos, ys, hf, top_w, g, b, *, alpha, lp, final):
    T, D = hf.shape
    tc = ROW_TILE
    nt = T // tc
    ntb = lp // tc
    if final:
        out_specs = pl.BlockSpec((tc, D), lambda i, pos: ((i // ntb) * (ntb - 1) + jnp.maximum(i % ntb - 1, 0), 0))
        out_shape = jax.ShapeDtypeStruct(((T // lp) * (lp - tc), D), F32)
    else:
        out_specs = [pl.BlockSpec((tc, D), lambda i, pos: (i, 0)), pl.BlockSpec((tc, D), lambda i, pos: (i, 0))]
        out_shape = [jax.ShapeDtypeStruct((T, D), F32), jax.ShapeDtypeStruct((T, D), BF16)]
    grid_spec = pltpu.PrefetchScalarGridSpec(
        num_scalar_prefetch=1,
        grid=(nt,),
        in_specs=[
            pl.BlockSpec(memory_space=pl.ANY),
            pl.BlockSpec((tc, D), lambda i, pos: (i, 0)),
            pl.BlockSpec((tc, 128), lambda i, pos: (i, 0)),
            pl.BlockSpec((1, D), lambda i, pos: (0, 0)),
            pl.BlockSpec((1, D), lambda i, pos: (0, 0)),
        ],
        out_specs=out_specs,
        scratch_shapes=[pltpu.VMEM((GATHER_RING, TOP_K, tc, D // 2), U32), pltpu.SemaphoreType.DMA((GATHER_RING,))],
    )
    return pl.pallas_call(
        functools.partial(_combine_kernel, alpha=alpha, tc=tc, nt=nt, final=final),
        grid_spec=grid_spec,
        out_shape=out_shape,
        compiler_params=_params(1),
        name="moe_combine_ln",
    )(pos, ys, hf, top_w, g, b)


def _route_kernel(idx_ref, pos_ref, te_ref, na_ref, cnt_ref, pst_ref, *, pad, tr, tmx, n_experts, spare_row):
    p = pl.program_id(0)
    i = pl.program_id(2)
    first = jnp.logical_and(pl.program_id(1) == 0, i == 0)

    @pl.when(jnp.logical_and(p == 0, first))
    def _():
        cnt_ref[...] = jnp.zeros_like(cnt_ref)

    @pl.when(jnp.logical_and(p == 1, first))
    def _():
        cnt = cnt_ref[...]
        padded = jnp.floor((cnt + (tmx - 1.0)) * (1.0 / tmx)) * tmx
        r = lax.broadcasted_iota(jnp.int32, (128, 128), 0)
        c = lax.broadcasted_iota(jnp.int32, (128, 128), 1)
        pst = jnp.dot(padded, (r < c).astype(F32), preferred_element_type=F32, precision=HIGHEST)
        pst_ref[...] = pst
        cnt_ref[...] = jnp.zeros_like(cnt_ref)
        tile_end = (pst + padded) * (1.0 / tmx)
        nt_pad = te_ref.shape[0]
        t = lax.broadcasted_iota(jnp.int32, (nt_pad, 128), 0).astype(F32)
        lane = lax.broadcasted_iota(jnp.int32, (nt_pad, 128), 1)
        is_expert = lane < n_experts
        te = jnp.sum(jnp.where(jnp.logical_and(tile_end <= t, is_expert), 1.0, 0.0), axis=-1, keepdims=True)
        lane1 = lax.broadcasted_iota(jnp.int32, (1, 128), 1).astype(F32)
        e_last = jnp.max(jnp.where(cnt > 0.0, lane1, 0.0), axis=-1, keepdims=True)
        te_ref[...] = jnp.broadcast_to(jnp.minimum(te, e_last), (nt_pad, 128)).astype(jnp.int32)
        na_ref[...] = jnp.broadcast_to(jnp.max(tile_end, axis=-1, keepdims=True), (1, 128)).astype(jnp.int32)

    idx = idx_ref[...]
    lane = lax.broadcasted_iota(jnp.int32, (tr, 128), 1)
    valid = (i * tr + lax.broadcasted_iota(jnp.int32, (tr, 1), 0)) >= pad
    onehot = jnp.zeros((tr, 128), F32)
    for k in range(TOP_K):
        onehot = onehot + jnp.where(lane == idx[:, k:k + 1], 1.0, 0.0)
    onehot = jnp.where(valid, onehot, 0.0)
    r = lax.broadcasted_iota(jnp.int32, (tr, tr), 0)
    c = lax.broadcasted_iota(jnp.int32, (tr, tr), 1)
    rank = jnp.dot((r > c).astype(BF16), onehot.astype(BF16), preferred_element_type=F32) + cnt_ref[...]
    cnt_ref[...] = cnt_ref[...] + jnp.sum(onehot, axis=0, keepdims=True)

    @pl.when(p == 0)
    def _():
        pos_ref[...] = jnp.zeros_like(pos_ref)

    @pl.when(p == 1)
    def _():
        dest = rank + pst_ref[...]
        out = jnp.zeros((tr, 128), F32)
        for k in range(TOP_K):
            d = jnp.sum(jnp.where(lane == idx[:, k:k + 1], dest, 0.0), axis=-1, keepdims=True)
            out = jnp.where(lane == k, jnp.where(valid, d, float(spare_row)), out)
        pos_ref[...] = out.astype(jnp.int32)


def _route(top_idx, *, B, lp, pad, tmx, n_experts, n_rows):
    T = top_idx.shape[0]
    tr = _pick(lp, (384, 256, 128))
    nt = lp // tr
    nt_pad = -(-(n_rows // tmx) // 8) * 8
    pos, te, na = pl.pallas_call(
        functools.partial(_route_kernel, pad=pad, tr=tr, tmx=tmx, n_experts=n_experts, spare_row=n_rows - 1),
        grid=(2, B, nt),
        in_specs=[pl.BlockSpec((tr, 128), lambda p, b, i: (b * nt + i, 0))],
        out_specs=[
            pl.BlockSpec((tr, 128), lambda p, b, i: (p * (b * nt + i), 0)),
            pl.BlockSpec((nt_pad, 128), lambda p, b, i: (0, 0)),
            pl.BlockSpec((1, 128), lambda p, b, i: (0, 0)),
        ],
        out_shape=[
            jax.ShapeDtypeStruct((T, 128), jnp.int32),
            jax.ShapeDtypeStruct((nt_pad, 128), jnp.int32),
            jax.ShapeDtypeStruct((1, 128), jnp.int32),
        ],
        scratch_shapes=[pltpu.VMEM((1, 128), F32), pltpu.VMEM((1, 128), F32)],
        compiler_params=_params(3),
        name="route_rank",
    )(top_idx)
    pos_flat = pos[:, :TOP_K].T.reshape(-1)
    return pos_flat, te[:n_rows // tmx, 0], na[0, :1]


def _row_token_kernel(pos_ref, zeros_hbm, rt_ref, sem_ref, *, t):
    fill = pltpu.make_async_copy(zeros_hbm, rt_ref, sem_ref.at[0])
    fill.start()
    fill.wait()
    for k in range(TOP_K):
        def body(tok, carry):
            rt_ref[pos_ref[k * t + tok]] = tok
            return carry

        lax.fori_loop(0, t, body, 0, unroll=8)


def _row_token(pos_flat, *, t, n_rows):
    return pl.pallas_call(
        functools.partial(_row_token_kernel, t=t),
        grid_spec=pltpu.PrefetchScalarGridSpec(
            num_scalar_prefetch=1, grid=(1,), in_specs=[pl.BlockSpec(memory_space=pl.ANY)],
            out_specs=pl.BlockSpec(memory_space=pltpu.SMEM),
            scratch_shapes=[pltpu.SemaphoreType.DMA((1,))]),
        out_shape=jax.ShapeDtypeStruct((n_rows,), jnp.int32),
        compiler_params=_params(1),
        name="route_row_token",
    )(pos_flat, jnp.zeros((n_rows,), jnp.int32))


def _segments(fw, fh, lw, kw, vw, rank, d):
    names = ("fq", "fk", "fv", "ff", "lx", "lg", "gq", "gk", "gv", "ga", "gg", "gate0", "gate1", "gate2")
    widths = (fw, fw, fw, fh, lw, lw, kw, kw, vw, rank, vw, d, d, d)
    segs, off = {}, 0
    for n, w in zip(names, widths):
        segs[n] = (off, w)
        off += w
    return segs, off


def kernel(x, meta_tokens, emb_ln_g, emb_ln_b, w_in, b_in, conv_w, conv_b, lru_w_r, lru_b_r, lru_w_i, lru_b_i,
           lru_lambda, gla_w_alpha, gla_b_alpha, gla_norm_g, w_branch, w_out, b_out, ln1_g, ln1_b, w_router,
           b_router, w_up, b_up, w_down, b_down, ln2_g, ln2_b):
    B, S, D = x.shape
    n_meta = meta_tokens.shape[0]
    depth = w_in.shape[0]
    L = S + n_meta
    pad = (-L) % ROW_TILE
    lp = L + pad
    assert pad + n_meta == ROW_TILE and S % ROW_TILE == 0
    T = B * lp

    lw = conv_w.shape[2]
    rank, kw = gla_w_alpha.shape[1:]
    vw = gla_norm_g.shape[1]
    n_experts = w_router.shape[2]
    fexp = w_up.shape[3] // 2
    in_cols = w_in.shape[2]
    fh = (in_cols - 2 * lw - 2 * kw - 2 * vw - rank - 3 * D) // (3 * HEAD_DIM + 1)
    fw = fh * HEAD_DIM
    segs, total = _segments(fw, fh, lw, kw, vw, rank, D)
    assert total == in_cols and fh <= 8 and fh + rank <= 128
    alpha = (2.0 * depth) ** 0.25

    windows = (("fq", "fk", "fv"), ("lx", "lg", "gq", "gk", "gv"), ("gg", "gate0", "gate1", "gate2"))
    qscale = HEAD_DIM ** -0.5
    off, win_start, win_scale = {}, [], []
    for names in windows:
        start = segs[names[0]][0]
        for n in names:
            off[n] = segs[n][0] - start
            assert off[n] % segs[n][1] == 0 or n.startswith("gate")
        win_start.append(start)
        scales = {"fq": qscale * LOG2E, "gq": qscale}
        win_scale.append(jnp.concatenate(
            [jnp.full((segs[n][1],), scales.get(n, 1.0), F32) for n in names]).reshape(1, -1))

    def cols(a, n):
        o, w = segs[n]
        return a[..., o:o + w]

    tt = _pick(lp, (384, 256, 128))
    tmx = MOE_ROW_TILE if T >= 4096 else 128
    n_rows = (-(-(B * L * TOP_K + n_experts * (tmx - 1)) // tmx) + 1) * tmx

    head = jnp.concatenate([jnp.zeros((pad, D), F32), meta_tokens.astype(F32)], axis=0)
    hf, hb = _embed_ln(x, head, emb_ln_g.reshape(1, D), emb_ln_b.reshape(1, D), pad=pad, lp=lp)
    sel = (jnp.arange(2 * fexp)[:, None] == 2 * jnp.arange(fexp)[None, :]).astype(BF16)

    w_in_t = jnp.swapaxes(w_in, 1, 2)

    def rows(l, n):
        o, w = segs[n]
        return w_in_t[l, o:o + w]

    for l in range(depth):
        w_small = jnp.concatenate(
            [rows(l, "ff"), jnp.zeros((8 - fh, D), F32), rows(l, "ga"), jnp.zeros((128 - 8 - rank, D), F32)], axis=0)
        b_small = jnp.concatenate(
            [cols(b_in[l], "ff"), jnp.zeros((8 - fh,), F32), cols(b_in[l], "ga"),
             jnp.zeros((128 - 8 - rank,), F32)]).reshape(1, 128)
        wa_ext = jnp.zeros((128, kw), F32).at[8:8 + rank].set(gla_w_alpha[l])

        pa, pb, pc = [
            _in_proj(hb, w_in_t, l, b_in[l][s:s + sc.shape[1]].reshape(1, -1), sc, start=s, name=f"in_proj_{k}")
            for k, (s, sc) in enumerate(zip(win_start, win_scale))]
        crow, la = _prep(hb, w_small, b_small, wa_ext, gla_b_alpha[l].reshape(1, kw), B=B, lp=lp, pad=pad, tt=tt)
        o_fox = _fox(pa, crow, B=B, lp=lp, fw=fw, off_q=off["fq"], off_k=off["fk"], off_v=off["fv"], tq=FOX_Q_TILE,
                     tk=tt)
        o_lru = _lru(pb, conv_w[l], conv_b[l].reshape(1, lw), lru_w_r[l], lru_b_r[l].reshape(1, lw), lru_w_i[l],
                     lru_b_i[l].reshape(1, lw), lru_lambda[l].reshape(1, lw), B=B, lp=lp, lw=lw, off_x=off["lx"],
                     off_g=off["lg"], pad=pad, tt=tt)
        o_gla = _gla(pb, pc, la, gla_norm_g[l].reshape(1, vw), B=B, lp=lp, kw=kw, vw=vw, off_q=off["gq"],
                     off_k=off["gk"], off_v=off["gv"], off_g=off["gg"], pad=pad, tt=tt)
        merged = _merge(o_fox, o_lru, o_gla, w_branch, l, pc, off_gates=off["gate0"], d=D)
        mix = _matmul(merged, w_out, l, b_out[l].reshape(1, D), jnp.ones((1, D), F32), F32, "out_proj")

        wr_pad = jnp.zeros((D, 128), F32).at[:, :n_experts].set(w_router[l])
        br_pad = jnp.full((1, 128), -MASKED_KEY_BIAS, F32).at[0, :n_experts].set(b_router[l])
        hf, hx, top_idx, top_w = _ln_router(hf, mix, ln1_g[l].reshape(1, D), ln1_b[l].reshape(1, D), wr_pad, br_pad,
                                            alpha=alpha)
        pos, tile_expert, n_active = _route(top_idx, B=B, lp=lp, pad=pad, tmx=tmx, n_experts=n_experts, n_rows=n_rows)
        row_token = _row_token(pos, t=T, n_rows=n_rows)
        ys = _moe(tile_expert, n_active, row_token, hx, w_up, b_up[l].reshape(n_experts, 1, 2 * fexp), sel, w_down,
                  b_down[l].reshape(n_experts, 1, D), tmx=tmx, layer=l)
        final = l == depth - 1
        res = _combine(pos, ys, hf, top_w, ln2_g[l].reshape(1, D), ln2_b[l].reshape(1, D), alpha=alpha, lp=lp,
                       final=final)
        if final:
            return res.reshape(B, S, D)
        hf, hb = res
```
